```python
import jax
import jax.numpy as jnp
from jax import lax
import numpy as np

D_MODEL = 2048
BATCH = 16
SEQ = 2048
DEPTH = 4
DEC_BATCH = 16
DEC_SEQ = 32
PAST_LEN = 4096

CHUNK = 64
N_META = 16
N_A = DEPTH // 2
N_B = DEPTH - N_A
A_INNER = 2 * D_MODEL
A_HEADS = 8
A_HEAD_DIM = A_INNER // A_HEADS
CONV_W = 4
B_HEADS = 16
B_HEAD_DIM = D_MODEL // B_HEADS
B_WIDTH = B_HEADS * B_HEAD_DIM
Q_BLOCK = 128
EPS = 1e-6

kernel_name = "yoco_mlstm_fox_streaming_step"


def rmsnorm(x, g):
    xf = x.astype(jnp.float32)
    y = xf * lax.rsqrt(jnp.mean(jnp.square(xf), axis=-1, keepdims=True) + EPS)
    return (y * g.astype(jnp.float32)).astype(x.dtype)


def head_layernorm(h, g):
    hc = h - jnp.mean(h, axis=-1, keepdims=True)
    return hc * lax.rsqrt(jnp.mean(hc * hc, axis=-1, keepdims=True) + EPS) * g


def causal_conv(x, buf, w, b):
    seq = x.shape[1]
    xp = jnp.concatenate([buf.astype(x.dtype), x], axis=1)
    y = b
    for i in range(CONV_W):
        y = y + xp[:, i:i + seq] * w[i]
    return y, xp[:, -(CONV_W - 1):]


def mlstm_block(state, blk):
    C, n, m = state
    q, k, v, ig, lf = blk
    L = q.shape[2]
    b = jnp.cumsum(lf, axis=-1)
    log_d = b[..., :, None] - b[..., None, :] + ig[..., None, :]
    log_d = jnp.where(jnp.tril(jnp.ones((L, L), dtype=bool)), log_d, -jnp.inf)
    inter = b + m[..., None]
    m_t = jnp.maximum(jnp.max(log_d, axis=-1), inter)
    d = jnp.exp(log_d - m_t[..., None])
    a = jnp.exp(inter - m_t)
    s = jnp.einsum('bhtd,bhsd->bhts', q, k) * d
    num = jnp.einsum('bhts,bhse->bhte', s, v) + a[..., None] * jnp.einsum('bhtd,bhde->bhte', q, C)
    den = jnp.sum(s, axis=-1) + a * jnp.einsum('bhtd,bhd->bht', q, n)
    h = num / jnp.maximum(jnp.abs(den), jnp.exp(-m_t))[..., None]
    m_new = m_t[..., -1]
    decay = jnp.exp(b[..., -1] + m - m_new)
    w = jnp.exp(b[..., -1:] - b + ig - m_new[..., None])
    C_new = decay[..., None, None] * C + jnp.einsum('bhs,bhsd,bhse->bhde', w, k, v)
    n_new = decay[..., None] * n + jnp.einsum('bhs,bhsd->bhd', w, k)
    return (C_new, n_new, m_new), h


def mlstm_run(q, k, v, ig, lf, state, lead):
    if lead == 0:
        state, h = mlstm_block(state, (q, k, v, ig, lf))
        return h, state
    bsz, nh, seq = q.shape[:3]
    state, h0 = mlstm_block(state, (q[:, :, :lead], k[:, :, :lead], v[:, :, :lead], ig[:, :, :lead], lf[:, :, :lead]))
    nc = (seq - lead) // CHUNK

    def to_blocks(t):
        t = t[:, :, lead:]
        return jnp.moveaxis(t.reshape((bsz, nh, nc, CHUNK) + t.shape[3:]), 2, 0)

    state, hs = lax.scan(mlstm_block, state, (to_blocks(q), to_blocks(k), to_blocks(v), to_blocks(ig), to_blocks(lf)))
    hs = jnp.moveaxis(hs, 0, 2).reshape(bsz, nh, seq - lead, A_HEAD_DIM)
    return jnp.concatenate([h0, hs], axis=2), state


def mlstm_layer(x, conv_buf, C, n, m, norm_g, w_in, conv_w, conv_b, wq, wk, wv, w_gate, b_gate, out_g, skip, w_out, lead):
    bsz, seq, _ = x.shape
    xm, z = jnp.split(rmsnorm(x, norm_g) @ w_in, 2, axis=-1)
    xc, new_buf = causal_conv(xm, conv_buf, conv_w, conv_b)
    xc = jax.nn.silu(xc)
    xc_h = xc.reshape(bsz, seq, A_HEADS, A_HEAD_DIM)
    xm_h = xm.reshape(bsz, seq, A_HEADS, A_HEAD_DIM)
    q = jnp.einsum('blhd,hde->blhe', xc_h, wq)
    k = jnp.einsum('blhd,hde->blhe', xc_h, wk) * (A_HEAD_DIM ** -0.5)
    v = jnp.einsum('blhd,hde->blhe', xm_h, wv)
    g_in = jnp.concatenate([q.reshape(bsz, seq, A_INNER), k.reshape(bsz, seq, A_INNER), v.reshape(bsz, seq, A_INNER)], axis=-1)
    gates = (g_in @ w_gate + b_gate).astype(jnp.float32)
    ig = gates[..., :A_HEADS]
    lf = jax.nn.log_sigmoid(gates[..., A_HEADS:])

    def to_bh(t):
        return jnp.moveaxis(t.astype(jnp.float32), 2, 1)

    h, (C, n, m) = mlstm_run(to_bh(q), to_bh(k), to_bh(v), to_bh(ig), to_bh(lf),
                             (C.astype(jnp.float32), n.astype(jnp.float32), m.astype(jnp.float32)), lead)
    h = head_layernorm(jnp.moveaxis(h, 1, 2), out_g.astype(jnp.float32)).reshape(bsz, seq, A_INNER)
    h = (h + skip * xc).astype(x.dtype)
    y = (h * jax.nn.silu(z)) @ w_out
    return x + y.astype(x.dtype), new_buf, C, n, m


def shared_kv(x, kv_norm_g, w_kvf, b_f, k_norm_g):
    bsz, seq, _ = x.shape
    kvf = rmsnorm(x, kv_norm_g) @ w_kvf
    k = rmsnorm(kvf[..., :B_WIDTH].reshape(bsz, seq, B_HEADS, B_HEAD_DIM), k_norm_g)
    v = kvf[..., B_WIDTH:2 * B_WIDTH].reshape(bsz, seq, B_HEADS, B_HEAD_DIM)
    logf = jax.nn.log_sigmoid(kvf[..., 2 * B_WIDTH:].astype(jnp.float32) + b_f.astype(jnp.float32))
    return k, v, logf


def fox_attend(q, f_q, q_pos, k, v, f_k, k_pos):
    s = jnp.einsum('bqhd,bkhd->bhqk', q, k).astype(jnp.float32) * (B_HEAD_DIM ** -0.5)
    s = s + jnp.transpose(f_q, (0, 2, 1))[..., :, None] - jnp.transpose(f_k, (0, 2, 1))[..., None, :]
    s = jnp.where(k_pos[None, :] <= q_pos[:, None], s, -jnp.inf)
    p = jax.nn.softmax(s, axis=-1)
    return jnp.einsum('bhqk,bkhd->bqhd', p.astype(v.dtype), v)


def fox_layer(x, q_pos, f_q, k, v, f_k, k_pos, norm_g, w_in, q_norm_g, w_out, blocked):
    bsz, seq, _ = x.shape
    q, z = jnp.split(rmsnorm(x, norm_g) @ w_in, 2, axis=-1)
    q = rmsnorm(q.reshape(bsz, seq, B_HEADS, B_HEAD_DIM), q_norm_g)
    if blocked:
        nb = seq // Q_BLOCK
        qb = jnp.swapaxes(q.reshape(bsz, nb, Q_BLOCK, B_HEADS, B_HEAD_DIM), 0, 1)
        fb = jnp.swapaxes(f_q.reshape(bsz, nb, Q_BLOCK, B_HEADS), 0, 1)
        pb = q_pos.reshape(nb, Q_BLOCK)
        o = lax.map(lambda blk: fox_attend(blk[0], blk[1], blk[2], k, v, f_k, k_pos), (qb, fb, pb))
        o = jnp.swapaxes(o, 0, 1).reshape(bsz, seq, B_WIDTH)
    else:
        o = fox_attend(q, f_q, q_pos, k, v, f_k, k_pos).reshape(bsz, seq, B_WIDTH)
    y = (o * jax.nn.silu(z)) @ w_out
    return x + y.astype(x.dtype)


def setup_inputs(seed: int = 0) -> dict:
    key = jax.random.key(seed)
    ks = jax.random.split(key, 32)
    f32 = jnp.float32

    def nrm(k, shape, scale):
        return jax.random.normal(k, shape, f32) * scale

    a_b_gate = jnp.concatenate([0.1 * nrm(ks[17], (N_A, A_HEADS), 1.0),
                                jnp.linspace(3.0, 6.0, A_HEADS, dtype=f32)[None] + nrm(ks[18], (N_A, A_HEADS), 0.1)], axis=-1)
    return {
        "x_prompt": nrm(ks[0], (BATCH, SEQ, D_MODEL), 1.0),
        "x_sample": nrm(ks[1], (DEC_BATCH, DEC_SEQ, D_MODEL), 1.0),
        "cache_k": nrm(ks[2], (DEC_BATCH, PAST_LEN, B_HEADS, B_HEAD_DIM), 1.0),
        "cache_v": nrm(ks[3], (DEC_BATCH, PAST_LEN, B_HEADS, B_HEAD_DIM), 1.0),
        "cache_logf": jax.nn.log_sigmoid(3.0 + nrm(ks[4], (DEC_BATCH, PAST_LEN, B_HEADS), 1.0)),
        "state_C": nrm(ks[5], (N_A, DEC_BATCH, A_HEADS, A_HEAD_DIM, A_HEAD_DIM), A_HEAD_DIM ** -0.5),
        "state_n": nrm(ks[6], (N_A, DEC_BATCH, A_HEADS, A_HEAD_DIM), A_HEAD_DIM ** -0.5),
        "state_m": nrm(ks[7], (N_A, DEC_BATCH, A_HEADS), 1.0),
        "state_conv": nrm(ks[8], (N_A, DEC_BATCH, CONV_W - 1, A_INNER), 1.0),
        "meta_tokens": nrm(ks[9], (N_META, D_MODEL), 1.0),
        "a_norm_g": 1.0 + nrm(ks[10], (N_A, D_MODEL), 0.02),
        "a_w_in": nrm(ks[11], (N_A, D_MODEL, 2 * A_INNER), D_MODEL ** -0.5),
        "a_conv_w": nrm(ks[12], (N_A, CONV_W, A_INNER), CONV_W ** -0.5),
        "a_conv_b": nrm(ks[13], (N_A, A_INNER), 0.02),
        "a_wq": nrm(ks[14], (N_A, A_HEADS, A_HEAD_DIM, A_HEAD_DIM), A_HEAD_DIM ** -0.5),
        "a_wk": nrm(ks[15], (N_A, A_HEADS, A_HEAD_DIM, A_HEAD_DIM), A_HEAD_DIM ** -0.5),
        "a_wv": nrm(ks[16], (N_A, A_HEADS, A_HEAD_DIM, A_HEAD_DIM), A_HEAD_DIM ** -0.5),
        "a_w_gate": nrm(ks[19], (N_A, 3 * A_INNER, 2 * A_HEADS), (3 * A_INNER) ** -0.5),
        "a_b_gate": a_b_gate,
        "a_out_g": 1.0 + nrm(ks[20], (N_A, A_HEADS, A_HEAD_DIM), 0.02),
        "a_skip": 1.0 + nrm(ks[21], (N_A, A_INNER), 0.02),
        "a_w_out": nrm(ks[22], (N_A, A_INNER, D_MODEL), A_INNER ** -0.5),
        "kv_norm_g": 1.0 + nrm(ks[23], (D_MODEL,), 0.02),
        "w_kvf": nrm(ks[24], (D_MODEL, 2 * B_WIDTH + B_HEADS), D_MODEL ** -0.5),
        "b_f": jnp.linspace(2.0, 6.0, B_HEADS, dtype=f32) + nrm(ks[25], (B_HEADS,), 0.1),
        "k_norm_g": 1.0 + nrm(ks[26], (B_HEADS, B_HEAD_DIM), 0.02),
        "b_norm_g": 1.0 + nrm(ks[27], (N_B, D_MODEL), 0.02),
        "b_w_in": nrm(ks[28], (N_B, D_MODEL, 2 * B_WIDTH), D_MODEL ** -0.5),
        "q_norm_g": 1.0 + nrm(ks[29], (N_B, B_HEADS, B_HEAD_DIM), 0.02),
        "b_w_out": nrm(ks[30], (N_B, B_WIDTH, D_MODEL), B_WIDTH ** -0.5),
    }


def reference(x_prompt, x_sample, cache_k, cache_v, cache_logf, state_C, state_n, state_m, state_conv,
              meta_tokens, a_norm_g, a_w_in, a_conv_w, a_conv_b, a_wq, a_wk, a_wv, a_w_gate, a_b_gate,
              a_out_g, a_skip, a_w_out, kv_norm_g, w_kvf, b_f, k_norm_g, b_norm_g, b_w_in, q_norm_g, b_w_out):
    bp = x_prompt.shape[0]
    past = cache_k.shape[1]
    dec = x_sample.shape[1]
    f32 = jnp.float32
    xp = jnp.concatenate([jnp.broadcast_to(meta_tokens.astype(x_prompt.dtype)[None], (bp, N_META, D_MODEL)), x_prompt], axis=1)
    xs = x_sample
    p_C, p_n, p_m, p_conv = [], [], [], []
    s_C, s_n, s_m, s_conv = [], [], [], []
    for layer in range(DEPTH):
        if layer < N_A:
            i = layer
            w = (a_norm_g[i], a_w_in[i], a_conv_w[i], a_conv_b[i], a_wq[i], a_wk[i], a_wv[i],
                 a_w_gate[i], a_b_gate[i], a_out_g[i], a_skip[i], a_w_out[i])
            xp, buf, C, n, m = mlstm_layer(
                xp, jnp.zeros((bp, CONV_W - 1, A_INNER), xp.dtype),
                jnp.zeros((bp, A_HEADS, A_HEAD_DIM, A_HEAD_DIM), f32), jnp.zeros((bp, A_HEADS, A_HEAD_DIM), f32),
                jnp.zeros((bp, A_HEADS), f32), *w, N_META)
            p_C.append(C); p_n.append(n); p_m.append(m); p_conv.append(buf)
            xs, buf, C, n, m = mlstm_layer(xs, state_conv[i], state_C[i], state_n[i], state_m[i], *w, 0)
            s_C.append(C); s_n.append(n); s_m.append(m); s_conv.append(buf)
            if layer == N_A - 1:
                p_k, p_v, p_logf = shared_kv(xp, kv_norm_g, w_kvf, b_f, k_norm_g)
                s_k, s_v, s_logf = shared_kv(xs, kv_norm_g, w_kvf, b_f, k_norm_g)
                fp = jnp.cumsum(p_logf, axis=1)
                kpos_p = jnp.arange(fp.shape[1])
                qpos_p = jnp.arange(N_META, fp.shape[1])
                ks_all = jnp.concatenate([cache_k.astype(s_k.dtype), s_k], axis=1)
                vs_all = jnp.concatenate([cache_v.astype(s_v.dtype), s_v], axis=1)
                fs = jnp.cumsum(jnp.concatenate([cache_logf.astype(f32), s_logf], axis=1), axis=1)
                kpos_s = jnp.arange(past + dec)
                qpos_s = jnp.arange(past, past + dec)
                xp = xp[:, N_META:]
        else:
            j = layer - N_A
            xp = fox_layer(xp, qpos_p, fp[:, N_META:], p_k, p_v, fp, kpos_p,
                           b_norm_g[j], b_w_in[j], q_norm_g[j], b_w_out[j], True)
            xs = fox_layer(xs, qpos_s, fs[:, past:], ks_all, vs_all, fs, kpos_s,
                           b_norm_g[j], b_w_in[j], q_norm_g[j], b_w_out[j], False)
    p_C = jnp.stack(p_C); p_n = jnp.stack(p_n); p_m = jnp.stack(p_m); p_conv = jnp.stack(p_conv)
    s_C = jnp.stack(s_C); s_n = jnp.stack(s_n); s_m = jnp.stack(s_m); s_conv = jnp.stack(s_conv)
    return (xp, xs, p_k, p_v, p_logf, p_C, p_n, p_m, p_conv, s_k, s_v, s_logf, s_C, s_n, s_m, s_conv)
```

```python
import functools

import jax
import jax.numpy as jnp
from jax import lax
from jax.experimental import pallas as pl
from jax.experimental.pallas import tpu as pltpu

F32 = jnp.float32
BF16 = jnp.bfloat16
EPS = 1e-6
CONV_PAD = 8
BF16_ROWS = 16
VMEM_LIMIT_BYTES = 56 * 1024 * 1024
HIGHEST = lax.Precision.HIGHEST
NEG_INF = float("-inf")


def _params(*sem):
    return pltpu.CompilerParams(dimension_semantics=sem, vmem_limit_bytes=VMEM_LIMIT_BYTES)


def _tile(n, cap, mult=BF16_ROWS):
    best = None
    for t in range(mult, min(n, cap) + 1, mult):
        if n % t == 0:
            best = t
    return best if best is not None else n


def _dot(a, b):
    return jnp.dot(a, b, preferred_element_type=F32)


def _dot_nt(a, b):
    return lax.dot_general(a, b, (((1,), (1,)), ((), ())), preferred_element_type=F32)


def _dot_tn(a, b):
    return lax.dot_general(a, b, (((0,), (0,)), ((), ())), preferred_element_type=F32)


def _silu(x):
    return x / (1.0 + jnp.exp(-x))


def _log_sigmoid(x):
    return jnp.minimum(x, 0.0) - jnp.log1p(jnp.exp(-jnp.abs(x)))


def _norm_matmul_kernel(*refs, n_out, n_head_blocks, head_dim, has_gain, has_gate):
    x_ref, g_ref, w_ref = refs[:3]
    pos = 3
    if has_gain:
        gain_ref = refs[pos]
        pos += 1
    if has_gate:
        gw_ref, gb_ref = refs[pos:pos + 2]
        pos += 2
    out_refs = refs[pos:pos + n_out]
    pos += n_out
    if has_gate:
        gate_out_ref = refs[pos]
        pos += 1
    xn_ref = refs[pos]
    j = pl.program_id(2)

    @pl.when(j == 0)
    def _():
        x = x_ref[0].astype(F32)
        ms = jnp.mean(x * x, axis=-1, keepdims=True)
        xn_ref[...] = (x * lax.rsqrt(ms + EPS) * g_ref[...]).astype(BF16)
        if has_gate:
            gate_out_ref[0] = _log_sigmoid(_dot(xn_ref[...], gw_ref[...]) + gb_ref[...])

    acc = _dot(xn_ref[...], w_ref[...])

    def store(val):
        for o in out_refs:
            o[0] = val.astype(o.dtype)

    if has_gain:
        def normed():
            cols = []
            for c in range(acc.shape[1] // head_dim):
                blk = acc[:, c * head_dim:(c + 1) * head_dim]
                ms = jnp.mean(blk * blk, axis=-1, keepdims=True)
                cols.append(blk * lax.rsqrt(ms + EPS))
            return jnp.concatenate(cols, axis=-1) * gain_ref[...]

        @pl.when(j < n_head_blocks)
        def _():
            store(normed())

        @pl.when(j >= n_head_blocks)
        def _():
            store(acc)
    else:
        store(acc)


def _norm_matmul(x, g, w, *, out_dtypes, head_gain=None, n_head_cols=0, head_dim=128,
                 gate_w=None, gate_b=None):
    lead_shape = x.shape[:2]
    x = x.reshape(1, -1, x.shape[-1])
    B, S, D = x.shape
    N = w.shape[1]
    tm = _tile(S, 1024)
    tn = _tile(N, 1024, 128)
    if head_gain is not None:
        tn = _tile(n_head_cols, tn, 128)
        assert N % tn == 0
    has_gain = head_gain is not None
    has_gate = gate_w is not None
    in_specs = [
        pl.BlockSpec((1, tm, D), lambda b, i, j: (b, i, 0)),
        pl.BlockSpec((1, D), lambda b, i, j: (0, 0)),
        pl.BlockSpec((D, tn), lambda b, i, j: (0, j)),
    ]
    args = [x, g.reshape(1, D).astype(F32), w]
    if has_gain:
        in_specs.append(pl.BlockSpec((1, tn), lambda b, i, j: (0, j)))
        args.append(head_gain.astype(F32))
    if has_gate:
        G = gate_w.shape[1]
        in_specs += [pl.BlockSpec((D, G), lambda b, i, j: (0, 0)),
                     pl.BlockSpec((1, G), lambda b, i, j: (0, 0))]
        args += [gate_w, gate_b.reshape(1, G).astype(F32)]
    out_shape = [jax.ShapeDtypeStruct((B, S, N), dt) for dt in out_dtypes]
    out_specs = [pl.BlockSpec((1, tm, tn), lambda b, i, j: (b, i, j)) for _ in out_dtypes]
    if has_gate:
        out_shape.append(jax.ShapeDtypeStruct((B, S, G), F32))
        out_specs.append(pl.BlockSpec((1, tm, G), lambda b, i, j: (b, i, 0)))
    kern = functools.partial(
        _norm_matmul_kernel, n_out=len(out_dtypes), n_head_blocks=n_head_cols // tn,
        head_dim=head_dim, has_gain=has_gain, has_gate=has_gate)
    outs = pl.pallas_call(
        kern, grid=(B, S // tm, N // tn), in_specs=in_specs, out_specs=out_specs,
        out_shape=out_shape, scratch_shapes=[pltpu.VMEM((tm, D), BF16)],
        compiler_params=_params("parallel", "parallel", "arbitrary"), name="norm_matmul",
    )(*args)
    return [o.reshape(lead_shape + o.shape[2:]) for o in outs]


def _matmul_residual_kernel(a_ref, w_ref, x_ref, o_ref):
    o_ref[0] = x_ref[0] + _dot(a_ref[0], w_ref[...])


def _matmul_residual(a, w, x):
    out_shape = x.shape
    a = a.reshape(1, -1, a.shape[-1])
    x = x.reshape(1, -1, x.shape[-1])
    B, S, K = a.shape
    N = w.shape[1]
    tm = _tile(S, 1024)
    tn = _tile(N, 1024, 128)
    return pl.pallas_call(
        _matmul_residual_kernel, grid=(B, S // tm, N // tn),
        in_specs=[pl.BlockSpec((1, tm, K), lambda b, i, j: (b, i, 0)),
                  pl.BlockSpec((K, tn), lambda b, i, j: (0, j)),
                  pl.BlockSpec((1, tm, tn), lambda b, i, j: (b, i, j))],
        out_specs=pl.BlockSpec((1, tm, tn), lambda b, i, j: (b, i, j)),
        out_shape=jax.ShapeDtypeStruct((B, S, N), F32),
        compiler_params=_params("parallel", "parallel", "arbitrary"), name="matmul_residual",
    )(a, w, x).reshape(out_shape)


def _conv_qkv_kernel(xm_ref, buf_ref, cw_ref, cb_ref, wq_ref, wk_ref, wv_ref, wg_ref, bg_ref,
                     q_ref, k_ref, v_ref, xc_ref, g_ref, seq_ref, *, rows, conv_w, n_heads,
                     k_scale):
    h = pl.program_id(1)
    S = xm_ref.shape[1]
    seq_ref[0:CONV_PAD, :] = buf_ref[0]
    seq_ref[CONV_PAD:CONV_PAD + S, :] = xm_ref[0].astype(F32)
    for r in range(S // rows):
        base = r * rows
        acc = jnp.broadcast_to(cb_ref[0], (rows, cb_ref.shape[-1]))
        for i in range(conv_w):
            start = base + CONV_PAD - (conv_w - 1) + i
            acc = acc + seq_ref[start:start + rows, :] * cw_ref[0, i:i + 1, :]
        xc = _silu(acc).astype(BF16)
        xm = xm_ref[0, base:base + rows, :]
        q = _dot(xc, wq_ref[0]).astype(BF16)
        k = (_dot(xc, wk_ref[0]) * k_scale).astype(BF16)
        v = _dot(xm, wv_ref[0]).astype(BF16)
        xc_ref[0, base:base + rows, :] = xc
        q_ref[0, base:base + rows, :] = q
        k_ref[0, base:base + rows, :] = k
        v_ref[0, base:base + rows, :] = v
        part = _dot(q, wg_ref[0, 0]) + _dot(k, wg_ref[1, 0]) + _dot(v, wg_ref[2, 0])

        @pl.when(h == 0)
        def _():
            g_ref[0, base:base + rows, :] = part + bg_ref[...]

        @pl.when(h > 0)
        def _():
            g_ref[0, base:base + rows, :] = g_ref[0, base:base + rows, :] + part

    @pl.when(h == n_heads - 1)
    def _():
        g = g_ref[0]
        lane = lax.broadcasted_iota(jnp.int32, g.shape, 1)
        g_ref[0] = jnp.where(lane >= n_heads, _log_sigmoid(g), g)


def _conv_qkv(xz, buf, conv_w, conv_b, wq, wk, wv, w_gate, b_gate):
    B, S, _ = xz.shape
    H, DH, _ = wq.shape
    AI = H * DH
    CW = conv_w.shape[0]
    G = 2 * H
    rows = _tile(S, 768)
    act = jax.ShapeDtypeStruct((B, S, AI), BF16)
    act_spec = pl.BlockSpec((1, S, DH), lambda b, h: (b, 0, h))
    w_spec = pl.BlockSpec((1, DH, DH), lambda b, h: (h, 0, 0))
    kern = functools.partial(_conv_qkv_kernel, rows=rows, conv_w=CW, n_heads=H,
                             k_scale=float(DH) ** -0.5)
    return pl.pallas_call(
        kern, grid=(B, H),
        in_specs=[act_spec,
                  pl.BlockSpec((1, CONV_PAD, DH), lambda b, h: (b, 0, h)),
                  pl.BlockSpec((1, CW, DH), lambda b, h: (h, 0, 0)),
                  pl.BlockSpec((1, 1, DH), lambda b, h: (h, 0, 0)),
                  w_spec, w_spec, w_spec,
                  pl.BlockSpec((3, 1, DH, G), lambda b, h: (0, h, 0, 0)),
                  pl.BlockSpec((1, G), lambda b, h: (0, 0))],
        out_specs=[act_spec, act_spec, act_spec, act_spec,
                   pl.BlockSpec((1, S, G), lambda b, h: (b, 0, 0))],
        out_shape=[act, act, act, act, jax.ShapeDtypeStruct((B, S, G), F32)],
        scratch_shapes=[pltpu.VMEM((S + CONV_PAD, DH), F32)],
        compiler_params=_params("parallel", "arbitrary"), name="conv_qkv",
    )(xz, buf,
      conv_w.reshape(CW, H, DH).transpose(1, 0, 2).astype(F32),
      conv_b.reshape(H, 1, DH).astype(F32),
      wq.astype(BF16), wk.astype(BF16), wv.astype(BF16),
      w_gate.reshape(3, H, DH, G).astype(BF16), b_gate.reshape(1, G).astype(F32))


def _mlstm_chunk(q, k, v, xc, z, gates, og, skip, C_ref, n_ref, m_ref):
    L = q.shape[0]
    row = lax.broadcasted_iota(jnp.int32, (L, L), 0)
    col = lax.broadcasted_iota(jnp.int32, (L, L), 1)
    tril = col <= row
    trilf = tril.astype(F32)
    eyef = (col == row).astype(F32)
    g8 = jnp.concatenate([gates, jnp.zeros((6, L), F32)], axis=0)
    ig_row = gates[0:1, :]
    b_row = jnp.dot(g8, (row <= col).astype(F32), precision=HIGHEST,
                    preferred_element_type=F32)[1:2, :]
    b_col = lax.dot_general(trilf, g8, (((1,), (1,)), ((), ())), precision=HIGHEST,
                            preferred_element_type=F32)[:, 1:2]
    ig_col = lax.dot_general(eyef, g8, (((1,), (1,)), ((), ())), precision=HIGHEST,
                             preferred_element_type=F32)[:, 0:1]
    m_prev = m_ref[...]
    log_d = jnp.where(tril, b_col - b_row + ig_row, NEG_INF)
    inter = b_col + m_prev
    m_t = jnp.maximum(jnp.max(log_d, axis=-1, keepdims=True), inter)
    d = jnp.exp(log_d - m_t)
    a = jnp.exp(inter - m_t)
    s = _dot_nt(q, k) * d
    C = C_ref[...]
    n = n_ref[...]
    num = _dot(s.astype(BF16), v) + a * _dot(q, C.astype(BF16))
    den = jnp.sum(s, axis=-1, keepdims=True) + a * jnp.sum(q.astype(F32) * n, axis=-1, keepdims=True)
    hid = num * (1.0 / jnp.maximum(jnp.abs(den), jnp.exp(-m_t)))
    m_new = m_t[L - 1:L, :]
    b_last = b_row[:, L - 1:L]
    decay = jnp.exp(b_last + m_prev - m_new)
    w_col = jnp.exp(b_last - b_col + ig_col - m_new)
    C_ref[...] = decay * C + _dot_tn(k, (w_col * v.astype(F32)).astype(BF16))
    n_ref[...] = decay * n + jnp.sum(w_col * k.astype(F32), axis=0, keepdims=True)
    m_ref[...] = m_new
    hc = hid - jnp.mean(hid, axis=-1, keepdims=True)
    hn = hc * lax.rsqrt(jnp.mean(hc * hc, axis=-1, keepdims=True) + EPS) * og
    return ((hn + skip * xc.astype(F32)) * _silu(z.astype(F32))).astype(BF16)


def _mlstm_kernel(*refs, lead, chunk, n_chunks, has_state):
    q_ref, k_ref, v_ref, xc_ref, z_ref = refs[:5]
    pos = 5
    if lead:
        gl_ref = refs[pos]
        pos += 1
    gm_ref, og_ref, skip_ref = refs[pos:pos + 3]
    pos += 3
    if has_state:
        c0_ref, n0_ref, m0_ref = refs[pos:pos + 3]
        pos += 3
    o_ref, C_ref, n_ref, m_ref = refs[pos:pos + 4]
    C_st, n_st, m_st = C_ref.at[0, 0], n_ref.at[0, 0], m_ref.at[0, 0]
    if has_state:
        C_st[...] = c0_ref[0, 0]
        n_st[...] = n0_ref[0, 0]
        m_st[...] = m0_ref[0, 0]
    else:
        C_st[...] = jnp.zeros(C_st.shape, F32)
        n_st[...] = jnp.zeros(n_st.shape, F32)
        m_st[...] = jnp.zeros(m_st.shape, F32)
    og = og_ref[0]
    skip = skip_ref[0]

    def run(rows, gates):
        o_ref[0, rows, :] = _mlstm_chunk(
            q_ref[0, rows, :], k_ref[0, rows, :], v_ref[0, rows, :], xc_ref[0, rows, :],
            z_ref[0, rows, :], gates, og, skip, C_st, n_st, m_st)

    if lead:
        run(pl.ds(0, lead), gl_ref[0, 0])

    def body(c, carry):
        run(pl.ds(pl.multiple_of(lead + c * chunk, BF16_ROWS), chunk), gm_ref[0, 0, c])
        return carry

    lax.fori_loop(0, n_chunks, body, 0)


def _mlstm(q, k, v, xc, xz, gates, out_g, skip, state, lead, chunk):
    B, S, AI = q.shape
    H, DH = out_g.shape
    n_chunks = (S - lead) // chunk
    assert lead + n_chunks * chunk == S
    g = gates.reshape(B, S, 2, H).transpose(0, 3, 2, 1)
    g_main = g[..., lead:].reshape(B, H, 2, n_chunks, chunk).transpose(0, 1, 3, 2, 4)
    act_spec = pl.BlockSpec((1, S, DH), lambda b, h: (b, 0, h))
    in_specs = [act_spec, act_spec, act_spec, act_spec,
                pl.BlockSpec((1, S, DH), lambda b, h: (b, 0, H + h))]
    args = [q, k, v, xc, xz]
    if lead:
        in_specs.append(pl.BlockSpec((1, 1, 2, lead), lambda b, h: (b, h, 0, 0)))
        args.append(g[..., :lead])
    in_specs += [pl.BlockSpec((1, 1, n_chunks, 2, chunk), lambda b, h: (b, h, 0, 0, 0)),
                 pl.BlockSpec((1, 1, DH), lambda b, h: (h, 0, 0)),
                 pl.BlockSpec((1, 1, DH), lambda b, h: (h, 0, 0))]
    args += [g_main, out_g.reshape(H, 1, DH).astype(F32), skip.reshape(H, 1, DH).astype(F32)]
    c_spec = pl.BlockSpec((1, 1, DH, DH), lambda b, h: (b, h, 0, 0))
    n_spec = pl.BlockSpec((1, 1, 1, DH), lambda b, h: (b, h, 0, 0))
    m_spec = pl.BlockSpec((1, 1, 1, 1), lambda b, h: (b, h, 0, 0))
    has_state = state is not None
    if has_state:
        C0, n0, m0 = state
        in_specs += [c_spec, n_spec, m_spec]
        args += [C0.astype(F32), n0.reshape(B, H, 1, DH).astype(F32), m0.reshape(B, H, 1, 1).astype(F32)]
    kern = functools.partial(_mlstm_kernel, lead=lead, chunk=chunk, n_chunks=n_chunks,
                             has_state=has_state)
    hz, C, n, m = pl.pallas_call(
        kern, grid=(B, H), in_specs=in_specs,
        out_specs=[act_spec, c_spec, n_spec, m_spec],
        out_shape=[jax.ShapeDtypeStruct((B, S, AI), BF16),
                   jax.ShapeDtypeStruct((B, H, DH, DH), F32),
                   jax.ShapeDtypeStruct((B, H, 1, DH), F32),
                   jax.ShapeDtypeStruct((B, H, 1, 1), F32)],
        compiler_params=_params("parallel", "parallel"), name="mlstm",
    )(*args)
    return hz, C, n.reshape(B, H, DH), m.reshape(B, H)


def _cumsum_kernel(x_ref, o_ref, *, rows):
    S, G = x_ref.shape[1:]
    r = lax.broadcasted_iota(jnp.int32, (rows, rows), 0)
    c = lax.broadcasted_iota(jnp.int32, (rows, rows), 1)
    trilf = (c <= r).astype(F32)
    carry = jnp.zeros((1, G), F32)
    for i in range(S // rows):
        blk = jnp.dot(trilf, x_ref[0, i * rows:(i + 1) * rows, :], precision=HIGHEST,
                      preferred_element_type=F32) + carry
        o_ref[0, i * rows:(i + 1) * rows, :] = blk
        carry = blk[rows - 1:rows, :]


def _cumsum(x):
    B, S, G = x.shape
    rows = _tile(S, 512, 8)
    spec = pl.BlockSpec((1, S, G), lambda b: (b, 0, 0))
    return pl.pallas_call(
        functools.partial(_cumsum_kernel, rows=rows), grid=(B,), in_specs=[spec], out_specs=spec,
        out_shape=jax.ShapeDtypeStruct((B, S, G), F32),
        compiler_params=_params("parallel"), name="cumsum",
    )(x)


def _attn_step(q, k, v, fq, fk, carry, masked):
    m, l, acc = carry
    s = _dot_nt(q, k) + fq - fk
    if masked:
        row = lax.broadcasted_iota(jnp.int32, s.shape, 0)
        col = lax.broadcasted_iota(jnp.int32, s.shape, 1)
        s = jnp.where(col <= row, s, NEG_INF)
    m_new = jnp.maximum(m, jnp.max(s, axis=-1, keepdims=True))
    alpha = jnp.exp(m - m_new)
    p = jnp.exp(s - m_new)
    l = alpha * l + jnp.sum(p, axis=-1, keepdims=True)
    acc = alpha * acc + _dot(p.astype(BF16), v)
    return m_new, l, acc


def _attn_init(n_q, dim):
    return (jnp.full((n_q, 1), NEG_INF, F32), jnp.zeros((n_q, 1), F32), jnp.zeros((n_q, dim), F32))


def _attn_finish(carry, z):
    _, l, acc = carry
    return (acc * (1.0 / l) * _silu(z.astype(F32))).astype(BF16)


def _fox_prompt_kernel(q_ref, z_ref, k_ref, v_ref, fq_ref, fkl_ref, fkm_ref, o_ref, *,
                       lead, blk, n_blk, heads, dim):
    def q_block(i, carry0):
        qrows = pl.ds(pl.multiple_of(i * blk, blk), blk)
        for hh in range(heads):
            cs = slice(hh * dim, (hh + 1) * dim)
            q = q_ref[0, qrows, cs]
            fq = fq_ref[0, 0, qrows, hh:hh + 1]
            carry = _attn_init(blk, dim)
            if lead:
                carry = _attn_step(q, k_ref[0, 0:lead, cs], v_ref[0, 0:lead, cs], fq,
                                   fkl_ref[0, 0, hh:hh + 1, :], carry, False)

            def kv_block(j, c):
                krows = pl.ds(pl.multiple_of(lead + j * blk, BF16_ROWS), blk)
                return _attn_step(q, k_ref[0, krows, cs], v_ref[0, krows, cs], fq,
                                  fkm_ref[0, 0, j, hh:hh + 1, :], c, False)

            carry = lax.fori_loop(0, i, kv_block, carry)
            krows = pl.ds(pl.multiple_of(lead + i * blk, BF16_ROWS), blk)
            carry = _attn_step(q, k_ref[0, krows, cs], v_ref[0, krows, cs], fq,
                               fkm_ref[0, 0, i, hh:hh + 1, :], carry, True)
            o_ref[0, qrows, cs] = _attn_finish(carry, z_ref[0, qrows, cs])
        return carry0

    lax.fori_loop(0, n_blk, q_block, 0)


def _fox_prompt(qz, kb, vb, f_cum, lead, heads_per_step=4, blk=256):
    B, SQ, W2 = qz.shape
    W = W2 // 2
    NH = f_cum.shape[-1]
    dim = W // NH
    hg = min(heads_per_step, NH)
    G = NH // hg
    blk = _tile(SQ, blk)
    n_blk = SQ // blk
    SK = lead + SQ
    fq = f_cum[:, lead:].reshape(B, SQ, G, hg).transpose(0, 2, 1, 3)
    fk = f_cum.transpose(0, 2, 1).reshape(B, G, hg, SK)
    fk_lead = fk[..., :max(lead, 1)]
    fk_main = fk[..., lead:].reshape(B, G, hg, n_blk, blk).transpose(0, 1, 3, 2, 4)
    wcols = hg * dim
    kern = functools.partial(_fox_prompt_kernel, lead=lead, blk=blk, n_blk=n_blk, heads=hg, dim=dim)
    return pl.pallas_call(
        kern, grid=(B, G),
        in_specs=[pl.BlockSpec((1, SQ, wcols), lambda b, g: (b, 0, g)),
                  pl.BlockSpec((1, SQ, wcols), lambda b, g: (b, 0, G + g)),
                  pl.BlockSpec((1, SK, wcols), lambda b, g: (b, 0, g)),
                  pl.BlockSpec((1, SK, wcols), lambda b, g: (b, 0, g)),
                  pl.BlockSpec((1, 1, SQ, hg), lambda b, g: (b, g, 0, 0)),
                  pl.BlockSpec((1, 1, hg, fk_lead.shape[-1]), lambda b, g: (b, g, 0, 0)),
                  pl.BlockSpec((1, 1, n_blk, hg, blk), lambda b, g: (b, g, 0, 0, 0))],
        out_specs=pl.BlockSpec((1, SQ, wcols), lambda b, g: (b, 0, g)),
        out_shape=jax.ShapeDtypeStruct((B, SQ, W), BF16),
        compiler_params=_params("parallel", "parallel"), name="fox_prompt",
    )(qz, qz, kb, vb, fq, fk_lead, fk_main)


def _fox_decode_kernel(q_ref, z_ref, ck_ref, cv_ref, k_ref, v_ref, fq_ref, fkc_ref, fkn_ref, o_ref,
                       *, blk, n_blk, heads, dim):
    for hh in range(heads):
        cs = slice(hh * dim, (hh + 1) * dim)
        q = q_ref[0, :, cs]
        fq = fq_ref[0, 0, :, hh:hh + 1]

        def kv_block(j, c):
            krows = pl.ds(pl.multiple_of(j * blk, blk), blk)
            return _attn_step(q, ck_ref[0, krows, cs].astype(BF16), cv_ref[0, krows, cs].astype(BF16),
                              fq, fkc_ref[0, 0, j, hh:hh + 1, :], c, False)

        carry = lax.fori_loop(0, n_blk, kv_block, _attn_init(q.shape[0], dim))
        carry = _attn_step(q, k_ref[0, :, cs], v_ref[0, :, cs], fq, fkn_ref[0, 0, hh:hh + 1, :],
                           carry, True)
        o_ref[0, :, cs] = _attn_finish(carry, z_ref[0, :, cs])


def _fox_decode(qz, cache_k, cache_v, kb, vb, f_cum, heads_per_step=2, blk=512):
    B, Q, W2 = qz.shape
    W = W2 // 2
    P = cache_k.shape[1]
    NH = f_cum.shape[-1]
    dim = W // NH
    hg = min(heads_per_step, NH)
    G = NH // hg
    blk = _tile(P, blk)
    n_blk = P // blk
    fq = f_cum[:, P:].reshape(B, Q, G, hg).transpose(0, 2, 1, 3)
    fk = f_cum.transpose(0, 2, 1).reshape(B, G, hg, P + Q)
    fk_cache = fk[..., :P].reshape(B, G, hg, n_blk, blk).transpose(0, 1, 3, 2, 4)
    fk_new = fk[..., P:]
    wcols = hg * dim
    kern = functools.partial(_fox_decode_kernel, blk=blk, n_blk=n_blk, heads=hg, dim=dim)
    return pl.pallas_call(
        kern, grid=(B, G),
        in_specs=[pl.BlockSpec((1, Q, wcols), lambda b, g: (b, 0, g)),
                  pl.BlockSpec((1, Q, wcols), lambda b, g: (b, 0, G + g)),
                  pl.BlockSpec((1, P, wcols), lambda b, g: (b, 0, g)),
                  pl.BlockSpec((1, P, wcols), lambda b, g: (b, 0, g)),
                  pl.BlockSpec((1, Q, wcols), lambda b, g: (b, 0, g)),
                  pl.BlockSpec((1, Q, wcols), lambda b, g: (b, 0, g)),
                  pl.BlockSpec((1, 1, Q, hg), lambda b, g: (b, g, 0, 0)),
                  pl.BlockSpec((1, 1, n_blk, hg, blk), lambda b, g: (b, g, 0, 0, 0)),
                  pl.BlockSpec((1, 1, hg, Q), lambda b, g: (b, g, 0, 0))],
        out_specs=pl.BlockSpec((1, Q, wcols), lambda b, g: (b, 0, g)),
        out_shape=jax.ShapeDtypeStruct((B, Q, W), BF16),
        compiler_params=_params("parallel", "parallel"), name="fox_decode",
    )(qz, qz, cache_k, cache_v, kb, vb, fq, fk_cache, fk_new)


def _mlstm_layer(x, conv_hist, state, w, lead, chunk):
    (norm_g, w_in, conv_w, conv_b, wq, wk, wv, w_gate, b_gate, out_g, skip, w_out) = w
    B, S, _ = x.shape
    H, DH, _ = wq.shape
    AI = H * DH
    CW = conv_w.shape[0]
    assert S >= CW - 1 and CW - 1 <= CONV_PAD
    (xz,) = _norm_matmul(x, norm_g, w_in.astype(BF16), out_dtypes=[BF16])
    buf = jnp.zeros((B, CONV_PAD, AI), F32)
    if conv_hist is not None:
        buf = buf.at[:, CONV_PAD - (CW - 1):].set(conv_hist.astype(F32))
    q, k, v, xc, gates = _conv_qkv(xz, buf, conv_w, conv_b, wq, wk, wv, w_gate, b_gate)
    hz, C, n, m = _mlstm(q, k, v, xc, xz, gates, out_g, skip.reshape(H, DH), state, lead, chunk)
    x_new = _matmul_residual(hz, w_out.astype(BF16), x)
    new_hist = xz[:, S - (CW - 1):, :AI].astype(F32)
    return x_new, new_hist, C, n, m


def _shared_kv(x, kv_norm_g, w_kvf, b_f, k_norm_g):
    NH, dim = k_norm_g.shape
    W = NH * dim
    wb = w_kvf.astype(BF16)
    k32, kb, logf = _norm_matmul(
        x, kv_norm_g, wb[:, :W], out_dtypes=[F32, BF16], head_gain=k_norm_g.reshape(1, W),
        n_head_cols=W, head_dim=dim, gate_w=wb[:, 2 * W:], gate_b=b_f)
    v32, vb = _norm_matmul(x, kv_norm_g, wb[:, W:2 * W], out_dtypes=[F32, BF16])
    return k32, kb, v32, vb, logf


def _fox_in(x, norm_g, w_in, q_norm_g):
    NH, dim = q_norm_g.shape
    W = NH * dim
    gain = jnp.concatenate([q_norm_g.reshape(1, W).astype(F32) * (float(dim) ** -0.5),
                            jnp.ones((1, W), F32)], axis=-1)
    (qz,) = _norm_matmul(x, norm_g, w_in.astype(BF16), out_dtypes=[BF16], head_gain=gain,
                         n_head_cols=W, head_dim=dim)
    return qz


def kernel(x_prompt, x_sample, cache_k, cache_v, cache_logf, state_C, state_n, state_m, state_conv,
           meta_tokens, a_norm_g, a_w_in, a_conv_w, a_conv_b, a_wq, a_wk, a_wv, a_w_gate, a_b_gate,
           a_out_g, a_skip, a_w_out, kv_norm_g, w_kvf, b_f, k_norm_g, b_norm_g, b_w_in, q_norm_g,
           b_w_out):
    B, SEQ, D = x_prompt.shape
    DB, DEC, _ = x_sample.shape
    NM = meta_tokens.shape[0]
    N_A = a_norm_g.shape[0]
    N_B = b_norm_g.shape[0]
    NH, dim = k_norm_g.shape
    W = NH * dim
    P = cache_k.shape[1]
    chunk = _tile(SEQ, 256)

    xp = jnp.concatenate(
        [jnp.broadcast_to(meta_tokens.astype(x_prompt.dtype)[None], (B, NM, D)), x_prompt], axis=1)
    xs = x_sample
    p_state, s_state = [], []
    for i in range(N_A):
        w = (a_norm_g[i], a_w_in[i], a_conv_w[i], a_conv_b[i], a_wq[i], a_wk[i], a_wv[i],
             a_w_gate[i], a_b_gate[i], a_out_g[i], a_skip[i], a_w_out[i])
        xp, *st = _mlstm_layer(xp, None, None, w, NM, chunk)
        p_state.append(st)
        xs, *st = _mlstm_layer(xs, state_conv[i], (state_C[i], state_n[i], state_m[i]), w, 0, DEC)
        s_state.append(st)

    pk32, pkb, pv32, pvb, p_logf = _shared_kv(xp, kv_norm_g, w_kvf, b_f, k_norm_g)
    sk32, skb, sv32, svb, s_logf = _shared_kv(xs, kv_norm_g, w_kvf, b_f, k_norm_g)
    fp = _cumsum(p_logf)
    fs = _cumsum(jnp.concatenate([cache_logf.astype(F32), s_logf], axis=1))
    ck = cache_k.reshape(DB, P, W)
    cv = cache_v.reshape(DB, P, W)
    xp = xp[:, NM:]
    for j in range(N_B):
        qz = _fox_in(xp, b_norm_g[j], b_w_in[j], q_norm_g[j])
        xp = _matmul_residual(_fox_prompt(qz, pkb, pvb, fp, NM), b_w_out[j].astype(BF16), xp)
        qz = _fox_in(xs, b_norm_g[j], b_w_in[j], q_norm_g[j])
        xs = _matmul_residual(_fox_decode(qz, ck, cv, skb, svb, fs), b_w_out[j].astype(BF16), xs)

    def stack(states, idx):
        return jnp.stack([st[idx] for st in states])

    return (xp, xs,
            pk32.reshape(B, NM + SEQ, NH, dim), pv32.reshape(B, NM + SEQ, NH, dim), p_logf,
            stack(p_state, 1), stack(p_state, 2), stack(p_state, 3), stack(p_state, 0),
            sk32.reshape(DB, DEC, NH, dim), sv32.reshape(DB, DEC, NH, dim), s_logf,
            stack(s_state, 1), stack(s_state, 2), stack(s_state, 3), stack(s_state, 0))
```

```python
import functools

import jax
import jax.numpy as jnp
from jax import lax
from jax.experimental import pallas as pl
from jax.experimental.pallas import tpu as pltpu

F32 = jnp.float32
BF16 = jnp.bfloat16
EPS = 1e-6
CONV_PAD = 8
BF16_ROWS = 16
LANES = 128
VMEM_LIMIT_BYTES = 56 * 1024 * 1024
HIGHEST = lax.Precision.HIGHEST
NEG_INF = float("-inf")
LOG2E = 1.4426950408889634


def _params(*sem):
    return pltpu.CompilerParams(dimension_semantics=sem, vmem_limit_bytes=VMEM_LIMIT_BYTES)


def _tile(n, cap, mult=BF16_ROWS):
    best = None
    for t in range(mult, min(n, cap) + 1, mult):
        if n % t == 0:
            best = t
    return best if best is not None else n


def _dot(a, b):
    return jnp.dot(a, b, preferred_element_type=F32)


def _dot_nt(a, b):
    return lax.dot_general(a, b, (((1,), (1,)), ((), ())), preferred_element_type=F32)


def _dot_tn(a, b):
    return lax.dot_general(a, b, (((0,), (0,)), ((), ())), preferred_element_type=F32)


def _silu(x):
    return x / (1.0 + jnp.exp(-x))


def _log_sigmoid(x):
    return jnp.minimum(x, 0.0) - jnp.log1p(jnp.exp(-jnp.abs(x)))


def _norm_matmul_kernel(*refs, n_out, n_head_blocks, head_dim, has_gain, has_gate):
    x_ref, g_ref, w_ref = refs[:3]
    pos = 3
    if has_gain:
        gain_ref = refs[pos]
        pos += 1
    if has_gate:
        gw_ref, gb_ref = refs[pos:pos + 2]
        pos += 2
    out_refs = refs[pos:pos + n_out]
    pos += n_out
    if has_gate:
        gate_out_ref = refs[pos]
        pos += 1
    xn_ref = refs[pos]
    j = pl.program_id(2)

    @pl.when(j == 0)
    def _():
        x = x_ref[0].astype(F32)
        ms = jnp.mean(x * x, axis=-1, keepdims=True)
        xn_ref[...] = (x * lax.rsqrt(ms + EPS) * g_ref[...]).astype(BF16)
        if has_gate:
            gate_out_ref[0] = _log_sigmoid(_dot(xn_ref[...], gw_ref[...]) + gb_ref[...])

    acc = _dot(xn_ref[...], w_ref[...])

    def store(val):
        for o in out_refs:
            if len(o.shape) == 4:
                o[0] = val.reshape(o.shape[1:]).astype(o.dtype)
            else:
                o[0] = val.astype(o.dtype)

    if has_gain:
        def normed():
            cols = []
            for c in range(acc.shape[1] // head_dim):
                blk = acc[:, c * head_dim:(c + 1) * head_dim]
                ms = jnp.mean(blk * blk, axis=-1, keepdims=True)
                cols.append(blk * lax.rsqrt(ms + EPS))
            return jnp.concatenate(cols, axis=-1) * gain_ref[...]

        @pl.when(j < n_head_blocks)
        def _():
            store(normed())

        @pl.when(j >= n_head_blocks)
        def _():
            store(acc)
    else:
        store(acc)


def _norm_matmul(x, g, w, *, out_dtypes, head_gain=None, n_head_cols=0, head_dim=128,
                 gate_w=None, gate_b=None, split_heads_first=False):
    lead_shape = x.shape[:2]
    x = x.reshape(1, -1, x.shape[-1])
    B, S, D = x.shape
    N = w.shape[1]
    tm = _tile(S, 1024)
    tn = _tile(N, 1024, 128)
    if head_gain is not None:
        tn = _tile(n_head_cols, tn, 128)
        assert N % tn == 0
    has_gain = head_gain is not None
    has_gate = gate_w is not None
    in_specs = [
        pl.BlockSpec((1, tm, D), lambda b, i, j: (b, i, 0)),
        pl.BlockSpec((1, D), lambda b, i, j: (0, 0)),
        pl.BlockSpec((D, tn), lambda b, i, j: (0, j)),
    ]
    args = [x, g.reshape(1, D).astype(F32), w]
    if has_gain:
        in_specs.append(pl.BlockSpec((1, tn), lambda b, i, j: (0, j)))
        args.append(head_gain.astype(F32))
    if has_gate:
        G = gate_w.shape[1]
        in_specs += [pl.BlockSpec((D, G), lambda b, i, j: (0, 0)),
                     pl.BlockSpec((1, G), lambda b, i, j: (0, 0))]
        args += [gate_w, gate_b.reshape(1, G).astype(F32)]
    out_shape = [jax.ShapeDtypeStruct((B, S, N), dt) for dt in out_dtypes]
    out_specs = [pl.BlockSpec((1, tm, tn), lambda b, i, j: (b, i, j)) for _ in out_dtypes]
    if split_heads_first:
        out_shape[0] = jax.ShapeDtypeStruct((B, S, N // head_dim, head_dim), out_dtypes[0])
        out_specs[0] = pl.BlockSpec((1, tm, tn // head_dim, head_dim), lambda b, i, j: (b, i, j, 0))
    if has_gate:
        out_shape.append(jax.ShapeDtypeStruct((B, S, G), F32))
        out_specs.append(pl.BlockSpec((1, tm, G), lambda b, i, j: (b, i, 0)))
    kern = functools.partial(
        _norm_matmul_kernel, n_out=len(out_dtypes), n_head_blocks=n_head_cols // tn,
        head_dim=head_dim, has_gain=has_gain, has_gate=has_gate)
    outs = pl.pallas_call(
        kern, grid=(B, S // tm, N // tn), in_specs=in_specs, out_specs=out_specs,
        out_shape=out_shape, scratch_shapes=[pltpu.VMEM((tm, D), BF16)],
        compiler_params=_params("parallel", "parallel", "arbitrary"), name="norm_matmul",
    )(*args)
    return [o.reshape(lead_shape + o.shape[2:]) for o in outs]


def _matmul_residual_kernel(a_ref, w_ref, x_ref, o_ref):
    o_ref[0] = x_ref[0] + _dot(a_ref[0], w_ref[...])


def _matmul_residual(a, w, x):
    out_shape = x.shape
    a = a.reshape(1, -1, a.shape[-1])
    x = x.reshape(1, -1, x.shape[-1])
    B, S, K = a.shape
    N = w.shape[1]
    tm = _tile(S, 1024)
    tn = _tile(N, 1024, 128)
    return pl.pallas_call(
        _matmul_residual_kernel, grid=(B, S // tm, N // tn),
        in_specs=[pl.BlockSpec((1, tm, K), lambda b, i, j: (b, i, 0)),
                  pl.BlockSpec((K, tn), lambda b, i, j: (0, j)),
                  pl.BlockSpec((1, tm, tn), lambda b, i, j: (b, i, j))],
        out_specs=pl.BlockSpec((1, tm, tn), lambda b, i, j: (b, i, j)),
        out_shape=jax.ShapeDtypeStruct((B, S, N), F32),
        compiler_params=_params("parallel", "parallel", "arbitrary"), name="matmul_residual",
    )(a, w, x).reshape(out_shape)


def _conv_qkv_kernel(xm_ref, buf_ref, cw_ref, cb_ref, wq_ref, wk_ref, wv_ref, wg_ref, bg_ref,
                     q_ref, k_ref, v_ref, xc_ref, g_ref, seq_ref, *, rows, cum_rows, conv_w,
                     n_heads):
    h = pl.program_id(1)
    S = xm_ref.shape[1]
    seq_ref[0:CONV_PAD, :] = buf_ref[0]
    seq_ref[CONV_PAD:CONV_PAD + S, :] = xm_ref[0].astype(F32)
    for r in range(S // rows):
        base = r * rows
        acc = jnp.broadcast_to(cb_ref[0], (rows, cb_ref.shape[-1]))
        for i in range(conv_w):
            start = base + CONV_PAD - (conv_w - 1) + i
            acc = acc + seq_ref[start:start + rows, :] * cw_ref[0, i:i + 1, :]
        xc = _silu(acc).astype(BF16)
        xm = xm_ref[0, base:base + rows, :]
        q = _dot(xc, wq_ref[0]).astype(BF16)
        k = _dot(xc, wk_ref[0]).astype(BF16)
        v = _dot(xm, wv_ref[0]).astype(BF16)
        xc_ref[0, base:base + rows, :] = xc
        q_ref[0, base:base + rows, :] = q
        k_ref[0, base:base + rows, :] = k
        v_ref[0, base:base + rows, :] = v
        part = _dot(q, wg_ref[0, 0]) + _dot(k, wg_ref[1, 0]) + _dot(v, wg_ref[2, 0])

        @pl.when(h == 0)
        def _():
            g_ref[0, base:base + rows, :] = part + bg_ref[...]

        @pl.when(h > 0)
        def _():
            g_ref[0, base:base + rows, :] = g_ref[0, base:base + rows, :] + part

    @pl.when(h == n_heads - 1)
    def _():
        r = lax.broadcasted_iota(jnp.int32, (cum_rows, cum_rows), 0)
        c = lax.broadcasted_iota(jnp.int32, (cum_rows, cum_rows), 1)
        trilf = (c <= r).astype(F32)
        lane = lax.broadcasted_iota(jnp.int32, (cum_rows, g_ref.shape[-1]), 1)
        carry = jnp.zeros((1, g_ref.shape[-1]), F32)
        for i in range(S // cum_rows):
            g = g_ref[0, i * cum_rows:(i + 1) * cum_rows, :]
            lf = jnp.where(lane >= n_heads, _log_sigmoid(g), 0.0)
            cum = jnp.dot(trilf, lf, precision=HIGHEST, preferred_element_type=F32) + carry
            g_ref[0, i * cum_rows:(i + 1) * cum_rows, :] = jnp.where(lane >= n_heads, cum, g)
            carry = cum[cum_rows - 1:cum_rows, :]


def _conv_qkv(xz, buf, conv_w, conv_b, wq, wk, wv, w_gate, b_gate):
    B, S, _ = xz.shape
    H, DH, _ = wq.shape
    AI = H * DH
    CW = conv_w.shape[0]
    G = 2 * H
    rows = _tile(S, 768)
    act = jax.ShapeDtypeStruct((B, S, AI), BF16)
    act_spec = pl.BlockSpec((1, S, DH), lambda b, h: (b, 0, h))
    w_spec = pl.BlockSpec((1, DH, DH), lambda b, h: (h, 0, 0))
    kern = functools.partial(_conv_qkv_kernel, rows=rows, cum_rows=_tile(S, 512, 8), conv_w=CW,
                             n_heads=H)
    return pl.pallas_call(
        kern, grid=(B, H),
        in_specs=[act_spec,
                  pl.BlockSpec((1, CONV_PAD, DH), lambda b, h: (b, 0, h)),
                  pl.BlockSpec((1, CW, DH), lambda b, h: (h, 0, 0)),
                  pl.BlockSpec((1, 1, DH), lambda b, h: (h, 0, 0)),
                  w_spec, w_spec, w_spec,
                  pl.BlockSpec((3, 1, DH, G), lambda b, h: (0, h, 0, 0)),
                  pl.BlockSpec((1, G), lambda b, h: (0, 0))],
        out_specs=[act_spec, act_spec, act_spec, act_spec,
                   pl.BlockSpec((1, S, G), lambda b, h: (b, 0, 0))],
        out_shape=[act, act, act, act, jax.ShapeDtypeStruct((B, S, G), F32)],
        scratch_shapes=[pltpu.VMEM((S + CONV_PAD, DH), F32)],
        compiler_params=_params("parallel", "arbitrary"), name="conv_qkv",
    )(xz, buf,
      conv_w.reshape(CW, H, DH).transpose(1, 0, 2).astype(F32),
      conv_b.reshape(H, 1, DH).astype(F32),
      wq.astype(BF16), (wk * float(DH) ** -0.5).astype(BF16), wv.astype(BF16),
      w_gate.reshape(3, H, DH, G).astype(BF16), b_gate.reshape(1, G).astype(F32))


def _mlstm_chunk(q, k, v, xc, z, g_rows, g_cols, head, n_heads, og, skip,
                 C_ref, n_ref, m_ref, b_ref):
    L = q.shape[0]
    row = lax.broadcasted_iota(jnp.int32, (L, L), 0)
    col = lax.broadcasted_iota(jnp.int32, (L, L), 1)
    tril = col <= row
    lane = lax.broadcasted_iota(jnp.int32, g_cols.shape, 1)
    ig_col = jnp.sum(jnp.where(lane == head, g_cols, 0.0), axis=-1, keepdims=True)
    f_col = jnp.sum(jnp.where(lane == head + n_heads, g_cols, 0.0), axis=-1, keepdims=True)
    ig_row = g_rows[0:1, :]
    f_row = g_rows[1:2, :]
    m_prev = m_ref[...]
    f_prev = b_ref[...]
    log_d = jnp.where(tril, f_col - f_row + ig_row, NEG_INF)
    inter = f_col - f_prev + m_prev
    m_t = jnp.maximum(jnp.max(log_d, axis=-1, keepdims=True), inter)
    d = jnp.exp(log_d - m_t)
    a = jnp.exp(inter - m_t)
    s = _dot_nt(q, k) * d
    C = C_ref[...]
    n = n_ref[...]
    num = _dot(s.astype(BF16), v) + a * _dot(q, C.astype(BF16))
    den = jnp.sum(s, axis=-1, keepdims=True) + a * jnp.sum(q.astype(F32) * n, axis=-1, keepdims=True)
    hid = num * (1.0 / jnp.maximum(jnp.abs(den), jnp.exp(-m_t)))
    m_new = m_t[L - 1:L, :]
    f_last = f_row[:, L - 1:L]
    decay = jnp.exp(f_last - f_prev + m_prev - m_new)
    w_col = jnp.exp(f_last - f_col + ig_col - m_new)
    C_ref[...] = decay * C + _dot_tn(k, (w_col * v.astype(F32)).astype(BF16))
    n_ref[...] = decay * n + jnp.sum(w_col * k.astype(F32), axis=0, keepdims=True)
    m_ref[...] = m_new
    b_ref[...] = f_last
    hc = hid - jnp.mean(hid, axis=-1, keepdims=True)
    hn = hc * lax.rsqrt(jnp.mean(hc * hc, axis=-1, keepdims=True) + EPS) * og
    return ((hn + skip * xc.astype(F32)) * _silu(z.astype(F32))).astype(BF16)


def _mlstm_kernel(*refs, lead, chunk, n_chunks, n_heads, has_state):
    q_ref, k_ref, v_ref, xc_ref, z_ref = refs[:5]
    pos = 5
    if lead:
        gl_ref = refs[pos]
        pos += 1
    gm_ref, gc_ref, og_ref, skip_ref = refs[pos:pos + 4]
    pos += 4
    if has_state:
        c0_ref, n0_ref, m0_ref = refs[pos:pos + 3]
        pos += 3
    pos += 1
    o_ref, C_ref, n_ref, m_ref, b_ref = refs[pos:pos + 5]
    C_st, n_st, m_st = C_ref.at[0, 0, 0], n_ref.at[0, 0], m_ref.at[0, 0]
    if has_state:
        C_st[...] = c0_ref[0, 0]
        n_st[...] = n0_ref[0, 0]
        m_st[...] = m0_ref[0, 0]
    else:
        C_st[...] = jnp.zeros(C_st.shape, F32)
        n_st[...] = jnp.zeros(n_st.shape, F32)
        m_st[...] = jnp.zeros(m_st.shape, F32)
    b_ref[...] = jnp.zeros(b_ref.shape, F32)
    head = pl.program_id(1)
    og = og_ref[0]
    skip = skip_ref[0]

    def run(rows, g_rows):
        o_ref[0, rows, :] = _mlstm_chunk(
            q_ref[0, rows, :], k_ref[0, rows, :], v_ref[0, rows, :], xc_ref[0, rows, :],
            z_ref[0, rows, :], g_rows, gc_ref[0, rows, :], head, n_heads, og, skip,
            C_st, n_st, m_st, b_ref)

    if lead:
        run(pl.ds(0, lead), gl_ref[0, 0])

    def body(c, carry):
        run(pl.ds(pl.multiple_of(lead + c * chunk, BF16_ROWS), chunk), gm_ref[0, 0, c])
        return carry

    lax.fori_loop(0, n_chunks, body, 0)


def _mlstm(q, k, v, xc, xz, gates, out_g, skip, state, lead, chunk, c_stack, layer, n_layers):
    B, S, AI = q.shape
    H, DH = out_g.shape
    n_chunks = (S - lead) // chunk
    assert lead + n_chunks * chunk == S
    g = gates.reshape(B, S, 2, H).transpose(0, 3, 2, 1)
    g_main = g[..., lead:].reshape(B, H, 2, n_chunks, chunk).transpose(0, 1, 3, 2, 4)
    act_spec = pl.BlockSpec((1, S, DH), lambda b, h: (b, 0, h))
    in_specs = [act_spec, act_spec, act_spec, act_spec,
                pl.BlockSpec((1, S, DH), lambda b, h: (b, 0, H + h))]
    args = [q, k, v, xc, xz]
    if lead:
        in_specs.append(pl.BlockSpec((1, 1, 2, lead), lambda b, h: (b, h, 0, 0)))
        args.append(g[..., :lead])
    in_specs += [pl.BlockSpec((1, 1, n_chunks, 2, chunk), lambda b, h: (b, h, 0, 0, 0)),
                 pl.BlockSpec((1, S, 2 * H), lambda b, h: (b, 0, 0)),
                 pl.BlockSpec((1, 1, DH), lambda b, h: (h, 0, 0)),
                 pl.BlockSpec((1, 1, DH), lambda b, h: (h, 0, 0))]
    args += [g_main, gates, out_g.reshape(H, 1, DH).astype(F32), skip.reshape(H, 1, DH).astype(F32)]
    c_spec = pl.BlockSpec((1, 1, DH, DH), lambda b, h: (b, h, 0, 0))
    n_spec = pl.BlockSpec((1, 1, 1, DH), lambda b, h: (b, h, 0, 0))
    m_spec = pl.BlockSpec((1, 1, 1, 1), lambda b, h: (b, h, 0, 0))
    has_state = state is not None
    if has_state:
        C0, n0, m0 = state
        in_specs += [c_spec, n_spec, m_spec]
        args += [C0.astype(F32), n0.reshape(B, H, 1, DH).astype(F32), m0.reshape(B, H, 1, 1).astype(F32)]
    if c_stack is None:
        c_stack = jnp.zeros((n_layers, B, H, DH, DH), F32)
    aliases = {len(args): 1}
    in_specs.append(pl.BlockSpec(memory_space=pl.ANY))
    args.append(c_stack)
    kern = functools.partial(_mlstm_kernel, lead=lead, chunk=chunk, n_chunks=n_chunks, n_heads=H,
                             has_state=has_state)
    hz, C, n, m = pl.pallas_call(
        kern, grid=(B, H), in_specs=in_specs,
        out_specs=[act_spec,
                   pl.BlockSpec((1, 1, 1, DH, DH), lambda b, h: (layer, b, h, 0, 0)),
                   n_spec, m_spec],
        out_shape=[jax.ShapeDtypeStruct((B, S, AI), BF16),
                   jax.ShapeDtypeStruct((n_layers, B, H, DH, DH), F32),
                   jax.ShapeDtypeStruct((B, H, 1, DH), F32),
                   jax.ShapeDtypeStruct((B, H, 1, 1), F32)],
        scratch_shapes=[pltpu.VMEM((1, 1), F32)],
        input_output_aliases=aliases,
        compiler_params=_params("parallel", "parallel"), name="mlstm",
    )(*args)
    return hz, C, n.reshape(B, H, DH), m.reshape(B, H)


def _cumsum_kernel(x_ref, o_ref, *, rows, scale):
    S, G = x_ref.shape[1:]
    r = lax.broadcasted_iota(jnp.int32, (rows, rows), 0)
    c = lax.broadcasted_iota(jnp.int32, (rows, rows), 1)
    trilf = (c <= r).astype(F32)
    carry = jnp.zeros((1, G), F32)
    for i in range(S // rows):
        blk = jnp.dot(trilf, x_ref[0, i * rows:(i + 1) * rows, :], precision=HIGHEST,
                      preferred_element_type=F32) + carry
        o_ref[0, i * rows:(i + 1) * rows, :] = blk * scale
        carry = blk[rows - 1:rows, :]


def _cumsum(x, scale):
    B, S, G = x.shape
    rows = _tile(S, 512, 8)
    spec = pl.BlockSpec((1, S, G), lambda b: (b, 0, 0))
    return pl.pallas_call(
        functools.partial(_cumsum_kernel, rows=rows, scale=scale), grid=(B,), in_specs=[spec],
        out_specs=spec, out_shape=jax.ShapeDtypeStruct((B, S, G), F32),
        compiler_params=_params("parallel"), name="cumsum",
    )(x)


def _lane_fold(op, acc, x):
    for c in range(x.shape[1] // LANES):
        acc = op(acc, x[:, c * LANES:(c + 1) * LANES])
    return acc


def _fox_prompt_kernel(q_ref, z_ref, k_ref, v_ref, fq_ref, fkl_ref, fkm_ref, o_ref,
                       s_ref, sl_ref, fq_rep, m_ref, l_ref, acc_ref, *, lead, blk, n_blk, heads,
                       dim):
    n_lane = blk // LANES
    cols = [slice(hh * dim, (hh + 1) * dim) for hh in range(heads)]

    def key_rows(j):
        return pl.ds(pl.multiple_of(lead + j * blk, BF16_ROWS), blk)

    def tiled(x):
        return jnp.concatenate([x] * n_lane, axis=1)

    def q_block(i, carry):
        qrows = pl.ds(pl.multiple_of(i * blk, blk), blk)

        def logits(hh, j):
            return (_dot_nt(q_ref[0, qrows, cols[hh]], k_ref[0, key_rows(j), cols[hh]])
                    + tiled(fq_rep[hh]) - fkm_ref[0, 0, j, hh:hh + 1, :])

        for hh in range(heads):
            fq_rep[hh] = jnp.broadcast_to(fq_ref[0, 0, qrows, hh:hh + 1], (blk, LANES))
        if lead:
            for hh in range(heads):
                s = (_dot_nt(q_ref[0, qrows, cols[hh]], k_ref[0, 0:LANES, cols[hh]])
                     + fq_rep[hh] - fkl_ref[0, 0, hh:hh + 1, :])
                sl_ref[hh] = s
                m_ref[hh] = s
        else:
            m_ref[...] = jnp.full(m_ref.shape, NEG_INF, F32)

        def pass1(j, c):
            for hh in range(heads):
                s = logits(hh, j)
                s_ref[hh, j] = s
                m_ref[hh] = _lane_fold(jnp.maximum, m_ref[hh], s)
            return c

        lax.fori_loop(0, i, pass1, 0)
        row = lax.broadcasted_iota(jnp.int32, (blk, blk), 0)
        col = lax.broadcasted_iota(jnp.int32, (blk, blk), 1)
        for hh in range(heads):
            s = jnp.where(col <= row, logits(hh, i), NEG_INF)
            s_ref[hh, i] = s
            m = jnp.max(_lane_fold(jnp.maximum, m_ref[hh], s), axis=-1, keepdims=True)
            m_ref[hh] = jnp.broadcast_to(m, (blk, LANES))
        if lead:
            for hh in range(heads):
                p = jnp.exp2(sl_ref[hh] - m_ref[hh])
                l_ref[hh] = p
                acc_ref[hh] = _dot(p.astype(BF16), v_ref[0, 0:LANES, cols[hh]])
        else:
            l_ref[...] = jnp.zeros(l_ref.shape, F32)
            acc_ref[...] = jnp.zeros(acc_ref.shape, F32)

        def pass2(j, c):
            for hh in range(heads):
                p = jnp.exp2(s_ref[hh, j] - tiled(m_ref[hh]))
                l_ref[hh] = _lane_fold(jnp.add, l_ref[hh], p)
                acc_ref[hh] += _dot(p.astype(BF16), v_ref[0, key_rows(j), cols[hh]])
            return c

        lax.fori_loop(0, i + 1, pass2, 0)
        for hh in range(heads):
            l = jnp.sum(l_ref[hh], axis=-1, keepdims=True)
            o_ref[0, qrows, cols[hh]] = (acc_ref[hh] * (1.0 / l)
                                         * _silu(z_ref[0, qrows, cols[hh]].astype(F32))).astype(BF16)
        return carry

    lax.fori_loop(0, n_blk, q_block, 0)


def _fox_prompt(qz, kb, vb, f_cum, lead, heads_per_step=4, blk=512):
    B, SQ, W2 = qz.shape
    W = W2 // 2
    NH = f_cum.shape[-1]
    dim = W // NH
    hg = min(heads_per_step, NH)
    G = NH // hg
    blk = _tile(SQ, blk, LANES)
    n_blk = SQ // blk
    SK = lead + SQ
    assert lead <= LANES <= SK
    fq = f_cum[:, lead:].reshape(B, SQ, G, hg).transpose(0, 2, 1, 3)
    fk = f_cum.transpose(0, 2, 1).reshape(B, G, hg, SK)
    fk_lead = jnp.pad(fk[..., :lead], ((0, 0), (0, 0), (0, 0), (0, LANES - lead)),
                      constant_values=float("inf"))
    fk_main = fk[..., lead:].reshape(B, G, hg, n_blk, blk).transpose(0, 1, 3, 2, 4)
    wcols = hg * dim
    kern = functools.partial(_fox_prompt_kernel, lead=lead, blk=blk, n_blk=n_blk, heads=hg, dim=dim)
    return pl.pallas_call(
        kern, grid=(B, G),
        in_specs=[pl.BlockSpec((1, SQ, wcols), lambda b, g: (b, 0, g)),
                  pl.BlockSpec((1, SQ, wcols), lambda b, g: (b, 0, G + g)),
                  pl.BlockSpec((1, SK, wcols), lambda b, g: (b, 0, g)),
                  pl.BlockSpec((1, SK, wcols), lambda b, g: (b, 0, g)),
                  pl.BlockSpec((1, 1, SQ, hg), lambda b, g: (b, g, 0, 0)),
                  pl.BlockSpec((1, 1, hg, LANES), lambda b, g: (b, g, 0, 0)),
                  pl.BlockSpec((1, 1, n_blk, hg, blk), lambda b, g: (b, g, 0, 0, 0))],
        out_specs=pl.BlockSpec((1, SQ, wcols), lambda b, g: (b, 0, g)),
        out_shape=jax.ShapeDtypeStruct((B, SQ, W), BF16),
        scratch_shapes=[pltpu.VMEM((hg, n_blk, blk, blk), F32)]
        + [pltpu.VMEM((hg, blk, LANES), F32)] * 4 + [pltpu.VMEM((hg, blk, dim), F32)],
        compiler_params=_params("parallel", "parallel"), name="fox_prompt",
    )(qz, qz, kb, vb, fq, fk_lead, fk_main)


def _fox_decode_kernel(q_ref, z_ref, ck_ref, cv_ref, k_ref, v_ref, fq_ref, fkc_ref, fkn_ref, o_ref,
                       *, heads, dim):
    Q = q_ref.shape[1]
    row = lax.broadcasted_iota(jnp.int32, (Q, Q), 0)
    col = lax.broadcasted_iota(jnp.int32, (Q, Q), 1)
    outs = []
    for hh in range(heads):
        cs = slice(hh * dim, (hh + 1) * dim)
        q = q_ref[0, :, cs]
        fq = fq_ref[0, 0, :, hh:hh + 1]
        s_c = _dot_nt(q, ck_ref[0, :, cs].astype(BF16)) + fq - fkc_ref[0, 0, hh:hh + 1, :]
        s_n = jnp.where(col <= row, _dot_nt(q, k_ref[0, :, cs]) + fq - fkn_ref[0, 0, hh:hh + 1, :],
                        NEG_INF)
        m = jnp.maximum(jnp.max(s_c, axis=-1, keepdims=True), jnp.max(s_n, axis=-1, keepdims=True))
        p_c = jnp.exp2(s_c - m)
        p_n = jnp.exp2(s_n - m)
        l = jnp.sum(p_c, axis=-1, keepdims=True) + jnp.sum(p_n, axis=-1, keepdims=True)
        acc = _dot(p_c.astype(BF16), cv_ref[0, :, cs].astype(BF16)) + _dot(p_n.astype(BF16), v_ref[0, :, cs])
        outs.append((acc * (1.0 / l) * _silu(z_ref[0, :, cs].astype(F32))).astype(BF16))
    o_ref[0] = jnp.concatenate(outs, axis=-1)


def _fox_decode(qz, cache_k, cache_v, kb, vb, f_cum, heads_per_step=2):
    B, Q, W2 = qz.shape
    W = W2 // 2
    P = cache_k.shape[1]
    NH = f_cum.shape[-1]
    dim = W // NH
    hg = min(heads_per_step, NH)
    G = NH // hg
    fq = f_cum[:, P:].reshape(B, Q, G, hg).transpose(0, 2, 1, 3)
    fk = f_cum.transpose(0, 2, 1).reshape(B, G, hg, P + Q)
    wcols = hg * dim
    kern = functools.partial(_fox_decode_kernel, heads=hg, dim=dim)
    return pl.pallas_call(
        kern, grid=(B, G),
        in_specs=[pl.BlockSpec((1, Q, wcols), lambda b, g: (b, 0, g)),
                  pl.BlockSpec((1, Q, wcols), lambda b, g: (b, 0, G + g)),
                  pl.BlockSpec((1, P, wcols), lambda b, g: (b, 0, g)),
                  pl.BlockSpec((1, P, wcols), lambda b, g: (b, 0, g)),
                  pl.BlockSpec((1, Q, wcols), lambda b, g: (b, 0, g)),
                  pl.BlockSpec((1, Q, wcols), lambda b, g: (b, 0, g)),
                  pl.BlockSpec((1, 1, Q, hg), lambda b, g: (b, g, 0, 0)),
                  pl.BlockSpec((1, 1, hg, P), lambda b, g: (b, g, 0, 0)),
                  pl.BlockSpec((1, 1, hg, Q), lambda b, g: (b, g, 0, 0))],
        out_specs=pl.BlockSpec((1, Q, wcols), lambda b, g: (b, 0, g)),
        out_shape=jax.ShapeDtypeStruct((B, Q, W), BF16),
        compiler_params=_params("parallel", "parallel"), name="fox_decode",
    )(qz, qz, cache_k, cache_v, kb, vb, fq, fk[..., :P], fk[..., P:])


def _mlstm_layer(x, conv_hist, state, w, lead, chunk, c_stack, layer, n_layers):
    (norm_g, w_in, conv_w, conv_b, wq, wk, wv, w_gate, b_gate, out_g, skip, w_out) = w
    B, S, _ = x.shape
    H, DH, _ = wq.shape
    AI = H * DH
    CW = conv_w.shape[0]
    assert S >= CW - 1 and CW - 1 <= CONV_PAD
    (xz,) = _norm_matmul(x, norm_g, w_in.astype(BF16), out_dtypes=[BF16])
    buf = jnp.zeros((B, CONV_PAD, AI), F32)
    if conv_hist is not None:
        buf = buf.at[:, CONV_PAD - (CW - 1):].set(conv_hist.astype(F32))
    q, k, v, xc, gates = _conv_qkv(xz, buf, conv_w, conv_b, wq, wk, wv, w_gate, b_gate)
    hz, c_stack, n, m = _mlstm(q, k, v, xc, xz, gates, out_g, skip.reshape(H, DH), state, lead,
                               chunk, c_stack, layer, n_layers)
    x_new = _matmul_residual(hz, w_out.astype(BF16), x)
    new_hist = xz[:, S - (CW - 1):, :AI].astype(F32)
    return x_new, new_hist, c_stack, n, m


def _shared_kv(x, kv_norm_g, w_kvf, b_f, k_norm_g):
    NH, dim = k_norm_g.shape
    W = NH * dim
    wb = w_kvf.astype(BF16)
    k32, kb, logf = _norm_matmul(
        x, kv_norm_g, wb[:, :W], out_dtypes=[F32, BF16], head_gain=k_norm_g.reshape(1, W),
        n_head_cols=W, head_dim=dim, gate_w=wb[:, 2 * W:], gate_b=b_f, split_heads_first=True)
    v32, vb = _norm_matmul(x, kv_norm_g, wb[:, W:2 * W], out_dtypes=[F32, BF16], head_dim=dim,
                           split_heads_first=True)
    return k32, kb, v32, vb, logf


def _fox_in(x, norm_g, w_in, q_norm_g):
    NH, dim = q_norm_g.shape
    W = NH * dim
    gain = jnp.concatenate([q_norm_g.reshape(1, W).astype(F32) * (float(dim) ** -0.5 * LOG2E),
                            jnp.ones((1, W), F32)], axis=-1)
    (qz,) = _norm_matmul(x, norm_g, w_in.astype(BF16), out_dtypes=[BF16], head_gain=gain,
                         n_head_cols=W, head_dim=dim)
    return qz


def kernel(x_prompt, x_sample, cache_k, cache_v, cache_logf, state_C, state_n, state_m, state_conv,
           meta_tokens, a_norm_g, a_w_in, a_conv_w, a_conv_b, a_wq, a_wk, a_wv, a_w_gate, a_b_gate,
           a_out_g, a_skip, a_w_out, kv_norm_g, w_kvf, b_f, k_norm_g, b_norm_g, b_w_in, q_norm_g,
           b_w_out):
    B, SEQ, D = x_prompt.shape
    DB, DEC, _ = x_sample.shape
    NM = meta_tokens.shape[0]
    N_A = a_norm_g.shape[0]
    N_B = b_norm_g.shape[0]
    NH, dim = k_norm_g.shape
    W = NH * dim
    P = cache_k.shape[1]
    chunk = _tile(SEQ, 256)

    xp = jnp.concatenate(
        [jnp.broadcast_to(meta_tokens.astype(x_prompt.dtype)[None], (B, NM, D)), x_prompt], axis=1)
    xs = x_sample
    p_state, s_state = [], []
    p_C = s_C = None
    for i in range(N_A):
        w = (a_norm_g[i], a_w_in[i], a_conv_w[i], a_conv_b[i], a_wq[i], a_wk[i], a_wv[i],
             a_w_gate[i], a_b_gate[i], a_out_g[i], a_skip[i], a_w_out[i])
        xp, hist, p_C, n, m = _mlstm_layer(xp, None, None, w, NM, chunk, p_C, i, N_A)
        p_state.append((hist, n, m))
        xs, hist, s_C, n, m = _mlstm_layer(xs, state_conv[i], (state_C[i], state_n[i], state_m[i]),
                                           w, 0, DEC, s_C, i, N_A)
        s_state.append((hist, n, m))

    pk32, pkb, pv32, pvb, p_logf = _shared_kv(xp, kv_norm_g, w_kvf, b_f, k_norm_g)
    sk32, skb, sv32, svb, s_logf = _shared_kv(xs, kv_norm_g, w_kvf, b_f, k_norm_g)
    fp = _cumsum(p_logf, LOG2E)
    fs = _cumsum(jnp.concatenate([cache_logf.astype(F32), s_logf], axis=1), LOG2E)
    ck = cache_k.reshape(DB, P, W)
    cv = cache_v.reshape(DB, P, W)
    xp = xp[:, NM:]
    for j in range(N_B):
        qz = _fox_in(xp, b_norm_g[j], b_w_in[j], q_norm_g[j])
        xp = _matmul_residual(_fox_prompt(qz, pkb, pvb, fp, NM), b_w_out[j].astype(BF16), xp)
        qz = _fox_in(xs, b_norm_g[j], b_w_in[j], q_norm_g[j])
        xs = _matmul_residual(_fox_decode(qz, ck, cv, skb, svb, fs), b_w_out[j].astype(BF16), xs)

    def stack(states, idx):
        return jnp.stack([st[idx] for st in states])

    return (xp, xs, pk32, pv32, p_logf,
            p_C, stack(p_state, 1), stack(p_state, 2), stack(p_state, 0),
            sk32, sv32, s_logf,
            s_C, stack(s_state, 1), stack(s_state, 2), stack(s_state, 0))
```

```python
import functools

import jax
import jax.numpy as jnp
from jax import lax
from jax.experimental import pallas as pl
from jax.experimental.pallas import tpu as pltpu

F32 = jnp.float32
BF16 = jnp.bfloat16
EPS = 1e-6
CONV_PAD = 8
BF16_ROWS = 16
LANES = 128
VMEM_LIMIT_BYTES = 56 * 1024 * 1024
HIGHEST = lax.Precision.HIGHEST
NEG_INF = float("-inf")
LOG2E = 1.4426950408889634


def _params(*sem):
    return pltpu.CompilerParams(dimension_semantics=sem, vmem_limit_bytes=VMEM_LIMIT_BYTES)


def _tile(n, cap, mult=BF16_ROWS):
    best = None
    for t in range(mult, min(n, cap) + 1, mult):
        if n % t == 0:
            best = t
    return best if best is not None else n


def _dot(a, b):
    return jnp.dot(a, b, preferred_element_type=F32)


def _dot_nt(a, b):
    return lax.dot_general(a, b, (((1,), (1,)), ((), ())), preferred_element_type=F32)


def _dot_tn(a, b):
    return lax.dot_general(a, b, (((0,), (0,)), ((), ())), preferred_element_type=F32)


def _silu(x):
    return x / (1.0 + jnp.exp(-x))


def _log_sigmoid(x):
    return jnp.minimum(x, 0.0) - jnp.log1p(jnp.exp(-jnp.abs(x)))


def _norm_matmul_kernel(*refs, n_out, n_head_blocks, head_dim, has_gain, has_gate):
    x_ref, g_ref, w_ref = refs[:3]
    pos = 3
    if has_gain:
        gain_ref = refs[pos]
        pos += 1
    if has_gate:
        gw_ref, gb_ref = refs[pos:pos + 2]
        pos += 2
    out_refs = refs[pos:pos + n_out]
    pos += n_out
    if has_gate:
        gate_out_ref = refs[pos]
        pos += 1
    xn_ref = refs[pos]
    j = pl.program_id(2)

    @pl.when(j == 0)
    def _():
        x = x_ref[0].astype(F32)
        ms = jnp.mean(x * x, axis=-1, keepdims=True)
        xn_ref[...] = (x * lax.rsqrt(ms + EPS) * g_ref[...]).astype(BF16)
        if has_gate:
            gate_out_ref[0] = _log_sigmoid(_dot(xn_ref[...], gw_ref[...]) + gb_ref[...])

    acc = _dot(xn_ref[...], w_ref[...])

    def store(val):
        for o in out_refs:
            if len(o.shape) == 4:
                o[0] = val.reshape(o.shape[1:]).astype(o.dtype)
            else:
                o[0] = val.astype(o.dtype)

    if has_gain:
        def normed():
            cols = []
            for c in range(acc.shape[1] // head_dim):
                blk = acc[:, c * head_dim:(c + 1) * head_dim]
                ms = jnp.mean(blk * blk, axis=-1, keepdims=True)
                cols.append(blk * lax.rsqrt(ms + EPS))
            return jnp.concatenate(cols, axis=-1) * gain_ref[...]

        @pl.when(j < n_head_blocks)
        def _():
            store(normed())

        @pl.when(j >= n_head_blocks)
        def _():
            store(acc)
    else:
        store(acc)


def _norm_matmul(x, g, w, *, out_dtypes, head_gain=None, n_head_cols=0, head_dim=128,
                 gate_w=None, gate_b=None, split_heads_first=False):
    lead_shape = x.shape[:2]
    x = x.reshape(1, -1, x.shape[-1])
    B, S, D = x.shape
    N = w.shape[1]
    tm = _tile(S, 1024)
    tn = _tile(N, 1024, 128)
    if head_gain is not None:
        tn = _tile(n_head_cols, tn, 128)
        assert N % tn == 0
    has_gain = head_gain is not None
    has_gate = gate_w is not None
    in_specs = [
        pl.BlockSpec((1, tm, D), lambda b, i, j: (b, i, 0)),
        pl.BlockSpec((1, D), lambda b, i, j: (0, 0)),
        pl.BlockSpec((D, tn), lambda b, i, j: (0, j)),
    ]
    args = [x, g.reshape(1, D).astype(F32), w]
    if has_gain:
        in_specs.append(pl.BlockSpec((1, tn), lambda b, i, j: (0, j)))
        args.append(head_gain.astype(F32))
    if has_gate:
        G = gate_w.shape[1]
        in_specs += [pl.BlockSpec((D, G), lambda b, i, j: (0, 0)),
                     pl.BlockSpec((1, G), lambda b, i, j: (0, 0))]
        args += [gate_w, gate_b.reshape(1, G).astype(F32)]
    out_shape = [jax.ShapeDtypeStruct((B, S, N), dt) for dt in out_dtypes]
    out_specs = [pl.BlockSpec((1, tm, tn), lambda b, i, j: (b, i, j)) for _ in out_dtypes]
    if split_heads_first:
        out_shape[0] = jax.ShapeDtypeStruct((B, S, N // head_dim, head_dim), out_dtypes[0])
        out_specs[0] = pl.BlockSpec((1, tm, tn // head_dim, head_dim), lambda b, i, j: (b, i, j, 0))
    if has_gate:
        out_shape.append(jax.ShapeDtypeStruct((B, S, G), F32))
        out_specs.append(pl.BlockSpec((1, tm, G), lambda b, i, j: (b, i, 0)))
    kern = functools.partial(
        _norm_matmul_kernel, n_out=len(out_dtypes), n_head_blocks=n_head_cols // tn,
        head_dim=head_dim, has_gain=has_gain, has_gate=has_gate)
    outs = pl.pallas_call(
        kern, grid=(B, S // tm, N // tn), in_specs=in_specs, out_specs=out_specs,
        out_shape=out_shape, scratch_shapes=[pltpu.VMEM((tm, D), BF16)],
        compiler_params=_params("parallel", "parallel", "arbitrary"), name="norm_matmul",
    )(*args)
    return [o.reshape(lead_shape + o.shape[2:]) for o in outs]


def _matmul_residual_kernel(a_ref, w_ref, x_ref, o_ref):
    o_ref[0] = x_ref[0] + _dot(a_ref[0], w_ref[...])


def _matmul_residual(a, w, x):
    out_shape = x.shape
    a = a.reshape(1, -1, a.shape[-1])
    x = x.reshape(1, -1, x.shape[-1])
    B, S, K = a.shape
    N = w.shape[1]
    tm = _tile(S, 1024)
    tn = _tile(N, 1024, 128)
    return pl.pallas_call(
        _matmul_residual_kernel, grid=(B, S // tm, N // tn),
        in_specs=[pl.BlockSpec((1, tm, K), lambda b, i, j: (b, i, 0)),
                  pl.BlockSpec((K, tn), lambda b, i, j: (0, j)),
                  pl.BlockSpec((1, tm, tn), lambda b, i, j: (b, i, j))],
        out_specs=pl.BlockSpec((1, tm, tn), lambda b, i, j: (b, i, j)),
        out_shape=jax.ShapeDtypeStruct((B, S, N), F32),
        compiler_params=_params("parallel", "parallel", "arbitrary"), name="matmul_residual",
    )(a, w, x).reshape(out_shape)


def _fold_gate_kernel(wq_ref, wk_ref, wv_ref, wg_ref, a_ref, b_ref, *, k_scale):
    def hdot(a, b):
        return jnp.dot(a, b, precision=HIGHEST, preferred_element_type=F32)

    a_ref[0, 0] = hdot(wq_ref[0, 0], wg_ref[0, 0, 0]) + hdot(wk_ref[0, 0], wg_ref[0, 1, 0]) * k_scale
    b_ref[0, 0] = hdot(wv_ref[0, 0], wg_ref[0, 2, 0])


def _fold_gate_weights(wq, wk, wv, w_gate):
    NA, H, DH, _ = wq.shape
    G = w_gate.shape[-1]
    w_spec = pl.BlockSpec((1, 1, DH, DH), lambda i, h: (i, h, 0, 0))
    o_spec = pl.BlockSpec((1, 1, DH, G), lambda i, h: (i, h, 0, 0))
    out = jax.ShapeDtypeStruct((NA, H, DH, G), F32)
    return pl.pallas_call(
        functools.partial(_fold_gate_kernel, k_scale=float(DH) ** -0.5), grid=(NA, H),
        in_specs=[w_spec, w_spec, w_spec,
                  pl.BlockSpec((1, 3, 1, DH, G), lambda i, h: (i, 0, h, 0, 0))],
        out_specs=[o_spec, o_spec], out_shape=[out, out],
        compiler_params=_params("parallel", "parallel"), name="fold_gate_weights",
    )(wq, wk, wv, w_gate.reshape(NA, 3, H, DH, G))


def _conv_gates_kernel(xm_ref, buf_ref, cw_ref, cb_ref, wa_ref, wb_ref, bg_ref,
                       xc_ref, g_ref, seq_ref, *, rows, cum_rows, conv_w, n_heads):
    h = pl.program_id(1)
    S = xm_ref.shape[1]
    seq_ref[0:CONV_PAD, :] = buf_ref[0]
    seq_ref[CONV_PAD:CONV_PAD + S, :] = xm_ref[0].astype(F32)
    for r in range(S // rows):
        base = r * rows
        acc = jnp.broadcast_to(cb_ref[0], (rows, cb_ref.shape[-1]))
        for i in range(conv_w):
            start = base + CONV_PAD - (conv_w - 1) + i
            acc = acc + seq_ref[start:start + rows, :] * cw_ref[0, i:i + 1, :]
        xc = _silu(acc).astype(BF16)
        xc_ref[0, base:base + rows, :] = xc
        part = _dot(xc, wa_ref[0]) + _dot(xm_ref[0, base:base + rows, :], wb_ref[0])

        @pl.when(h == 0)
        def _():
            g_ref[0, base:base + rows, :] = part + bg_ref[...]

        @pl.when(h > 0)
        def _():
            g_ref[0, base:base + rows, :] = g_ref[0, base:base + rows, :] + part

    @pl.when(h == n_heads - 1)
    def _():
        r = lax.broadcasted_iota(jnp.int32, (cum_rows, cum_rows), 0)
        c = lax.broadcasted_iota(jnp.int32, (cum_rows, cum_rows), 1)
        trilf = (c <= r).astype(F32)
        lane = lax.broadcasted_iota(jnp.int32, (cum_rows, g_ref.shape[-1]), 1)
        carry = jnp.zeros((1, g_ref.shape[-1]), F32)
        for i in range(S // cum_rows):
            g = g_ref[0, i * cum_rows:(i + 1) * cum_rows, :]
            lf = jnp.where(lane >= n_heads, _log_sigmoid(g), 0.0)
            cum = jnp.dot(trilf, lf, precision=HIGHEST, preferred_element_type=F32) + carry
            g_ref[0, i * cum_rows:(i + 1) * cum_rows, :] = jnp.where(lane >= n_heads, cum, g)
            carry = cum[cum_rows - 1:cum_rows, :]


def _conv_gates(xz, buf, conv_w, conv_b, w_xc, w_xm, b_gate):
    B, S, _ = xz.shape
    H, DH, G = w_xc.shape
    AI = H * DH
    CW = conv_w.shape[0]
    rows = _tile(S, 768)
    act_spec = pl.BlockSpec((1, S, DH), lambda b, h: (b, 0, h))
    w_spec = pl.BlockSpec((1, DH, G), lambda b, h: (h, 0, 0))
    kern = functools.partial(_conv_gates_kernel, rows=rows, cum_rows=_tile(S, 512, 8), conv_w=CW,
                             n_heads=H)
    return pl.pallas_call(
        kern, grid=(B, H),
        in_specs=[act_spec,
                  pl.BlockSpec((1, CONV_PAD, DH), lambda b, h: (b, 0, h)),
                  pl.BlockSpec((1, CW, DH), lambda b, h: (h, 0, 0)),
                  pl.BlockSpec((1, 1, DH), lambda b, h: (h, 0, 0)),
                  w_spec, w_spec,
                  pl.BlockSpec((1, G), lambda b, h: (0, 0))],
        out_specs=[act_spec, pl.BlockSpec((1, S, G), lambda b, h: (b, 0, 0))],
        out_shape=[jax.ShapeDtypeStruct((B, S, AI), BF16), jax.ShapeDtypeStruct((B, S, G), F32)],
        scratch_shapes=[pltpu.VMEM((S + CONV_PAD, DH), F32)],
        compiler_params=_params("parallel", "arbitrary"), name="conv_gates",
    )(xz, buf,
      conv_w.reshape(CW, H, DH).transpose(1, 0, 2).astype(F32),
      conv_b.reshape(H, 1, DH).astype(F32), w_xc, w_xm, b_gate.reshape(1, G).astype(F32))


def _mlstm_chunk(xm, xc, z, wq, wk, wv, g_rows, g_cols, head, n_heads, og, skip,
                 C_ref, n_ref, m_ref, b_ref):
    L = xm.shape[0]
    q = _dot(xc, wq).astype(BF16)
    k = _dot(xc, wk).astype(BF16)
    v = _dot(xm, wv).astype(BF16)
    row = lax.broadcasted_iota(jnp.int32, (L, L), 0)
    col = lax.broadcasted_iota(jnp.int32, (L, L), 1)
    tril = col <= row
    lane = lax.broadcasted_iota(jnp.int32, g_cols.shape, 1)
    ig_col = jnp.sum(jnp.where(lane == head, g_cols, 0.0), axis=-1, keepdims=True)
    f_col = jnp.sum(jnp.where(lane == head + n_heads, g_cols, 0.0), axis=-1, keepdims=True)
    ig_row = g_rows[0:1, :]
    f_row = g_rows[1:2, :]
    m_prev = m_ref[...]
    f_prev = b_ref[...]
    log_d = jnp.where(tril, f_col - f_row + ig_row, NEG_INF)
    inter = f_col - f_prev + m_prev
    m_t = jnp.maximum(jnp.max(log_d, axis=-1, keepdims=True), inter)
    d = jnp.exp(log_d - m_t)
    a = jnp.exp(inter - m_t)
    s = _dot_nt(q, k) * d
    C = C_ref[...]
    n = n_ref[...]
    num = _dot(s.astype(BF16), v) + a * _dot(q, C.astype(BF16))
    qn = _dot_nt(q, jnp.broadcast_to(n, (8, n.shape[1])).astype(BF16))[:, 0:1]
    den = jnp.sum(s, axis=-1, keepdims=True) + a * qn
    hid = num * (1.0 / jnp.maximum(jnp.abs(den), jnp.exp(-m_t)))
    m_new = m_t[L - 1:L, :]
    f_last = f_row[:, L - 1:L]
    decay = jnp.exp(f_last - f_prev + m_prev - m_new)
    w_col = jnp.exp(f_last - f_col + ig_col - m_new)
    C_ref[...] = decay * C + _dot_tn(k, (w_col * v.astype(F32)).astype(BF16))
    n_ref[...] = decay * n + jnp.sum(w_col * k.astype(F32), axis=0, keepdims=True)
    m_ref[...] = m_new
    b_ref[...] = f_last
    hc = hid - jnp.mean(hid, axis=-1, keepdims=True)
    hn = hc * lax.rsqrt(jnp.mean(hc * hc, axis=-1, keepdims=True) + EPS) * og
    return ((hn + skip * xc.astype(F32)) * _silu(z.astype(F32))).astype(BF16)


def _mlstm_kernel(*refs, lead, chunk, n_chunks, n_heads, has_state):
    xm_ref, xc_ref, z_ref, wq_ref, wk_ref, wv_ref = refs[:6]
    pos = 6
    if lead:
        gl_ref = refs[pos]
        pos += 1
    gm_ref, gc_ref, og_ref, skip_ref = refs[pos:pos + 4]
    pos += 4
    if has_state:
        c0_ref, n0_ref, m0_ref = refs[pos:pos + 3]
        pos += 3
    pos += 1
    o_ref, C_ref, n_ref, m_ref, b_ref = refs[pos:pos + 5]
    C_st, n_st, m_st = C_ref.at[0, 0, 0], n_ref.at[0, 0], m_ref.at[0, 0]
    if has_state:
        C_st[...] = c0_ref[0, 0, 0]
        n_st[...] = n0_ref[0, 0, 0]
        m_st[...] = m0_ref[0, 0, 0]
    else:
        C_st[...] = jnp.zeros(C_st.shape, F32)
        n_st[...] = jnp.zeros(n_st.shape, F32)
        m_st[...] = jnp.zeros(m_st.shape, F32)
    b_ref[...] = jnp.zeros(b_ref.shape, F32)
    head = pl.program_id(1)
    og = og_ref[0]
    skip = skip_ref[0]

    def run(rows, g_rows):
        o_ref[0, rows, :] = _mlstm_chunk(
            xm_ref[0, rows, :], xc_ref[0, rows, :], z_ref[0, rows, :], wq_ref[0], wk_ref[0],
            wv_ref[0], g_rows, gc_ref[0, rows, :], head, n_heads, og, skip,
            C_st, n_st, m_st, b_ref)

    if lead:
        run(pl.ds(0, lead), gl_ref[0, 0])

    def body(c, carry):
        run(pl.ds(pl.multiple_of(lead + c * chunk, BF16_ROWS), chunk), gm_ref[0, 0, c])
        return carry

    lax.fori_loop(0, n_chunks, body, 0, unroll=_tile(n_chunks, 2, 1))


def _mlstm(xz, xc, gates, wq, wk, wv, out_g, skip, state, lead, chunk, c_stack, layer, n_layers):
    B, S, AI = xc.shape
    H, DH = out_g.shape
    n_chunks = (S - lead) // chunk
    assert lead + n_chunks * chunk == S
    g = gates.reshape(B, S, 2, H).transpose(0, 3, 2, 1)
    g_main = g[..., lead:].reshape(B, H, 2, n_chunks, chunk).transpose(0, 1, 3, 2, 4)
    act_spec = pl.BlockSpec((1, S, DH), lambda b, h: (b, 0, h))
    w_spec = pl.BlockSpec((1, DH, DH), lambda b, h: (h, 0, 0))
    in_specs = [act_spec, act_spec, pl.BlockSpec((1, S, DH), lambda b, h: (b, 0, H + h)),
                w_spec, w_spec, w_spec]
    args = [xz, xc, xz, wq, wk, wv]
    if lead:
        in_specs.append(pl.BlockSpec((1, 1, 2, lead), lambda b, h: (b, h, 0, 0)))
        args.append(g[..., :lead])
    in_specs += [pl.BlockSpec((1, 1, n_chunks, 2, chunk), lambda b, h: (b, h, 0, 0, 0)),
                 pl.BlockSpec((1, S, 2 * H), lambda b, h: (b, 0, 0)),
                 pl.BlockSpec((1, 1, DH), lambda b, h: (h, 0, 0)),
                 pl.BlockSpec((1, 1, DH), lambda b, h: (h, 0, 0))]
    args += [g_main, gates, out_g.reshape(H, 1, DH).astype(F32), skip.reshape(H, 1, DH).astype(F32)]
    c_spec = pl.BlockSpec((1, 1, DH, DH), lambda b, h: (b, h, 0, 0))
    n_spec = pl.BlockSpec((1, 1, 1, DH), lambda b, h: (b, h, 0, 0))
    m_spec = pl.BlockSpec((1, 1, 1, 1), lambda b, h: (b, h, 0, 0))
    has_state = state is not None
    if has_state:
        C0, n0, m0 = state
        NL = C0.shape[0]
        in_specs += [pl.BlockSpec((1, 1, 1, DH, DH), lambda b, h: (layer, b, h, 0, 0)),
                     pl.BlockSpec((1, 1, 1, 1, DH), lambda b, h: (layer, b, h, 0, 0)),
                     pl.BlockSpec((1, 1, 1, 1, 1), lambda b, h: (layer, b, h, 0, 0))]
        args += [C0.astype(F32), n0.reshape(NL, B, H, 1, DH).astype(F32),
                 m0.reshape(NL, B, H, 1, 1).astype(F32)]
    if c_stack is None:
        c_stack = jnp.zeros((n_layers, B, H, DH, DH), F32)
    aliases = {len(args): 1}
    in_specs.append(pl.BlockSpec(memory_space=pl.ANY))
    args.append(c_stack)
    kern = functools.partial(_mlstm_kernel, lead=lead, chunk=chunk, n_chunks=n_chunks, n_heads=H,
                             has_state=has_state)
    hz, C, n, m = pl.pallas_call(
        kern, grid=(B, H), in_specs=in_specs,
        out_specs=[act_spec,
                   pl.BlockSpec((1, 1, 1, DH, DH), lambda b, h: (layer, b, h, 0, 0)),
                   n_spec, m_spec],
        out_shape=[jax.ShapeDtypeStruct((B, S, AI), BF16),
                   jax.ShapeDtypeStruct((n_layers, B, H, DH, DH), F32),
                   jax.ShapeDtypeStruct((B, H, 1, DH), F32),
                   jax.ShapeDtypeStruct((B, H, 1, 1), F32)],
        scratch_shapes=[pltpu.VMEM((1, 1), F32)],
        input_output_aliases=aliases,
        compiler_params=_params("parallel", "parallel"), name="mlstm",
    )(*args)
    return hz, C, n.reshape(B, H, DH), m.reshape(B, H)


def _cumsum_kernel(x_ref, o_ref, *, rows, scale):
    S, G = x_ref.shape[1:]
    r = lax.broadcasted_iota(jnp.int32, (rows, rows), 0)
    c = lax.broadcasted_iota(jnp.int32, (rows, rows), 1)
    trilf = (c <= r).astype(F32)
    carry = jnp.zeros((1, G), F32)
    for i in range(S // rows):
        blk = jnp.dot(trilf, x_ref[0, i * rows:(i + 1) * rows, :], precision=HIGHEST,
                      preferred_element_type=F32) + carry
        o_ref[0, i * rows:(i + 1) * rows, :] = blk * scale
        carry = blk[rows - 1:rows, :]


def _cumsum(x, scale):
    B, S, G = x.shape
    rows = _tile(S, 512, 8)
    spec = pl.BlockSpec((1, S, G), lambda b: (b, 0, 0))
    return pl.pallas_call(
        functools.partial(_cumsum_kernel, rows=rows, scale=scale), grid=(B,), in_specs=[spec],
        out_specs=spec, out_shape=jax.ShapeDtypeStruct((B, S, G), F32),
        compiler_params=_params("parallel"), name="cumsum",
    )(x)


def _lane_fold(op, acc, x):
    for c in range(x.shape[1] // LANES):
        acc = op(acc, x[:, c * LANES:(c + 1) * LANES])
    return acc


def _fox_prompt_kernel(q_ref, z_ref, k_ref, v_ref, fq_ref, fkl_ref, fkm_ref, o_ref,
                       s_ref, sl_ref, fq_rep, m_ref, l_ref, acc_ref, *, lead, blk, n_blk, heads,
                       dim):
    n_lane = blk // LANES
    cols = [slice(hh * dim, (hh + 1) * dim) for hh in range(heads)]

    def key_rows(j):
        return pl.ds(pl.multiple_of(lead + j * blk, BF16_ROWS), blk)

    def tiled(x):
        return jnp.concatenate([x] * n_lane, axis=1)

    def q_block(i, carry):
        qrows = pl.ds(pl.multiple_of(i * blk, blk), blk)

        def logits(hh, j):
            return (_dot_nt(q_ref[0, qrows, cols[hh]], k_ref[0, key_rows(j), cols[hh]])
                    + tiled(fq_rep[hh]) - fkm_ref[0, 0, j, hh:hh + 1, :])

        for hh in range(heads):
            fq_rep[hh] = jnp.broadcast_to(fq_ref[0, 0, qrows, hh:hh + 1], (blk, LANES))
        if lead:
            for hh in range(heads):
                s = (_dot_nt(q_ref[0, qrows, cols[hh]], k_ref[0, 0:LANES, cols[hh]])
                     + fq_rep[hh] - fkl_ref[0, 0, hh:hh + 1, :])
                sl_ref[hh] = s
                m_ref[hh] = s
        else:
            m_ref[...] = jnp.full(m_ref.shape, NEG_INF, F32)

        def pass1(j, c):
            for hh in range(heads):
                s = logits(hh, j)
                s_ref[hh, j] = s
                m_ref[hh] = _lane_fold(jnp.maximum, m_ref[hh], s)
            return c

        lax.fori_loop(0, i, pass1, 0)
        row = lax.broadcasted_iota(jnp.int32, (blk, blk), 0)
        col = lax.broadcasted_iota(jnp.int32, (blk, blk), 1)
        for hh in range(heads):
            s = jnp.where(col <= row, logits(hh, i), NEG_INF)
            s_ref[hh, i] = s
            m = jnp.max(_lane_fold(jnp.maximum, m_ref[hh], s), axis=-1, keepdims=True)
            m_ref[hh] = jnp.broadcast_to(m, (blk, LANES))
        if lead:
            for hh in range(heads):
                p = jnp.exp2(sl_ref[hh] - m_ref[hh])
                l_ref[hh] = p
                acc_ref[hh] = _dot(p.astype(BF16), v_ref[0, 0:LANES, cols[hh]])
        else:
            l_ref[...] = jnp.zeros(l_ref.shape, F32)
            acc_ref[...] = jnp.zeros(acc_ref.shape, F32)

        def pass2(j, c):
            for hh in range(heads):
                p = jnp.exp2(s_ref[hh, j] - tiled(m_ref[hh]))
                l_ref[hh] = _lane_fold(jnp.add, l_ref[hh], p)
                acc_ref[hh] += _dot(p.astype(BF16), v_ref[0, key_rows(j), cols[hh]])
            return c

        lax.fori_loop(0, i + 1, pass2, 0)
        for hh in range(heads):
            l = jnp.sum(l_ref[hh], axis=-1, keepdims=True)
            o_ref[0, qrows, cols[hh]] = (acc_ref[hh] * (1.0 / l)
                                         * _silu(z_ref[0, qrows, cols[hh]].astype(F32))).astype(BF16)
        return carry

    lax.fori_loop(0, n_blk, q_block, 0)


def _fox_prompt(qz, kb, vb, f_cum, lead, heads_per_step=4, blk=512):
    B, SQ, W2 = qz.shape
    W = W2 // 2
    NH = f_cum.shape[-1]
    dim = W // NH
    hg = min(heads_per_step, NH)
    G = NH // hg
    blk = _tile(SQ, blk, LANES)
    n_blk = SQ // blk
    SK = lead + SQ
    assert lead <= LANES <= SK
    fq = f_cum[:, lead:].reshape(B, SQ, G, hg).transpose(0, 2, 1, 3)
    fk = f_cum.transpose(0, 2, 1).reshape(B, G, hg, SK)
    fk_lead = jnp.pad(fk[..., :lead], ((0, 0), (0, 0), (0, 0), (0, LANES - lead)),
                      constant_values=float("inf"))
    fk_main = fk[..., lead:].reshape(B, G, hg, n_blk, blk).transpose(0, 1, 3, 2, 4)
    wcols = hg * dim
    kern = functools.partial(_fox_prompt_kernel, lead=lead, blk=blk, n_blk=n_blk, heads=hg, dim=dim)
    return pl.pallas_call(
        kern, grid=(B, G),
        in_specs=[pl.BlockSpec((1, SQ, wcols), lambda b, g: (b, 0, g)),
                  pl.BlockSpec((1, SQ, wcols), lambda b, g: (b, 0, G + g)),
                  pl.BlockSpec((1, SK, wcols), lambda b, g: (b, 0, g)),
                  pl.BlockSpec((1, SK, wcols), lambda b, g: (b, 0, g)),
                  pl.BlockSpec((1, 1, SQ, hg), lambda b, g: (b, g, 0, 0)),
                  pl.BlockSpec((1, 1, hg, LANES), lambda b, g: (b, g, 0, 0)),
                  pl.BlockSpec((1, 1, n_blk, hg, blk), lambda b, g: (b, g, 0, 0, 0))],
        out_specs=pl.BlockSpec((1, SQ, wcols), lambda b, g: (b, 0, g)),
        out_shape=jax.ShapeDtypeStruct((B, SQ, W), BF16),
        scratch_shapes=[pltpu.VMEM((hg, n_blk, blk, blk), F32)]
        + [pltpu.VMEM((hg, blk, LANES), F32)] * 4 + [pltpu.VMEM((hg, blk, dim), F32)],
        compiler_params=_params("parallel", "parallel"), name="fox_prompt",
    )(qz, qz, kb, vb, fq, fk_lead, fk_main)


def _fox_decode_kernel(q_ref, z_ref, ck_ref, cv_ref, k_ref, v_ref, fq_ref, fkc_ref, fkn_ref, o_ref,
                       *, heads, dim):
    Q = q_ref.shape[1]
    row = lax.broadcasted_iota(jnp.int32, (Q, Q), 0)
    col = lax.broadcasted_iota(jnp.int32, (Q, Q), 1)
    outs = []
    for hh in range(heads):
        cs = slice(hh * dim, (hh + 1) * dim)
        q = q_ref[0, :, cs]
        fq = fq_ref[0, 0, :, hh:hh + 1]
        s_c = _dot_nt(q, ck_ref[0, :, cs].astype(BF16)) + fq - fkc_ref[0, 0, hh:hh + 1, :]
        s_n = jnp.where(col <= row, _dot_nt(q, k_ref[0, :, cs]) + fq - fkn_ref[0, 0, hh:hh + 1, :],
                        NEG_INF)
        m = jnp.maximum(jnp.max(s_c, axis=-1, keepdims=True), jnp.max(s_n, axis=-1, keepdims=True))
        p_c = jnp.exp2(s_c - m)
        p_n = jnp.exp2(s_n - m)
        l = jnp.sum(p_c, axis=-1, keepdims=True) + jnp.sum(p_n, axis=-1, keepdims=True)
        acc = _dot(p_c.astype(BF16), cv_ref[0, :, cs].astype(BF16)) + _dot(p_n.astype(BF16), v_ref[0, :, cs])
        outs.append((acc * (1.0 / l) * _silu(z_ref[0, :, cs].astype(F32))).astype(BF16))
    o_ref[0] = jnp.concatenate(outs, axis=-1)


def _fox_decode(qz, cache_k, cache_v, kb, vb, f_cum, heads_per_step=4):
    B, Q, W2 = qz.shape
    W = W2 // 2
    P = cache_k.shape[1]
    NH = f_cum.shape[-1]
    dim = W // NH
    hg = min(heads_per_step, NH)
    G = NH // hg
    fq = f_cum[:, P:].reshape(B, Q, G, hg).transpose(0, 2, 1, 3)
    fk = f_cum.transpose(0, 2, 1).reshape(B, G, hg, P + Q)
    wcols = hg * dim
    kern = functools.partial(_fox_decode_kernel, heads=hg, dim=dim)
    return pl.pallas_call(
        kern, grid=(B, G),
        in_specs=[pl.BlockSpec((1, Q, wcols), lambda b, g: (b, 0, g)),
                  pl.BlockSpec((1, Q, wcols), lambda b, g: (b, 0, G + g)),
                  pl.BlockSpec((1, P, wcols), lambda b, g: (b, 0, g)),
                  pl.BlockSpec((1, P, wcols), lambda b, g: (b, 0, g)),
                  pl.BlockSpec((1, Q, wcols), lambda b, g: (b, 0, g)),
                  pl.BlockSpec((1, Q, wcols), lambda b, g: (b, 0, g)),
                  pl.BlockSpec((1, 1, Q, hg), lambda b, g: (b, g, 0, 0)),
                  pl.BlockSpec((1, 1, hg, P), lambda b, g: (b, g, 0, 0)),
                  pl.BlockSpec((1, 1, hg, Q), lambda b, g: (b, g, 0, 0))],
        out_specs=pl.BlockSpec((1, Q, wcols), lambda b, g: (b, 0, g)),
        out_shape=jax.ShapeDtypeStruct((B, Q, W), BF16),
        compiler_params=_params("parallel", "parallel"), name="fox_decode",
    )(qz, qz, cache_k, cache_v, kb, vb, fq, fk[..., :P], fk[..., P:])


def _mlstm_layer(x, conv_hist, state, w, lead, chunk, c_stack, layer, n_layers):
    (norm_g, w_in, conv_w, conv_b, wq, wk, wv, w_xc, w_xm, b_gate, out_g, skip, w_out) = w
    B, S, _ = x.shape
    H, DH, _ = wq.shape
    AI = H * DH
    CW = conv_w.shape[0]
    assert S >= CW - 1 and CW - 1 <= CONV_PAD
    (xz,) = _norm_matmul(x, norm_g, w_in.astype(BF16), out_dtypes=[BF16])
    buf = jnp.zeros((B, CONV_PAD, AI), F32)
    if conv_hist is not None:
        buf = buf.at[:, CONV_PAD - (CW - 1):].set(conv_hist.astype(F32))
    xc, gates = _conv_gates(xz, buf, conv_w, conv_b, w_xc.astype(BF16), w_xm.astype(BF16), b_gate)
    hz, c_stack, n, m = _mlstm(
        xz, xc, gates, wq.astype(BF16), (wk * float(DH) ** -0.5).astype(BF16), wv.astype(BF16),
        out_g, skip.reshape(H, DH), state, lead, chunk, c_stack, layer, n_layers)
    x_new = _matmul_residual(hz, w_out.astype(BF16), x)
    new_hist = xz[:, S - (CW - 1):, :AI].astype(F32)
    return x_new, new_hist, c_stack, n, m


def _shared_kv(x, kv_norm_g, w_kvf, b_f, k_norm_g):
    NH, dim = k_norm_g.shape
    W = NH * dim
    wb = w_kvf.astype(BF16)
    k32, kb, logf = _norm_matmul(
        x, kv_norm_g, wb[:, :W], out_dtypes=[F32, BF16], head_gain=k_norm_g.reshape(1, W),
        n_head_cols=W, head_dim=dim, gate_w=wb[:, 2 * W:], gate_b=b_f, split_heads_first=True)
    v32, vb = _norm_matmul(x, kv_norm_g, wb[:, W:2 * W], out_dtypes=[F32, BF16], head_dim=dim,
                           split_heads_first=True)
    return k32, kb, v32, vb, logf


def _fox_in(x, norm_g, w_in, q_norm_g):
    NH, dim = q_norm_g.shape
    W = NH * dim
    gain = jnp.concatenate([q_norm_g.reshape(1, W).astype(F32) * (float(dim) ** -0.5 * LOG2E),
                            jnp.ones((1, W), F32)], axis=-1)
    (qz,) = _norm_matmul(x, norm_g, w_in.astype(BF16), out_dtypes=[BF16], head_gain=gain,
                         n_head_cols=W, head_dim=dim)
    return qz


def kernel(x_prompt, x_sample, cache_k, cache_v, cache_logf, state_C, state_n, state_m, state_conv,
           meta_tokens, a_norm_g, a_w_in, a_conv_w, a_conv_b, a_wq, a_wk, a_wv, a_w_gate, a_b_gate,
           a_out_g, a_skip, a_w_out, kv_norm_g, w_kvf, b_f, k_norm_g, b_norm_g, b_w_in, q_norm_g,
           b_w_out):
    B, SEQ, D = x_prompt.shape
    DB, DEC, _ = x_sample.shape
    NM = meta_tokens.shape[0]
    N_A = a_norm_g.shape[0]
    N_B = b_norm_g.shape[0]
    NH, dim = k_norm_g.shape
    W = NH * dim
    P = cache_k.shape[1]
    chunk = _tile(SEQ, 256)

    xp = jnp.concatenate(
        [jnp.broadcast_to(meta_tokens.astype(x_prompt.dtype)[None], (B, NM, D)), x_prompt], axis=1)
    xs = x_sample
    p_state, s_state = [], []
    p_C = s_C = None
    w_xc, w_xm = _fold_gate_weights(a_wq, a_wk, a_wv, a_w_gate)
    for i in range(N_A):
        w = (a_norm_g[i], a_w_in[i], a_conv_w[i], a_conv_b[i], a_wq[i], a_wk[i], a_wv[i],
             w_xc[i], w_xm[i], a_b_gate[i], a_out_g[i], a_skip[i], a_w_out[i])
        xp, hist, p_C, n, m = _mlstm_layer(xp, None, None, w, NM, chunk, p_C, i, N_A)
        p_state.append((hist, n, m))
        xs, hist, s_C, n, m = _mlstm_layer(xs, state_conv[i], (state_C, state_n, state_m),
                                           w, 0, DEC, s_C, i, N_A)
        s_state.append((hist, n, m))

    pk32, pkb, pv32, pvb, p_logf = _shared_kv(xp, kv_norm_g, w_kvf, b_f, k_norm_g)
    sk32, skb, sv32, svb, s_logf = _shared_kv(xs, kv_norm_g, w_kvf, b_f, k_norm_g)
    fp = _cumsum(p_logf, LOG2E)
    fs = _cumsum(jnp.concatenate([cache_logf.astype(F32), s_logf], axis=1), LOG2E)
    ck = cache_k.astype(BF16).reshape(DB, P, W)
    cv = cache_v.astype(BF16).reshape(DB, P, W)
    xp = xp[:, NM:]
    for j in range(N_B):
        qz = _fox_in(xp, b_norm_g[j], b_w_in[j], q_norm_g[j])
        xp = _matmul_residual(_fox_prompt(qz, pkb, pvb, fp, NM), b_w_out[j].astype(BF16), xp)
        qz = _fox_in(xs, b_norm_g[j], b_w_in[j], q_norm_g[j])
        xs = _matmul_residual(_fox_decode(qz, ck, cv, skb, svb, fs), b_w_out[j].astype(BF16), xs)

    def stack(states, idx):
        return jnp.stack([st[idx] for st in states])

    return (xp, xs, pk32, pv32, p_logf,
            p_C, stack(p_state, 1), stack(p_state, 2), stack(p_state, 0),
            sk32, sv32, s_logf,
            s_C, stack(s_state, 1), stack(s_state, 2), stack(s_state, 0))
```

```python
import functools

import jax
import jax.numpy as jnp
from jax import lax
from jax.experimental import pallas as pl
from jax.experimental.pallas import tpu as pltpu

F32 = jnp.float32
BF16 = jnp.bfloat16
EPS = 1e-6
CONV_PAD = 8
BF16_ROWS = 16
LANES = 128
MIN_FUSED_CONV_ROWS = 256
VMEM_LIMIT_BYTES = 56 * 1024 * 1024
HIGHEST = lax.Precision.HIGHEST
NEG_INF = float("-inf")
LOG2E = 1.4426950408889634


def _params(*sem):
    return pltpu.CompilerParams(dimension_semantics=sem, vmem_limit_bytes=VMEM_LIMIT_BYTES)


def _tile(n, cap, mult=BF16_ROWS):
    best = None
    for t in range(mult, min(n, cap) + 1, mult):
        if n % t == 0:
            best = t
    return best if best is not None else n


def _dot(a, b):
    return jnp.dot(a, b, preferred_element_type=F32)


def _dot_nt(a, b):
    return lax.dot_general(a, b, (((1,), (1,)), ((), ())), preferred_element_type=F32)


def _dot_tn(a, b):
    return lax.dot_general(a, b, (((0,), (0,)), ((), ())), preferred_element_type=F32)


def _silu(x):
    return x / (1.0 + jnp.exp(-x))


def _log_sigmoid(x):
    return jnp.minimum(x, 0.0) - jnp.log1p(jnp.exp(-jnp.abs(x)))


def _norm_matmul_kernel(*refs, n_out, n_head_blocks, head_dim, has_gain, has_gate):
    x_ref, g_ref, w_ref = refs[:3]
    pos = 3
    if has_gain:
        gain_ref = refs[pos]
        pos += 1
    if has_gate:
        gw_ref, gb_ref = refs[pos:pos + 2]
        pos += 2
    out_refs = refs[pos:pos + n_out]
    pos += n_out
    if has_gate:
        gate_out_ref = refs[pos]
        pos += 1
    xn_ref = refs[pos]
    j = pl.program_id(2)

    @pl.when(j == 0)
    def _():
        x = x_ref[0].astype(F32)
        ms = jnp.mean(x * x, axis=-1, keepdims=True)
        xn_ref[...] = (x * lax.rsqrt(ms + EPS) * g_ref[...]).astype(BF16)
        if has_gate:
            gate_out_ref[0] = _log_sigmoid(_dot(xn_ref[...], gw_ref[...]) + gb_ref[...])

    acc = _dot(xn_ref[...], w_ref[...])

    def store(val):
        for o in out_refs:
            if len(o.shape) == 4:
                o[0] = val.reshape(o.shape[1:]).astype(o.dtype)
            else:
                o[0] = val.astype(o.dtype)

    if has_gain:
        def normed():
            cols = []
            for c in range(acc.shape[1] // head_dim):
                blk = acc[:, c * head_dim:(c + 1) * head_dim]
                ms = jnp.mean(blk * blk, axis=-1, keepdims=True)
                cols.append(blk * lax.rsqrt(ms + EPS))
            return jnp.concatenate(cols, axis=-1) * gain_ref[...]

        @pl.when(j < n_head_blocks)
        def _():
            store(normed())

        @pl.when(j >= n_head_blocks)
        def _():
            store(acc)
    else:
        store(acc)


def _norm_matmul(x, g, w, *, out_dtypes, head_gain=None, n_head_cols=0, head_dim=128,
                 gate_w=None, gate_b=None, split_heads_first=False):
    lead_shape = x.shape[:2]
    x = x.reshape(1, -1, x.shape[-1])
    B, S, D = x.shape
    N = w.shape[1]
    tm = _tile(S, 1024)
    tn = _tile(N, 1024, 128)
    if head_gain is not None:
        tn = _tile(n_head_cols, tn, 128)
        assert N % tn == 0
    has_gain = head_gain is not None
    has_gate = gate_w is not None
    in_specs = [
        pl.BlockSpec((1, tm, D), lambda b, i, j: (b, i, 0)),
        pl.BlockSpec((1, D), lambda b, i, j: (0, 0)),
        pl.BlockSpec((D, tn), lambda b, i, j: (0, j)),
    ]
    args = [x, g.reshape(1, D).astype(F32), w]
    if has_gain:
        in_specs.append(pl.BlockSpec((1, tn), lambda b, i, j: (0, j)))
        args.append(head_gain.astype(F32))
    if has_gate:
        G = gate_w.shape[1]
        in_specs += [pl.BlockSpec((D, G), lambda b, i, j: (0, 0)),
                     pl.BlockSpec((1, G), lambda b, i, j: (0, 0))]
        args += [gate_w, gate_b.reshape(1, G).astype(F32)]
    out_shape = [jax.ShapeDtypeStruct((B, S, N), dt) for dt in out_dtypes]
    out_specs = [pl.BlockSpec((1, tm, tn), lambda b, i, j: (b, i, j)) for _ in out_dtypes]
    if split_heads_first:
        out_shape[0] = jax.ShapeDtypeStruct((B, S, N // head_dim, head_dim), out_dtypes[0])
        out_specs[0] = pl.BlockSpec((1, tm, tn // head_dim, head_dim), lambda b, i, j: (b, i, j, 0))
    if has_gate:
        out_shape.append(jax.ShapeDtypeStruct((B, S, G), F32))
        out_specs.append(pl.BlockSpec((1, tm, G), lambda b, i, j: (b, i, 0)))
    kern = functools.partial(
        _norm_matmul_kernel, n_out=len(out_dtypes), n_head_blocks=n_head_cols // tn,
        head_dim=head_dim, has_gain=has_gain, has_gate=has_gate)
    outs = pl.pallas_call(
        kern, grid=(B, S // tm, N // tn), in_specs=in_specs, out_specs=out_specs,
        out_shape=out_shape, scratch_shapes=[pltpu.VMEM((tm, D), BF16)],
        compiler_params=_params("parallel", "parallel", "arbitrary"), name="norm_matmul",
    )(*args)
    return [o.reshape(lead_shape + o.shape[2:]) for o in outs]


def _matmul_residual_kernel(a_ref, w_ref, x_ref, o_ref):
    o_ref[0] = x_ref[0] + _dot(a_ref[0], w_ref[...])


def _matmul_residual(a, w, x):
    out_shape = x.shape
    a = a.reshape(1, -1, a.shape[-1])
    x = x.reshape(1, -1, x.shape[-1])
    B, S, K = a.shape
    N = w.shape[1]
    tm = _tile(S, 1024)
    tn = _tile(N, 1024, 128)
    return pl.pallas_call(
        _matmul_residual_kernel, grid=(B, S // tm, N // tn),
        in_specs=[pl.BlockSpec((1, tm, K), lambda b, i, j: (b, i, 0)),
                  pl.BlockSpec((K, tn), lambda b, i, j: (0, j)),
                  pl.BlockSpec((1, tm, tn), lambda b, i, j: (b, i, j))],
        out_specs=pl.BlockSpec((1, tm, tn), lambda b, i, j: (b, i, j)),
        out_shape=jax.ShapeDtypeStruct((B, S, N), F32),
        compiler_params=_params("parallel", "parallel", "arbitrary"), name="matmul_residual",
    )(a, w, x).reshape(out_shape)


def _fold_gate_kernel(wq_ref, wk_ref, wv_ref, wg_ref, a_ref, b_ref, *, k_scale):
    def hdot(a, b):
        return jnp.dot(a, b, precision=HIGHEST, preferred_element_type=F32)

    a_ref[0, 0] = hdot(wq_ref[0, 0], wg_ref[0, 0, 0]) + hdot(wk_ref[0, 0], wg_ref[0, 1, 0]) * k_scale
    b_ref[0, 0] = hdot(wv_ref[0, 0], wg_ref[0, 2, 0])


def _fold_gate_weights(wq, wk, wv, w_gate):
    NA, H, DH, _ = wq.shape
    G = w_gate.shape[-1]
    w_spec = pl.BlockSpec((1, 1, DH, DH), lambda i, h: (i, h, 0, 0))
    o_spec = pl.BlockSpec((1, 1, DH, G), lambda i, h: (i, h, 0, 0))
    out = jax.ShapeDtypeStruct((NA, H, DH, G), F32)
    return pl.pallas_call(
        functools.partial(_fold_gate_kernel, k_scale=float(DH) ** -0.5), grid=(NA, H),
        in_specs=[w_spec, w_spec, w_spec,
                  pl.BlockSpec((1, 3, 1, DH, G), lambda i, h: (i, 0, h, 0, 0))],
        out_specs=[o_spec, o_spec], out_shape=[out, out],
        compiler_params=_params("parallel", "parallel"), name="fold_gate_weights",
    )(wq, wk, wv, w_gate.reshape(NA, 3, H, DH, G))


def _conv_gates_kernel(xm_ref, buf_ref, cw_ref, cb_ref, wa_ref, wb_ref, bg_ref,
                       xc_ref, g_ref, seq_ref, *, rows, cum_rows, conv_w, n_heads):
    h = pl.program_id(1)
    S = xm_ref.shape[1]
    seq_ref[0:CONV_PAD, :] = buf_ref[0]
    seq_ref[CONV_PAD:CONV_PAD + S, :] = xm_ref[0].astype(F32)
    for r in range(S // rows):
        base = r * rows
        acc = jnp.broadcast_to(cb_ref[0], (rows, cb_ref.shape[-1]))
        for i in range(conv_w):
            start = base + CONV_PAD - (conv_w - 1) + i
            acc = acc + seq_ref[start:start + rows, :] * cw_ref[0, i:i + 1, :]
        xc = _silu(acc).astype(BF16)
        xc_ref[0, base:base + rows, :] = xc
        part = _dot(xc, wa_ref[0]) + _dot(xm_ref[0, base:base + rows, :], wb_ref[0])

        @pl.when(h == 0)
        def _():
            g_ref[0, base:base + rows, :] = part + bg_ref[...]

        @pl.when(h > 0)
        def _():
            g_ref[0, base:base + rows, :] = g_ref[0, base:base + rows, :] + part

    @pl.when(h == n_heads - 1)
    def _():
        r = lax.broadcasted_iota(jnp.int32, (cum_rows, cum_rows), 0)
        c = lax.broadcasted_iota(jnp.int32, (cum_rows, cum_rows), 1)
        trilf = (c <= r).astype(F32)
        lane = lax.broadcasted_iota(jnp.int32, (cum_rows, g_ref.shape[-1]), 1)
        carry = jnp.zeros((1, g_ref.shape[-1]), F32)
        for i in range(S // cum_rows):
            g = g_ref[0, i * cum_rows:(i + 1) * cum_rows, :]
            lf = jnp.where(lane >= n_heads, _log_sigmoid(g), 0.0)
            cum = jnp.dot(trilf, lf, precision=HIGHEST, preferred_element_type=F32) + carry
            g_ref[0, i * cum_rows:(i + 1) * cum_rows, :] = jnp.where(lane >= n_heads, cum, g)
            carry = cum[cum_rows - 1:cum_rows, :]


def _conv_gates(xz, buf, conv_w, conv_b, w_xc, w_xm, b_gate):
    B, S, _ = xz.shape
    H, DH, G = w_xc.shape
    AI = H * DH
    CW = conv_w.shape[0]
    rows = _tile(S, 768)
    act_spec = pl.BlockSpec((1, S, DH), lambda b, h: (b, 0, h))
    w_spec = pl.BlockSpec((1, DH, G), lambda b, h: (h, 0, 0))
    kern = functools.partial(_conv_gates_kernel, rows=rows, cum_rows=_tile(S, 512, 8), conv_w=CW,
                             n_heads=H)
    return pl.pallas_call(
        kern, grid=(B, H),
        in_specs=[act_spec,
                  pl.BlockSpec((1, CONV_PAD, DH), lambda b, h: (b, 0, h)),
                  pl.BlockSpec((1, CW, DH), lambda b, h: (h, 0, 0)),
                  pl.BlockSpec((1, 1, DH), lambda b, h: (h, 0, 0)),
                  w_spec, w_spec,
                  pl.BlockSpec((1, G), lambda b, h: (0, 0))],
        out_specs=[act_spec, pl.BlockSpec((1, S, G), lambda b, h: (b, 0, 0))],
        out_shape=[jax.ShapeDtypeStruct((B, S, AI), BF16), jax.ShapeDtypeStruct((B, S, G), F32)],
        scratch_shapes=[pltpu.VMEM((S + CONV_PAD, DH), F32)],
        compiler_params=_params("parallel", "arbitrary"), name="conv_gates",
    )(xz, buf,
      conv_w.reshape(CW, H, DH).transpose(1, 0, 2).astype(F32),
      conv_b.reshape(H, 1, DH).astype(F32), w_xc, w_xm, b_gate.reshape(1, G).astype(F32))


def _in_conv_kernel(x_ref, g_ref, w_ref, buf_ref, cw_ref, cb_ref, wa_ref, wb_ref, bg_ref,
                    xm_ref, xc_ref, gate_ref, xn_ref, hist_ref, *, conv_w, n_heads):
    i = pl.program_id(1)
    j = pl.program_id(2)
    tm = x_ref.shape[1]

    @pl.when(j == 0)
    def _():
        x = x_ref[0].astype(F32)
        ms = jnp.mean(x * x, axis=-1, keepdims=True)
        xn_ref[...] = (x * lax.rsqrt(ms + EPS) * g_ref[...]).astype(BF16)

    acc = _dot(xn_ref[...], w_ref[...])
    xm = acc.astype(BF16)
    xm_ref[0] = xm
    prev = jnp.where(i == 0, buf_ref[0], hist_ref[j])
    hist_ref[j] = acc[tm - CONV_PAD:tm, :]
    row = lax.broadcasted_iota(jnp.int32, prev.shape, 0)
    y = cb_ref[...] + acc * cw_ref[conv_w - 1:conv_w, :]
    for shift in range(1, conv_w):
        rolled = pltpu.roll(acc, shift, 0)
        head = jnp.where(row < shift, pltpu.roll(prev, shift, 0), rolled[0:CONV_PAD, :])
        shifted = jnp.concatenate([head, rolled[CONV_PAD:, :]], axis=0)
        y = y + shifted * cw_ref[conv_w - 1 - shift:conv_w - shift, :]
    xc = _silu(y).astype(BF16)
    xc_ref[0] = xc
    part = _dot(xc, wa_ref[...]) + _dot(xm, wb_ref[...])

    @pl.when(j == 0)
    def _():
        gate_ref[0] = part + bg_ref[...]

    @pl.when(j > 0)
    def _():
        gate_ref[0] = gate_ref[0] + part

    @pl.when(j == pl.num_programs(2) - 1)
    def _():
        g = gate_ref[0]
        lane = lax.broadcasted_iota(jnp.int32, g.shape, 1)
        gate_ref[0] = jnp.where(lane >= n_heads, _log_sigmoid(g), g)


def _in_conv(x, norm_g, w_xm_in, buf, conv_w, conv_b, w_xc, w_xm, b_gate):
    B, S, D = x.shape
    H, DH, G = w_xc.shape
    AI = H * DH
    CW = conv_w.shape[0]
    tm = _tile(S, 1024)
    tn = _tile(AI, 1024, LANES)
    act_spec = pl.BlockSpec((1, tm, tn), lambda b, i, j: (b, i, j))
    act = jax.ShapeDtypeStruct((B, S, AI), BF16)
    kern = functools.partial(_in_conv_kernel, conv_w=CW, n_heads=H)
    return pl.pallas_call(
        kern, grid=(B, S // tm, AI // tn),
        in_specs=[pl.BlockSpec((1, tm, D), lambda b, i, j: (b, i, 0)),
                  pl.BlockSpec((1, D), lambda b, i, j: (0, 0)),
                  pl.BlockSpec((D, tn), lambda b, i, j: (0, j)),
                  pl.BlockSpec((1, CONV_PAD, tn), lambda b, i, j: (b, 0, j)),
                  pl.BlockSpec((CW, tn), lambda b, i, j: (0, j)),
                  pl.BlockSpec((1, tn), lambda b, i, j: (0, j)),
                  pl.BlockSpec((tn, G), lambda b, i, j: (j, 0)),
                  pl.BlockSpec((tn, G), lambda b, i, j: (j, 0)),
                  pl.BlockSpec((1, G), lambda b, i, j: (0, 0))],
        out_specs=[act_spec, act_spec, pl.BlockSpec((1, tm, G), lambda b, i, j: (b, i, 0))],
        out_shape=[act, act, jax.ShapeDtypeStruct((B, S, G), F32)],
        scratch_shapes=[pltpu.VMEM((tm, D), BF16), pltpu.VMEM((AI // tn, CONV_PAD, tn), F32)],
        compiler_params=_params("parallel", "arbitrary", "arbitrary"), name="in_conv",
    )(x, norm_g.reshape(1, D).astype(F32), w_xm_in, buf, conv_w.astype(F32),
      conv_b.reshape(1, AI).astype(F32), w_xc.reshape(AI, G), w_xm.reshape(AI, G),
      b_gate.reshape(1, G).astype(F32))


def _mlstm_chunk(xm, xc, z, wq, wk, wv, g_rows, g_cols, head, n_heads, og, skip,
                 C_ref, n_ref, m_ref, b_ref):
    L = xm.shape[0]
    q = _dot(xc, wq).astype(BF16)
    k = _dot(xc, wk).astype(BF16)
    v = _dot(xm, wv).astype(BF16)
    row = lax.broadcasted_iota(jnp.int32, (L, L), 0)
    col = lax.broadcasted_iota(jnp.int32, (L, L), 1)
    tril = col <= row
    lane = lax.broadcasted_iota(jnp.int32, g_cols.shape, 1)
    ig_col = jnp.sum(jnp.where(lane == head, g_cols, 0.0), axis=-1, keepdims=True)
    f_col = jnp.sum(jnp.where(lane == head + n_heads, g_cols, 0.0), axis=-1, keepdims=True)
    ig_row = g_rows[0:1, :]
    f_row = g_rows[1:2, :]
    m_prev = m_ref[...]
    f_prev = b_ref[...]
    log_d = jnp.where(tril, f_col - f_row + ig_row, NEG_INF)
    inter = f_col - f_prev + m_prev
    m_t = jnp.maximum(jnp.max(log_d, axis=-1, keepdims=True), inter)
    d = jnp.exp(log_d - m_t)
    a = jnp.exp(inter - m_t)
    s = _dot_nt(q, k) * d
    C = C_ref[...]
    n = n_ref[...]
    num = _dot(s.astype(BF16), v) + a * _dot(q, C.astype(BF16))
    qn = _dot_nt(q, jnp.broadcast_to(n, (8, n.shape[1])).astype(BF16))[:, 0:1]
    den = jnp.sum(s, axis=-1, keepdims=True) + a * qn
    hid = num * (1.0 / jnp.maximum(jnp.abs(den), jnp.exp(-m_t)))
    m_new = m_t[L - 1:L, :]
    f_last = f_row[:, L - 1:L]
    decay = jnp.exp(f_last - f_prev + m_prev - m_new)
    w_col = jnp.exp(f_last - f_col + ig_col - m_new)
    C_ref[...] = decay * C + _dot_tn(k, (w_col * v.astype(F32)).astype(BF16))
    n_ref[...] = decay * n + jnp.sum(w_col * k.astype(F32), axis=0, keepdims=True)
    m_ref[...] = m_new
    b_ref[...] = f_last
    hc = hid - jnp.mean(hid, axis=-1, keepdims=True)
    hn = hc * lax.rsqrt(jnp.mean(hc * hc, axis=-1, keepdims=True) + EPS) * og
    return ((hn + skip * xc.astype(F32)) * _silu(z.astype(F32))).astype(BF16)


def _mlstm_kernel(*refs, lead, chunk, n_chunks, n_heads, has_state):
    xm_ref, xc_ref, z_ref, wq_ref, wk_ref, wv_ref = refs[:6]
    pos = 6
    if lead:
        gl_ref = refs[pos]
        pos += 1
    gm_ref, gc_ref, og_ref, skip_ref = refs[pos:pos + 4]
    pos += 4
    if has_state:
        c0_ref, n0_ref, m0_ref = refs[pos:pos + 3]
        pos += 3
    pos += 1
    o_ref, C_ref, n_ref, m_ref, b_ref = refs[pos:pos + 5]
    C_st, n_st, m_st = C_ref.at[0, 0, 0], n_ref.at[0, 0], m_ref.at[0, 0]
    if has_state:
        C_st[...] = c0_ref[0, 0, 0]
        n_st[...] = n0_ref[0, 0, 0]
        m_st[...] = m0_ref[0, 0, 0]
    else:
        C_st[...] = jnp.zeros(C_st.shape, F32)
        n_st[...] = jnp.zeros(n_st.shape, F32)
        m_st[...] = jnp.zeros(m_st.shape, F32)
    b_ref[...] = jnp.zeros(b_ref.shape, F32)
    head = pl.program_id(1)
    og = og_ref[0]
    skip = skip_ref[0]

    def run(rows, g_rows):
        o_ref[0, rows, :] = _mlstm_chunk(
            xm_ref[0, rows, :], xc_ref[0, rows, :], z_ref[0, rows, :], wq_ref[0], wk_ref[0],
            wv_ref[0], g_rows, gc_ref[0, rows, :], head, n_heads, og, skip,
            C_st, n_st, m_st, b_ref)

    if lead:
        run(pl.ds(0, lead), gl_ref[0, 0])

    def body(c, carry):
        run(pl.ds(pl.multiple_of(lead + c * chunk, BF16_ROWS), chunk), gm_ref[0, 0, c])
        return carry

    lax.fori_loop(0, n_chunks, body, 0, unroll=_tile(n_chunks, 2, 1))


def _mlstm(xm, z, z_head0, xc, gates, wq, wk, wv, out_g, skip, state, lead, chunk, c_stack, layer,
           n_layers):
    B, S, AI = xc.shape
    H, DH = out_g.shape
    n_chunks = (S - lead) // chunk
    assert lead + n_chunks * chunk == S
    g = gates.reshape(B, S, 2, H).transpose(0, 3, 2, 1)
    g_main = g[..., lead:].reshape(B, H, 2, n_chunks, chunk).transpose(0, 1, 3, 2, 4)
    act_spec = pl.BlockSpec((1, S, DH), lambda b, h: (b, 0, h))
    w_spec = pl.BlockSpec((1, DH, DH), lambda b, h: (h, 0, 0))
    in_specs = [act_spec, act_spec, pl.BlockSpec((1, S, DH), lambda b, h: (b, 0, z_head0 + h)),
                w_spec, w_spec, w_spec]
    args = [xm, xc, z, wq, wk, wv]
    if lead:
        in_specs.append(pl.BlockSpec((1, 1, 2, lead), lambda b, h: (b, h, 0, 0)))
        args.append(g[..., :lead])
    in_specs += [pl.BlockSpec((1, 1, n_chunks, 2, chunk), lambda b, h: (b, h, 0, 0, 0)),
                 pl.BlockSpec((1, S, 2 * H), lambda b, h: (b, 0, 0)),
                 pl.BlockSpec((1, 1, DH), lambda b, h: (h, 0, 0)),
                 pl.BlockSpec((1, 1, DH), lambda b, h: (h, 0, 0))]
    args += [g_main, gates, out_g.reshape(H, 1, DH).astype(F32), skip.reshape(H, 1, DH).astype(F32)]
    c_spec = pl.BlockSpec((1, 1, DH, DH), lambda b, h: (b, h, 0, 0))
    n_spec = pl.BlockSpec((1, 1, 1, DH), lambda b, h: (b, h, 0, 0))
    m_spec = pl.BlockSpec((1, 1, 1, 1), lambda b, h: (b, h, 0, 0))
    has_state = state is not None
    if has_state:
        C0, n0, m0 = state
        NL = C0.shape[0]
        in_specs += [pl.BlockSpec((1, 1, 1, DH, DH), lambda b, h: (layer, b, h, 0, 0)),
                     pl.BlockSpec((1, 1, 1, 1, DH), lambda b, h: (layer, b, h, 0, 0)),
                     pl.BlockSpec((1, 1, 1, 1, 1), lambda b, h: (layer, b, h, 0, 0))]
        args += [C0.astype(F32), n0.reshape(NL, B, H, 1, DH).astype(F32),
                 m0.reshape(NL, B, H, 1, 1).astype(F32)]
    if c_stack is None:
        c_stack = jnp.zeros((n_layers, B, H, DH, DH), F32)
    aliases = {len(args): 1}
    in_specs.append(pl.BlockSpec(memory_space=pl.ANY))
    args.append(c_stack)
    kern = functools.partial(_mlstm_kernel, lead=lead, chunk=chunk, n_chunks=n_chunks, n_heads=H,
                             has_state=has_state)
    hz, C, n, m = pl.pallas_call(
        kern, grid=(B, H), in_specs=in_specs,
        out_specs=[act_spec,
                   pl.BlockSpec((1, 1, 1, DH, DH), lambda b, h: (layer, b, h, 0, 0)),
                   n_spec, m_spec],
        out_shape=[jax.ShapeDtypeStruct((B, S, AI), BF16),
                   jax.ShapeDtypeStruct((n_layers, B, H, DH, DH), F32),
                   jax.ShapeDtypeStruct((B, H, 1, DH), F32),
                   jax.ShapeDtypeStruct((B, H, 1, 1), F32)],
        scratch_shapes=[pltpu.VMEM((1, 1), F32)],
        input_output_aliases=aliases,
        compiler_params=_params("parallel", "parallel"), name="mlstm",
    )(*args)
    return hz, C, n.reshape(B, H, DH), m.reshape(B, H)


def _cumsum_kernel(x_ref, o_ref, *, rows, scale, first_lane):
    S, G = x_ref.shape[1:]
    r = lax.broadcasted_iota(jnp.int32, (rows, rows), 0)
    c = lax.broadcasted_iota(jnp.int32, (rows, rows), 1)
    trilf = (c <= r).astype(F32)
    summed = lax.broadcasted_iota(jnp.int32, (rows, G), 1) >= first_lane
    carry = jnp.zeros((1, G), F32)
    for i in range(S // rows):
        x = x_ref[0, i * rows:(i + 1) * rows, :]
        blk = jnp.dot(trilf, jnp.where(summed, x, 0.0), precision=HIGHEST,
                      preferred_element_type=F32) + carry
        o_ref[0, i * rows:(i + 1) * rows, :] = jnp.where(summed, blk * scale, x)
        carry = blk[rows - 1:rows, :]


def _cumsum(x, scale=1.0, first_lane=0):
    B, S, G = x.shape
    rows = _tile(S, 512, 8)
    spec = pl.BlockSpec((1, S, G), lambda b: (b, 0, 0))
    return pl.pallas_call(
        functools.partial(_cumsum_kernel, rows=rows, scale=scale, first_lane=first_lane),
        grid=(B,), in_specs=[spec], out_specs=spec, out_shape=jax.ShapeDtypeStruct((B, S, G), F32),
        compiler_params=_params("parallel"), name="cumsum",
    )(x)


def _lane_fold(op, acc, x):
    for c in range(x.shape[1] // LANES):
        acc = op(acc, x[:, c * LANES:(c + 1) * LANES])
    return acc


def _fox_prompt_kernel(q_ref, z_ref, k_ref, v_ref, fq_ref, fkl_ref, fkm_ref, o_ref,
                       s_ref, sl_ref, fq_rep, m_ref, l_ref, acc_ref, *, lead, blk, n_blk, heads,
                       dim):
    n_lane = blk // LANES
    cols = [slice(hh * dim, (hh + 1) * dim) for hh in range(heads)]

    def key_rows(j):
        return pl.ds(pl.multiple_of(lead + j * blk, BF16_ROWS), blk)

    def tiled(x):
        return jnp.concatenate([x] * n_lane, axis=1)

    def q_block(i, carry):
        qrows = pl.ds(pl.multiple_of(i * blk, blk), blk)

        def logits(hh, j):
            return (_dot_nt(q_ref[0, qrows, cols[hh]], k_ref[0, key_rows(j), cols[hh]])
                    + tiled(fq_rep[hh]) - fkm_ref[0, 0, j, hh:hh + 1, :])

        for hh in range(heads):
            fq_rep[hh] = jnp.broadcast_to(fq_ref[0, 0, qrows, hh:hh + 1], (blk, LANES))
        if lead:
            for hh in range(heads):
                s = (_dot_nt(q_ref[0, qrows, cols[hh]], k_ref[0, 0:LANES, cols[hh]])
                     + fq_rep[hh] - fkl_ref[0, 0, hh:hh + 1, :])
                sl_ref[hh] = s
                m_ref[hh] = s
        else:
            m_ref[...] = jnp.full(m_ref.shape, NEG_INF, F32)

        def pass1(j, c):
            for hh in range(heads):
                s = logits(hh, j)
                s_ref[hh, j] = s
                m_ref[hh] = _lane_fold(jnp.maximum, m_ref[hh], s)
            return c

        lax.fori_loop(0, i, pass1, 0)
        row = lax.broadcasted_iota(jnp.int32, (blk, blk), 0)
        col = lax.broadcasted_iota(jnp.int32, (blk, blk), 1)
        for hh in range(heads):
            s = jnp.where(col <= row, logits(hh, i), NEG_INF)
            s_ref[hh, i] = s
            m = jnp.max(_lane_fold(jnp.maximum, m_ref[hh], s), axis=-1, keepdims=True)
            m_ref[hh] = jnp.broadcast_to(m, (blk, LANES))
        if lead:
            for hh in range(heads):
                p = jnp.exp2(sl_ref[hh] - m_ref[hh])
                l_ref[hh] = p
                acc_ref[hh] = _dot(p.astype(BF16), v_ref[0, 0:LANES, cols[hh]])
        else:
            l_ref[...] = jnp.zeros(l_ref.shape, F32)
            acc_ref[...] = jnp.zeros(acc_ref.shape, F32)

        def pass2(j, c):
            for hh in range(heads):
                p = jnp.exp2(s_ref[hh, j] - tiled(m_ref[hh]))
                l_ref[hh] = _lane_fold(jnp.add, l_ref[hh], p)
                acc_ref[hh] += _dot(p.astype(BF16), v_ref[0, key_rows(j), cols[hh]])
            return c

        lax.fori_loop(0, i + 1, pass2, 0)
        for hh in range(heads):
            l = jnp.sum(l_ref[hh], axis=-1, keepdims=True)
            o_ref[0, qrows, cols[hh]] = (acc_ref[hh] * (1.0 / l)
                                         * _silu(z_ref[0, qrows, cols[hh]].astype(F32))).astype(BF16)
        return carry

    lax.fori_loop(0, n_blk, q_block, 0)


def _fox_prompt(qz, kb, vb, f_cum, lead, heads_per_step=4, blk=512):
    B, SQ, W2 = qz.shape
    W = W2 // 2
    NH = f_cum.shape[-1]
    dim = W // NH
    hg = min(heads_per_step, NH)
    G = NH // hg
    blk = _tile(SQ, blk, LANES)
    n_blk = SQ // blk
    SK = lead + SQ
    assert lead <= LANES <= SK
    fq = f_cum[:, lead:].reshape(B, SQ, G, hg).transpose(0, 2, 1, 3)
    fk = f_cum.transpose(0, 2, 1).reshape(B, G, hg, SK)
    fk_lead = jnp.pad(fk[..., :lead], ((0, 0), (0, 0), (0, 0), (0, LANES - lead)),
                      constant_values=float("inf"))
    fk_main = fk[..., lead:].reshape(B, G, hg, n_blk, blk).transpose(0, 1, 3, 2, 4)
    wcols = hg * dim
    kern = functools.partial(_fox_prompt_kernel, lead=lead, blk=blk, n_blk=n_blk, heads=hg, dim=dim)
    return pl.pallas_call(
        kern, grid=(B, G),
        in_specs=[pl.BlockSpec((1, SQ, wcols), lambda b, g: (b, 0, g)),
                  pl.BlockSpec((1, SQ, wcols), lambda b, g: (b, 0, G + g)),
                  pl.BlockSpec((1, SK, wcols), lambda b, g: (b, 0, g)),
                  pl.BlockSpec((1, SK, wcols), lambda b, g: (b, 0, g)),
                  pl.BlockSpec((1, 1, SQ, hg), lambda b, g: (b, g, 0, 0)),
                  pl.BlockSpec((1, 1, hg, LANES), lambda b, g: (b, g, 0, 0)),
                  pl.BlockSpec((1, 1, n_blk, hg, blk), lambda b, g: (b, g, 0, 0, 0))],
        out_specs=pl.BlockSpec((1, SQ, wcols), lambda b, g: (b, 0, g)),
        out_shape=jax.ShapeDtypeStruct((B, SQ, W), BF16),
        scratch_shapes=[pltpu.VMEM((hg, n_blk, blk, blk), F32)]
        + [pltpu.VMEM((hg, blk, LANES), F32)] * 4 + [pltpu.VMEM((hg, blk, dim), F32)],
        compiler_params=_params("parallel", "parallel"), name="fox_prompt",
    )(qz, qz, kb, vb, fq, fk_lead, fk_main)


def _pack_heads_kernel(x_ref, o_ref, *, heads, dim):
    rows = o_ref.shape[1]
    for h in range(heads):
        o_ref[0, :, h * dim:(h + 1) * dim] = x_ref[0, pl.ds(h, rows, stride=heads), :].astype(o_ref.dtype)


def _pack_heads(x, dtype):
    B, P, NH, dim = x.shape
    rows = _tile(P, 512)
    return pl.pallas_call(
        functools.partial(_pack_heads_kernel, heads=NH, dim=dim), grid=(B, P // rows),
        in_specs=[pl.BlockSpec((1, rows * NH, dim), lambda b, i: (b, i, 0))],
        out_specs=pl.BlockSpec((1, rows, NH * dim), lambda b, i: (b, i, 0)),
        out_shape=jax.ShapeDtypeStruct((B, P, NH * dim), dtype),
        compiler_params=_params("parallel", "parallel"), name="pack_heads",
    )(x.reshape(B, P * NH, dim))


def _fox_decode_kernel(q_ref, z_ref, ck_ref, cv_ref, k_ref, v_ref, fq_ref, fkc_ref, fkn_ref, o_ref,
                       *, heads, dim):
    Q = q_ref.shape[1]
    row = lax.broadcasted_iota(jnp.int32, (Q, Q), 0)
    col = lax.broadcasted_iota(jnp.int32, (Q, Q), 1)
    outs = []
    for hh in range(heads):
        cs = slice(hh * dim, (hh + 1) * dim)
        q = q_ref[0, :, cs]
        fq = fq_ref[0, 0, :, hh:hh + 1]
        s_c = _dot_nt(q, ck_ref[0, :, cs].astype(BF16)) + fq - fkc_ref[0, 0, hh:hh + 1, :]
        s_n = jnp.where(col <= row, _dot_nt(q, k_ref[0, :, cs]) + fq - fkn_ref[0, 0, hh:hh + 1, :],
                        NEG_INF)
        m = jnp.maximum(jnp.max(s_c, axis=-1, keepdims=True), jnp.max(s_n, axis=-1, keepdims=True))
        p_c = jnp.exp2(s_c - m)
        p_n = jnp.exp2(s_n - m)
        l = jnp.sum(p_c, axis=-1, keepdims=True) + jnp.sum(p_n, axis=-1, keepdims=True)
        acc = _dot(p_c.astype(BF16), cv_ref[0, :, cs].astype(BF16)) + _dot(p_n.astype(BF16), v_ref[0, :, cs])
        outs.append((acc * (1.0 / l) * _silu(z_ref[0, :, cs].astype(F32))).astype(BF16))
    o_ref[0] = jnp.concatenate(outs, axis=-1)


def _fox_decode(qz, cache_k, cache_v, kb, vb, f_cum, heads_per_step=4):
    B, Q, W2 = qz.shape
    W = W2 // 2
    P = cache_k.shape[1]
    NH = f_cum.shape[-1]
    dim = W // NH
    hg = min(heads_per_step, NH)
    G = NH // hg
    fq = f_cum[:, P:].reshape(B, Q, G, hg).transpose(0, 2, 1, 3)
    fk = f_cum.transpose(0, 2, 1).reshape(B, G, hg, P + Q)
    wcols = hg * dim
    kern = functools.partial(_fox_decode_kernel, heads=hg, dim=dim)
    return pl.pallas_call(
        kern, grid=(B, G),
        in_specs=[pl.BlockSpec((1, Q, wcols), lambda b, g: (b, 0, g)),
                  pl.BlockSpec((1, Q, wcols), lambda b, g: (b, 0, G + g)),
                  pl.BlockSpec((1, P, wcols), lambda b, g: (b, 0, g)),
                  pl.BlockSpec((1, P, wcols), lambda b, g: (b, 0, g)),
                  pl.BlockSpec((1, Q, wcols), lambda b, g: (b, 0, g)),
                  pl.BlockSpec((1, Q, wcols), lambda b, g: (b, 0, g)),
                  pl.BlockSpec((1, 1, Q, hg), lambda b, g: (b, g, 0, 0)),
                  pl.BlockSpec((1, 1, hg, P), lambda b, g: (b, g, 0, 0)),
                  pl.BlockSpec((1, 1, hg, Q), lambda b, g: (b, g, 0, 0))],
        out_specs=pl.BlockSpec((1, Q, wcols), lambda b, g: (b, 0, g)),
        out_shape=jax.ShapeDtypeStruct((B, Q, W), BF16),
        compiler_params=_params("parallel", "parallel"), name="fox_decode",
    )(qz, qz, cache_k, cache_v, kb, vb, fq, fk[..., :P], fk[..., P:])


def _mlstm_layer(x, conv_hist, state, w, lead, chunk, c_stack, layer, n_layers):
    (norm_g, w_in, conv_w, conv_b, wq, wk, wv, w_xc, w_xm, b_gate, out_g, skip, w_out) = w
    B, S, _ = x.shape
    H, DH, _ = wq.shape
    AI = H * DH
    CW = conv_w.shape[0]
    assert S >= CW - 1 and CW - 1 <= CONV_PAD
    buf = jnp.zeros((B, CONV_PAD, AI), F32)
    if conv_hist is not None:
        buf = buf.at[:, CONV_PAD - (CW - 1):].set(conv_hist.astype(F32))
    if S >= MIN_FUSED_CONV_ROWS:
        wb = w_in.astype(BF16)
        xm, xc, gates = _in_conv(x, norm_g, wb[:, :AI], buf, conv_w, conv_b,
                                 w_xc.astype(BF16), w_xm.astype(BF16), b_gate)
        gates = _cumsum(gates, first_lane=H)
        (z,) = _norm_matmul(x, norm_g, wb[:, AI:], out_dtypes=[BF16])
        z_head0 = 0
    else:
        (xm,) = _norm_matmul(x, norm_g, w_in.astype(BF16), out_dtypes=[BF16])
        xc, gates = _conv_gates(xm, buf, conv_w, conv_b, w_xc.astype(BF16), w_xm.astype(BF16),
                                b_gate)
        z, z_head0 = xm, H
    hz, c_stack, n, m = _mlstm(
        xm, z, z_head0, xc, gates, wq.astype(BF16), (wk * float(DH) ** -0.5).astype(BF16),
        wv.astype(BF16), out_g, skip.reshape(H, DH), state, lead, chunk, c_stack, layer, n_layers)
    x_new = _matmul_residual(hz, w_out.astype(BF16), x)
    new_hist = xm[:, S - (CW - 1):, :AI].astype(F32)
    return x_new, new_hist, c_stack, n, m


def _shared_kv(x, kv_norm_g, w_kvf, b_f, k_norm_g):
    NH, dim = k_norm_g.shape
    W = NH * dim
    wb = w_kvf.astype(BF16)
    k32, kb, logf = _norm_matmul(
        x, kv_norm_g, wb[:, :W], out_dtypes=[F32, BF16], head_gain=k_norm_g.reshape(1, W),
        n_head_cols=W, head_dim=dim, gate_w=wb[:, 2 * W:], gate_b=b_f, split_heads_first=True)
    v32, vb = _norm_matmul(x, kv_norm_g, wb[:, W:2 * W], out_dtypes=[F32, BF16], head_dim=dim,
                           split_heads_first=True)
    return k32, kb, v32, vb, logf


def _fox_in(x, norm_g, w_in, q_norm_g):
    NH, dim = q_norm_g.shape
    W = NH * dim
    gain = jnp.concatenate([q_norm_g.reshape(1, W).astype(F32) * (float(dim) ** -0.5 * LOG2E),
                            jnp.ones((1, W), F32)], axis=-1)
    (qz,) = _norm_matmul(x, norm_g, w_in.astype(BF16), out_dtypes=[BF16], head_gain=gain,
                         n_head_cols=W, head_dim=dim)
    return qz


def kernel(x_prompt, x_sample, cache_k, cache_v, cache_logf, state_C, state_n, state_m, state_conv,
           meta_tokens, a_norm_g, a_w_in, a_conv_w, a_conv_b, a_wq, a_wk, a_wv, a_w_gate, a_b_gate,
           a_out_g, a_skip, a_w_out, kv_norm_g, w_kvf, b_f, k_norm_g, b_norm_g, b_w_in, q_norm_g,
           b_w_out):
    B, SEQ, D = x_prompt.shape
    DB, DEC, _ = x_sample.shape
    NM = meta_tokens.shape[0]
    N_A = a_norm_g.shape[0]
    N_B = b_norm_g.shape[0]
    NH, dim = k_norm_g.shape
    W = NH * dim
    P = cache_k.shape[1]
    chunk = _tile(SEQ, 256)

    xp = jnp.concatenate(
        [jnp.broadcast_to(meta_tokens.astype(x_prompt.dtype)[None], (B, NM, D)), x_prompt], axis=1)
    xs = x_sample
    p_state, s_state = [], []
    p_C = s_C = None
    w_xc, w_xm = _fold_gate_weights(a_wq, a_wk, a_wv, a_w_gate)
    for i in range(N_A):
        w = (a_norm_g[i], a_w_in[i], a_conv_w[i], a_conv_b[i], a_wq[i], a_wk[i], a_wv[i],
             w_xc[i], w_xm[i], a_b_gate[i], a_out_g[i], a_skip[i], a_w_out[i])
        xp, hist, p_C, n, m = _mlstm_layer(xp, None, None, w, NM, chunk, p_C, i, N_A)
        p_state.append((hist, n, m))
        xs, hist, s_C, n, m = _mlstm_layer(xs, state_conv[i], (state_C, state_n, state_m),
                                           w, 0, DEC, s_C, i, N_A)
        s_state.append((hist, n, m))

    pk32, pkb, pv32, pvb, p_logf = _shared_kv(xp, kv_norm_g, w_kvf, b_f, k_norm_g)
    sk32, skb, sv32, svb, s_logf = _shared_kv(xs, kv_norm_g, w_kvf, b_f, k_norm_g)
    fp = _cumsum(p_logf, LOG2E)
    fs = _cumsum(jnp.concatenate([cache_logf.astype(F32), s_logf], axis=1), LOG2E)
    ck = _pack_heads(cache_k, BF16)
    cv = _pack_heads(cache_v, BF16)
    xp = xp[:, NM:]
    for j in range(N_B):
        qz = _fox_in(xp, b_norm_g[j], b_w_in[j], q_norm_g[j])
        xp = _matmul_residual(_fox_prompt(qz, pkb, pvb, fp, NM), b_w_out[j].astype(BF16), xp)
        qz = _fox_in(xs, b_norm_g[j], b_w_in[j], q_norm_g[j])
        xs = _matmul_residual(_fox_decode(qz, ck, cv, skb, svb, fs), b_w_out[j].astype(BF16), xs)

    def stack(states, idx):
        return jnp.stack([st[idx] for st in states])

    return (xp, xs, pk32, pv32, p_logf,
            p_C, stack(p_state, 1), stack(p_state, 2), stack(p_state, 0),
            sk32, sv32, s_logf,
            s_C, stack(s_state, 1), stack(s_state, 2), stack(s_state, 0))
```

```python
import functools

import jax
import jax.numpy as jnp
from jax import lax
from jax.experimental import pallas as pl
from jax.experimental.pallas import tpu as pltpu

F32 = jnp.float32
BF16 = jnp.bfloat16
EPS = 1e-6
CONV_PAD = 8
BF16_ROWS = 16
LANES = 128
MIN_FUSED_CONV_ROWS = 256
VMEM_LIMIT_BYTES = 56 * 1024 * 1024
HIGHEST = lax.Precision.HIGHEST
NEG_INF = float("-inf")
LOG2E = 1.4426950408889634


def _params(*sem):
    return pltpu.CompilerParams(dimension_semantics=sem, vmem_limit_bytes=VMEM_LIMIT_BYTES)


def _tile(n, cap, mult=BF16_ROWS):
    best = None
    for t in range(mult, min(n, cap) + 1, mult):
        if n % t == 0:
            best = t
    return best if best is not None else n


def _dot(a, b):
    return jnp.dot(a, b, preferred_element_type=F32)


def _dot_nt(a, b):
    return lax.dot_general(a, b, (((1,), (1,)), ((), ())), preferred_element_type=F32)


def _dot_tn(a, b):
    return lax.dot_general(a, b, (((0,), (0,)), ((), ())), preferred_element_type=F32)


def _silu(x):
    return x / (1.0 + jnp.exp(-x))


def _log_sigmoid(x):
    return jnp.minimum(x, 0.0) - jnp.log1p(jnp.exp(-jnp.abs(x)))


def _norm_matmul_kernel(*refs, n_out, n_head_blocks, all_head_blocks, head_dim, has_gain,
                        has_gate):
    x_ref, g_ref, w_ref = refs[:3]
    pos = 3
    if has_gain:
        gain_ref = refs[pos]
        pos += 1
    if has_gate:
        gw_ref, gb_ref = refs[pos:pos + 2]
        pos += 2
    out_refs = refs[pos:pos + n_out]
    pos += n_out
    if has_gate:
        gate_out_ref = refs[pos]
        pos += 1
    xn_ref = refs[pos]
    j = pl.program_id(2)

    @pl.when(j == 0)
    def _():
        x = x_ref[0].astype(F32)
        ms = jnp.mean(x * x, axis=-1, keepdims=True)
        xn_ref[...] = (x * lax.rsqrt(ms + EPS) * g_ref[...]).astype(BF16)
        if has_gate:
            gate_out_ref[0] = _log_sigmoid(_dot(xn_ref[...], gw_ref[...]) + gb_ref[...])

    acc = _dot(xn_ref[...], w_ref[...])

    def store(val):
        for o in out_refs:
            if len(o.shape) == 4:
                o[0] = val.reshape(o.shape[1:]).astype(o.dtype)
            else:
                o[0] = val.astype(o.dtype)

    if has_gain:
        def normed():
            cols = []
            for c in range(acc.shape[1] // head_dim):
                blk = acc[:, c * head_dim:(c + 1) * head_dim]
                ms = jnp.mean(blk * blk, axis=-1, keepdims=True)
                cols.append(blk * lax.rsqrt(ms + EPS))
            return jnp.concatenate(cols, axis=-1) * gain_ref[...]

        if all_head_blocks:
            store(normed())
        else:
            @pl.when(j < n_head_blocks)
            def _():
                store(normed())

            @pl.when(j >= n_head_blocks)
            def _():
                store(acc)
    else:
        store(acc)


def _norm_matmul(x, g, w, *, out_dtypes, head_gain=None, n_head_cols=0, head_dim=128,
                 gate_w=None, gate_b=None, split_heads_first=False):
    lead_shape = x.shape[:2]
    x = x.reshape(1, -1, x.shape[-1])
    B, S, D = x.shape
    N = w.shape[1]
    tm = _tile(S, 1024)
    tn = _tile(N, 1024, 128)
    if head_gain is not None:
        tn = _tile(n_head_cols, tn, 128)
        assert N % tn == 0
    has_gain = head_gain is not None
    has_gate = gate_w is not None
    in_specs = [
        pl.BlockSpec((1, tm, D), lambda b, i, j: (b, i, 0)),
        pl.BlockSpec((1, D), lambda b, i, j: (0, 0)),
        pl.BlockSpec((D, tn), lambda b, i, j: (0, j)),
    ]
    args = [x, g.reshape(1, D).astype(F32), w]
    if has_gain:
        in_specs.append(pl.BlockSpec((1, tn), lambda b, i, j: (0, j)))
        args.append(head_gain.astype(F32))
    if has_gate:
        G = gate_w.shape[1]
        in_specs += [pl.BlockSpec((D, G), lambda b, i, j: (0, 0)),
                     pl.BlockSpec((1, G), lambda b, i, j: (0, 0))]
        args += [gate_w, gate_b.reshape(1, G).astype(F32)]
    out_shape = [jax.ShapeDtypeStruct((B, S, N), dt) for dt in out_dtypes]
    out_specs = [pl.BlockSpec((1, tm, tn), lambda b, i, j: (b, i, j)) for _ in out_dtypes]
    if split_heads_first:
        out_shape[0] = jax.ShapeDtypeStruct((B, S, N // head_dim, head_dim), out_dtypes[0])
        out_specs[0] = pl.BlockSpec((1, tm, tn // head_dim, head_dim), lambda b, i, j: (b, i, j, 0))
    if has_gate:
        out_shape.append(jax.ShapeDtypeStruct((B, S, G), F32))
        out_specs.append(pl.BlockSpec((1, tm, G), lambda b, i, j: (b, i, 0)))
    kern = functools.partial(
        _norm_matmul_kernel, n_out=len(out_dtypes), n_head_blocks=n_head_cols // tn,
        all_head_blocks=n_head_cols == N, head_dim=head_dim, has_gain=has_gain, has_gate=has_gate)
    outs = pl.pallas_call(
        kern, grid=(B, S // tm, N // tn), in_specs=in_specs, out_specs=out_specs,
        out_shape=out_shape, scratch_shapes=[pltpu.VMEM((tm, D), BF16)],
        compiler_params=_params("parallel", "parallel", "arbitrary"), name="norm_matmul",
    )(*args)
    return [o.reshape(lead_shape + o.shape[2:]) for o in outs]


def _matmul_residual_kernel(a_ref, w_ref, x_ref, o_ref):
    o_ref[0] = x_ref[0] + _dot(a_ref[0], w_ref[...])


def _matmul_residual(a, w, x):
    out_shape = x.shape
    a = a.reshape(1, -1, a.shape[-1])
    x = x.reshape(1, -1, x.shape[-1])
    B, S, K = a.shape
    N = w.shape[1]
    tm = _tile(S, 1024)
    tn = _tile(N, 1024, 128)
    return pl.pallas_call(
        _matmul_residual_kernel, grid=(B, S // tm, N // tn),
        in_specs=[pl.BlockSpec((1, tm, K), lambda b, i, j: (b, i, 0)),
                  pl.BlockSpec((K, tn), lambda b, i, j: (0, j)),
                  pl.BlockSpec((1, tm, tn), lambda b, i, j: (b, i, j))],
        out_specs=pl.BlockSpec((1, tm, tn), lambda b, i, j: (b, i, j)),
        out_shape=jax.ShapeDtypeStruct((B, S, N), F32),
        compiler_params=_params("parallel", "parallel", "arbitrary"), name="matmul_residual",
    )(a, w, x).reshape(out_shape)


def _fold_gate_kernel(wq_ref, wk_ref, wv_ref, wg_ref, a_ref, b_ref, *, k_scale):
    def hdot(a, b):
        return jnp.dot(a, b, precision=HIGHEST, preferred_element_type=F32)

    a_ref[0, 0] = hdot(wq_ref[0, 0], wg_ref[0, 0, 0]) + hdot(wk_ref[0, 0], wg_ref[0, 1, 0]) * k_scale
    b_ref[0, 0] = hdot(wv_ref[0, 0], wg_ref[0, 2, 0])


def _fold_gate_weights(wq, wk, wv, w_gate):
    NA, H, DH, _ = wq.shape
    G = w_gate.shape[-1]
    w_spec = pl.BlockSpec((1, 1, DH, DH), lambda i, h: (i, h, 0, 0))
    o_spec = pl.BlockSpec((1, 1, DH, G), lambda i, h: (i, h, 0, 0))
    out = jax.ShapeDtypeStruct((NA, H, DH, G), F32)
    return pl.pallas_call(
        functools.partial(_fold_gate_kernel, k_scale=float(DH) ** -0.5), grid=(NA, H),
        in_specs=[w_spec, w_spec, w_spec,
                  pl.BlockSpec((1, 3, 1, DH, G), lambda i, h: (i, 0, h, 0, 0))],
        out_specs=[o_spec, o_spec], out_shape=[out, out],
        compiler_params=_params("parallel", "parallel"), name="fold_gate_weights",
    )(wq, wk, wv, w_gate.reshape(NA, 3, H, DH, G))


def _conv_gates_kernel(xm_ref, buf_ref, cw_ref, cb_ref, wa_ref, wb_ref, bg_ref,
                       xc_ref, g_ref, seq_ref, *, rows, cum_rows, conv_w, n_heads):
    h = pl.program_id(1)
    S = xm_ref.shape[1]
    seq_ref[0:CONV_PAD, :] = buf_ref[0]
    seq_ref[CONV_PAD:CONV_PAD + S, :] = xm_ref[0].astype(F32)
    for r in range(S // rows):
        base = r * rows
        acc = jnp.broadcast_to(cb_ref[0], (rows, cb_ref.shape[-1]))
        for i in range(conv_w):
            start = base + CONV_PAD - (conv_w - 1) + i
            acc = acc + seq_ref[start:start + rows, :] * cw_ref[0, i:i + 1, :]
        xc = _silu(acc).astype(BF16)
        xc_ref[0, base:base + rows, :] = xc
        part = _dot(xc, wa_ref[0]) + _dot(xm_ref[0, base:base + rows, :], wb_ref[0])

        @pl.when(h == 0)
        def _():
            g_ref[0, base:base + rows, :] = part + bg_ref[...]

        @pl.when(h > 0)
        def _():
            g_ref[0, base:base + rows, :] = g_ref[0, base:base + rows, :] + part

    @pl.when(h == n_heads - 1)
    def _():
        r = lax.broadcasted_iota(jnp.int32, (cum_rows, cum_rows), 0)
        c = lax.broadcasted_iota(jnp.int32, (cum_rows, cum_rows), 1)
        trilf = (c <= r).astype(F32)
        lane = lax.broadcasted_iota(jnp.int32, (cum_rows, g_ref.shape[-1]), 1)
        carry = jnp.zeros((1, g_ref.shape[-1]), F32)
        for i in range(S // cum_rows):
            g = g_ref[0, i * cum_rows:(i + 1) * cum_rows, :]
            lf = jnp.where(lane >= n_heads, _log_sigmoid(g), 0.0)
            cum = jnp.dot(trilf, lf, precision=HIGHEST, preferred_element_type=F32) + carry
            g_ref[0, i * cum_rows:(i + 1) * cum_rows, :] = jnp.where(lane >= n_heads, cum, g)
            carry = cum[cum_rows - 1:cum_rows, :]


def _conv_gates(xz, buf, conv_w, conv_b, w_xc, w_xm, b_gate):
    B, S, _ = xz.shape
    H, DH, G = w_xc.shape
    AI = H * DH
    CW = conv_w.shape[0]
    rows = _tile(S, 768)
    act_spec = pl.BlockSpec((1, S, DH), lambda b, h: (b, 0, h))
    w_spec = pl.BlockSpec((1, DH, G), lambda b, h: (h, 0, 0))
    kern = functools.partial(_conv_gates_kernel, rows=rows, cum_rows=_tile(S, 512, 8), conv_w=CW,
                             n_heads=H)
    return pl.pallas_call(
        kern, grid=(B, H),
        in_specs=[act_spec,
                  pl.BlockSpec((1, CONV_PAD, DH), lambda b, h: (b, 0, h)),
                  pl.BlockSpec((1, CW, DH), lambda b, h: (h, 0, 0)),
                  pl.BlockSpec((1, 1, DH), lambda b, h: (h, 0, 0)),
                  w_spec, w_spec,
                  pl.BlockSpec((1, G), lambda b, h: (0, 0))],
        out_specs=[act_spec, pl.BlockSpec((1, S, G), lambda b, h: (b, 0, 0))],
        out_shape=[jax.ShapeDtypeStruct((B, S, AI), BF16), jax.ShapeDtypeStruct((B, S, G), F32)],
        scratch_shapes=[pltpu.VMEM((S + CONV_PAD, DH), F32)],
        compiler_params=_params("parallel", "arbitrary"), name="conv_gates",
    )(xz, buf,
      conv_w.reshape(CW, H, DH).transpose(1, 0, 2).astype(F32),
      conv_b.reshape(H, 1, DH).astype(F32), w_xc, w_xm, b_gate.reshape(1, G).astype(F32))


def _in_conv_kernel(x_ref, g_ref, w_ref, buf_ref, cw_ref, cb_ref, wa_ref, wb_ref, bg_ref,
                    xm_ref, xc_ref, gate_ref, xn_ref, hist_ref, *, conv_w, n_heads):
    i = pl.program_id(1)
    j = pl.program_id(2)
    tm = x_ref.shape[1]

    @pl.when(j == 0)
    def _():
        x = x_ref[0].astype(F32)
        ms = jnp.mean(x * x, axis=-1, keepdims=True)
        xn_ref[...] = (x * lax.rsqrt(ms + EPS) * g_ref[...]).astype(BF16)

    prev = jnp.where(i == 0, buf_ref[0], hist_ref[j])
    row = lax.broadcasted_iota(jnp.int32, prev.shape, 0)
    mid = tm // 2 // BF16_ROWS * BF16_ROWS
    parts = []
    for r0, r1 in ((0, mid), (mid, tm)) if mid else ((0, tm),):
        acc = _dot(xn_ref[r0:r1, :], w_ref[...])
        xm = acc.astype(BF16)
        xm_ref[0, r0:r1, :] = xm
        y = cb_ref[...] + acc * cw_ref[conv_w - 1:conv_w, :]
        for shift in range(1, conv_w):
            rolled = pltpu.roll(acc, shift, 0)
            head = jnp.where(row < shift, pltpu.roll(prev, shift, 0), rolled[0:CONV_PAD, :])
            shifted = jnp.concatenate([head, rolled[CONV_PAD:, :]], axis=0)
            y = y + shifted * cw_ref[conv_w - 1 - shift:conv_w - shift, :]
        xc = _silu(y).astype(BF16)
        xc_ref[0, r0:r1, :] = xc
        parts.append(_dot(xc, wa_ref[...]) + _dot(xm, wb_ref[...]))
        prev = acc[r1 - r0 - CONV_PAD:r1 - r0, :]
    hist_ref[j] = prev
    part = jnp.concatenate(parts, axis=0)

    @pl.when(j == 0)
    def _():
        gate_ref[0] = part + bg_ref[...]

    @pl.when(j > 0)
    def _():
        gate_ref[0] = gate_ref[0] + part

    @pl.when(j == pl.num_programs(2) - 1)
    def _():
        g = gate_ref[0]
        lane = lax.broadcasted_iota(jnp.int32, g.shape, 1)
        gate_ref[0] = jnp.where(lane >= n_heads, _log_sigmoid(g), g)


def _in_conv(x, norm_g, w_xm_in, buf, conv_w, conv_b, w_xc, w_xm, b_gate):
    B, S, D = x.shape
    H, DH, G = w_xc.shape
    AI = H * DH
    CW = conv_w.shape[0]
    tm = _tile(S, 1024)
    tn = _tile(AI, 1024, LANES)
    act_spec = pl.BlockSpec((1, tm, tn), lambda b, i, j: (b, i, j))
    act = jax.ShapeDtypeStruct((B, S, AI), BF16)
    kern = functools.partial(_in_conv_kernel, conv_w=CW, n_heads=H)
    return pl.pallas_call(
        kern, grid=(B, S // tm, AI // tn),
        in_specs=[pl.BlockSpec((1, tm, D), lambda b, i, j: (b, i, 0)),
                  pl.BlockSpec((1, D), lambda b, i, j: (0, 0)),
                  pl.BlockSpec((D, tn), lambda b, i, j: (0, j)),
                  pl.BlockSpec((1, CONV_PAD, tn), lambda b, i, j: (b, 0, j)),
                  pl.BlockSpec((CW, tn), lambda b, i, j: (0, j)),
                  pl.BlockSpec((1, tn), lambda b, i, j: (0, j)),
                  pl.BlockSpec((tn, G), lambda b, i, j: (j, 0)),
                  pl.BlockSpec((tn, G), lambda b, i, j: (j, 0)),
                  pl.BlockSpec((1, G), lambda b, i, j: (0, 0))],
        out_specs=[act_spec, act_spec, pl.BlockSpec((1, tm, G), lambda b, i, j: (b, i, 0))],
        out_shape=[act, act, jax.ShapeDtypeStruct((B, S, G), F32)],
        scratch_shapes=[pltpu.VMEM((tm, D), BF16), pltpu.VMEM((AI // tn, CONV_PAD, tn), F32)],
        compiler_params=_params("parallel", "arbitrary", "arbitrary"), name="in_conv",
    )(x, norm_g.reshape(1, D).astype(F32), w_xm_in, buf, conv_w.astype(F32),
      conv_b.reshape(1, AI).astype(F32), w_xc.reshape(AI, G), w_xm.reshape(AI, G),
      b_gate.reshape(1, G).astype(F32))


def _mlstm_project(xm, xc, wq, wk, wv):
    return (_dot(xc, wq).astype(BF16), _dot(xc, wk).astype(BF16), _dot(xm, wv).astype(BF16))


def _mlstm_chunk(q, k, v, xc, z, g_rows, g_cols, head, n_heads, og, skip,
                 C_ref, n_ref, m_ref, b_ref):
    L = q.shape[0]
    row = lax.broadcasted_iota(jnp.int32, (L, L), 0)
    col = lax.broadcasted_iota(jnp.int32, (L, L), 1)
    tril = col <= row
    lane = lax.broadcasted_iota(jnp.int32, g_cols.shape, 1)
    ig_col = jnp.sum(jnp.where(lane == head, g_cols, 0.0), axis=-1, keepdims=True)
    f_col = jnp.sum(jnp.where(lane == head + n_heads, g_cols, 0.0), axis=-1, keepdims=True)
    ig_row = g_rows[0:1, :]
    f_row = g_rows[1:2, :]
    m_prev = m_ref[...]
    f_prev = b_ref[...]
    log_d = jnp.where(tril, f_col - f_row + ig_row, NEG_INF)
    inter = f_col - f_prev + m_prev
    m_t = jnp.maximum(jnp.max(log_d, axis=-1, keepdims=True), inter)
    d = jnp.exp(log_d - m_t)
    a = jnp.exp(inter - m_t)
    s = _dot_nt(q, k) * d
    C = C_ref[...]
    n = n_ref[...]
    num = _dot(s.astype(BF16), v) + a * _dot(q, C.astype(BF16))
    qn = _dot_nt(q, jnp.broadcast_to(n, (8, n.shape[1])).astype(BF16))[:, 0:1]
    den = jnp.sum(s, axis=-1, keepdims=True) + a * qn
    hid = num * (1.0 / jnp.maximum(jnp.abs(den), jnp.exp(-m_t)))
    m_new = m_t[L - 1:L, :]
    f_last = f_row[:, L - 1:L]
    decay = jnp.exp(f_last - f_prev + m_prev - m_new)
    w_col = jnp.exp(f_last - f_col + ig_col - m_new)
    C_ref[...] = decay * C + _dot_tn(k, (w_col * v.astype(F32)).astype(BF16))
    n_ref[...] = decay * n + jnp.sum(w_col * k.astype(F32), axis=0, keepdims=True)
    m_ref[...] = m_new
    b_ref[...] = f_last
    hc = hid - jnp.mean(hid, axis=-1, keepdims=True)
    hn = hc * lax.rsqrt(jnp.mean(hc * hc, axis=-1, keepdims=True) + EPS) * og
    return ((hn + skip * xc.astype(F32)) * _silu(z.astype(F32))).astype(BF16)


def _mlstm_kernel(*refs, lead, chunk, n_chunks, n_heads, hps, has_state):
    xm_ref, xc_ref, z_ref, wq_ref, wk_ref, wv_ref = refs[:6]
    pos = 6
    if lead:
        gl_ref = refs[pos]
        pos += 1
    gm_ref, gc_ref, og_ref, skip_ref = refs[pos:pos + 4]
    pos += 4
    if has_state:
        c0_ref, n0_ref, m0_ref = refs[pos:pos + 3]
        pos += 3
    pos += 1
    o_ref, C_ref, n_ref, m_ref, b_ref = refs[pos:pos + 5]
    DH = wq_ref.shape[-1]
    for hh in range(hps):
        if has_state:
            C_ref[0, 0, hh] = c0_ref[0, 0, hh]
            n_ref[0, hh] = n0_ref[0, 0, hh]
            m_ref[0, hh] = m0_ref[0, 0, hh]
        else:
            C_ref[0, 0, hh] = jnp.zeros((DH, DH), F32)
            n_ref[0, hh] = jnp.zeros((1, DH), F32)
            m_ref[0, hh] = jnp.zeros((1, 1), F32)
    b_ref[...] = jnp.zeros(b_ref.shape, F32)
    head0 = pl.program_id(1) * hps

    def chunk_rows(c):
        return pl.ds(pl.multiple_of(lead + c * chunk, BF16_ROWS), chunk)

    def project(rows, hh):
        cs = slice(hh * DH, (hh + 1) * DH)
        return _mlstm_project(xm_ref[0, rows, cs], xc_ref[0, rows, cs], wq_ref[hh], wk_ref[hh],
                              wv_ref[hh])

    def recur(rows, hh, qkv, g_rows):
        cs = slice(hh * DH, (hh + 1) * DH)
        o_ref[0, rows, cs] = _mlstm_chunk(
            *qkv, xc_ref[0, rows, cs], z_ref[0, rows, cs], g_rows, gc_ref[0, rows, :],
            head0 + hh, n_heads, og_ref[hh], skip_ref[hh], C_ref.at[0, 0, hh], n_ref.at[0, hh],
            m_ref.at[0, hh], b_ref.at[hh])

    for hh in range(hps):
        if lead:
            recur(pl.ds(0, lead), hh, project(pl.ds(0, lead), hh), gl_ref[0, hh])

    def body(c, carry):
        for hh in range(hps):
            recur(chunk_rows(c), hh, project(chunk_rows(c), hh), gm_ref[0, hh, c])
        return carry

    lax.fori_loop(0, n_chunks, body, 0, unroll=_tile(n_chunks, 2, 1))


def _mlstm(xm, z, z_head0, xc, gates, wq, wk, wv, out_g, skip, state, lead, chunk, c_stack, layer,
           n_layers, heads_per_step=1):
    B, S, AI = xc.shape
    H, DH = out_g.shape
    n_chunks = (S - lead) // chunk
    assert lead + n_chunks * chunk == S
    g = gates.reshape(B, S, 2, H).transpose(0, 3, 2, 1)
    g_main = g[..., lead:].reshape(B, H, 2, n_chunks, chunk).transpose(0, 1, 3, 2, 4)
    hps = heads_per_step if H % heads_per_step == 0 and z_head0 % heads_per_step == 0 else 1
    act_spec = pl.BlockSpec((1, S, hps * DH), lambda b, h: (b, 0, h))
    w_spec = pl.BlockSpec((hps, DH, DH), lambda b, h: (h, 0, 0))
    in_specs = [act_spec, act_spec,
                pl.BlockSpec((1, S, hps * DH), lambda b, h: (b, 0, z_head0 // hps + h)),
                w_spec, w_spec, w_spec]
    args = [xm, xc, z, wq, wk, wv]
    if lead:
        in_specs.append(pl.BlockSpec((1, hps, 2, lead), lambda b, h: (b, h, 0, 0)))
        args.append(g[..., :lead])
    in_specs += [pl.BlockSpec((1, hps, n_chunks, 2, chunk), lambda b, h: (b, h, 0, 0, 0)),
                 pl.BlockSpec((1, S, 2 * H), lambda b, h: (b, 0, 0)),
                 pl.BlockSpec((hps, 1, DH), lambda b, h: (h, 0, 0)),
                 pl.BlockSpec((hps, 1, DH), lambda b, h: (h, 0, 0))]
    args += [g_main, gates, out_g.reshape(H, 1, DH).astype(F32), skip.reshape(H, 1, DH).astype(F32)]
    n_spec = pl.BlockSpec((1, hps, 1, DH), lambda b, h: (b, h, 0, 0))
    m_spec = pl.BlockSpec((1, hps, 1, 1), lambda b, h: (b, h, 0, 0))
    has_state = state is not None
    if has_state:
        C0, n0, m0 = state
        NL = C0.shape[0]
        in_specs += [pl.BlockSpec((1, 1, hps, DH, DH), lambda b, h: (layer, b, h, 0, 0)),
                     pl.BlockSpec((1, 1, hps, 1, DH), lambda b, h: (layer, b, h, 0, 0)),
                     pl.BlockSpec((1, 1, hps, 1, 1), lambda b, h: (layer, b, h, 0, 0))]
        args += [C0.astype(F32), n0.reshape(NL, B, H, 1, DH).astype(F32),
                 m0.reshape(NL, B, H, 1, 1).astype(F32)]
    if c_stack is None:
        c_stack = pl.empty((n_layers, B, H, DH, DH), F32)
    aliases = {len(args): 1}
    in_specs.append(pl.BlockSpec(memory_space=pl.ANY))
    args.append(c_stack)
    kern = functools.partial(_mlstm_kernel, lead=lead, chunk=chunk, n_chunks=n_chunks, n_heads=H,
                             hps=hps, has_state=has_state)
    hz, C, n, m = pl.pallas_call(
        kern, grid=(B, H // hps), in_specs=in_specs,
        out_specs=[act_spec,
                   pl.BlockSpec((1, 1, hps, DH, DH), lambda b, h: (layer, b, h, 0, 0)),
                   n_spec, m_spec],
        out_shape=[jax.ShapeDtypeStruct((B, S, AI), BF16),
                   jax.ShapeDtypeStruct((n_layers, B, H, DH, DH), F32),
                   jax.ShapeDtypeStruct((B, H, 1, DH), F32),
                   jax.ShapeDtypeStruct((B, H, 1, 1), F32)],
        scratch_shapes=[pltpu.VMEM((hps, 1, 1), F32)],
        input_output_aliases=aliases,
        compiler_params=_params("parallel", "parallel"), name="mlstm",
    )(*args)
    return hz, C, n.reshape(B, H, DH), m.reshape(B, H)


def _cumsum_kernel(x_ref, o_ref, *, scale, first_lane):
    x = x_ref[0]
    S = x.shape[0]
    summed = lax.broadcasted_iota(jnp.int32, x.shape, 1) >= first_lane
    row = lax.broadcasted_iota(jnp.int32, x.shape, 0)
    y = jnp.where(summed, x, 0.0)
    step = 1
    while step < S:
        y = y + jnp.where(row >= step, pltpu.roll(y, step, 0), 0.0)
        step *= 2
    o_ref[0] = jnp.where(summed, y * scale, x)


def _cumsum(x, scale=1.0, first_lane=0):
    B, S, G = x.shape
    spec = pl.BlockSpec((1, S, G), lambda b: (b, 0, 0))
    return pl.pallas_call(
        functools.partial(_cumsum_kernel, scale=scale, first_lane=first_lane),
        grid=(B,), in_specs=[spec], out_specs=spec, out_shape=jax.ShapeDtypeStruct((B, S, G), F32),
        compiler_params=_params("parallel"), name="cumsum",
    )(x)


def _lane_fold(op, acc, x):
    for c in range(x.shape[1] // LANES):
        acc = op(acc, x[:, c * LANES:(c + 1) * LANES])
    return acc


def _fox_prompt_kernel(q_ref, z_ref, k_ref, v_ref, fq_ref, fkl_ref, fkm_ref, o_ref,
                       s_ref, sl_ref, fq_rep, m_ref, l_ref, acc_ref, *, lead, blk, n_blk, heads,
                       dim):
    n_lane = blk // LANES
    cols = [slice(hh * dim, (hh + 1) * dim) for hh in range(heads)]

    def key_rows(j):
        return pl.ds(pl.multiple_of(lead + j * blk, BF16_ROWS), blk)

    def tiled(x):
        return jnp.concatenate([x] * n_lane, axis=1)

    def q_block(i, carry):
        qrows = pl.ds(pl.multiple_of(i * blk, blk), blk)

        def logits(hh, j):
            return (_dot_nt(q_ref[0, qrows, cols[hh]], k_ref[0, key_rows(j), cols[hh]])
                    + tiled(fq_rep[hh]) - fkm_ref[0, 0, j, hh:hh + 1, :])

        for hh in range(heads):
            fq_rep[hh] = jnp.broadcast_to(fq_ref[0, 0, qrows, hh:hh + 1], (blk, LANES))
        if lead:
            for hh in range(heads):
                s = (_dot_nt(q_ref[0, qrows, cols[hh]], k_ref[0, 0:LANES, cols[hh]])
                     + fq_rep[hh] - fkl_ref[0, 0, hh:hh + 1, :])
                sl_ref[hh] = s
                m_ref[hh] = s
        else:
            m_ref[...] = jnp.full(m_ref.shape, NEG_INF, F32)

        def pass1(j, c):
            for hh in range(heads):
                s = logits(hh, j)
                s_ref[hh, j] = s
                m_ref[hh] = _lane_fold(jnp.maximum, m_ref[hh], s)
            return c

        lax.fori_loop(0, i, pass1, 0)
        row = lax.broadcasted_iota(jnp.int32, (blk, blk), 0)
        col = lax.broadcasted_iota(jnp.int32, (blk, blk), 1)
        for hh in range(heads):
            s = jnp.where(col <= row, logits(hh, i), NEG_INF)
            s_ref[hh, i] = s
            m = jnp.max(_lane_fold(jnp.maximum, m_ref[hh], s), axis=-1, keepdims=True)
            m_ref[hh] = jnp.broadcast_to(m, (blk, LANES))
        if lead:
            for hh in range(heads):
                p = jnp.exp2(sl_ref[hh] - m_ref[hh])
                l_ref[hh] = p
                acc_ref[hh] = _dot(p.astype(BF16), v_ref[0, 0:LANES, cols[hh]])
        else:
            l_ref[...] = jnp.zeros(l_ref.shape, F32)
            acc_ref[...] = jnp.zeros(acc_ref.shape, F32)

        def pass2(j, c):
            for hh in range(heads):
                p = jnp.exp2(s_ref[hh, j] - tiled(m_ref[hh]))
                l_ref[hh] = _lane_fold(jnp.add, l_ref[hh], p)
                acc_ref[hh] += _dot(p.astype(BF16), v_ref[0, key_rows(j), cols[hh]])
            return c

        lax.fori_loop(0, i + 1, pass2, 0)
        for hh in range(heads):
            l = jnp.sum(l_ref[hh], axis=-1, keepdims=True)
            o_ref[0, qrows, cols[hh]] = (acc_ref[hh] * (1.0 / l)
                                         * _silu(z_ref[0, qrows, cols[hh]].astype(F32))).astype(BF16)
        return carry

    lax.fori_loop(0, n_blk, q_block, 0)


def _fox_prompt(qz, kb, vb, f_cum, lead, heads_per_step=4, blk=512):
    B, SQ, W2 = qz.shape
    W = W2 // 2
    NH = f_cum.shape[-1]
    dim = W // NH
    hg = min(heads_per_step, NH)
    G = NH // hg
    blk = _tile(SQ, blk, LANES)
    n_blk = SQ // blk
    SK = lead + SQ
    assert lead <= LANES <= SK
    fq = f_cum[:, lead:].reshape(B, SQ, G, hg).transpose(0, 2, 1, 3)
    fk = f_cum.transpose(0, 2, 1).reshape(B, G, hg, SK)
    fk_lead = jnp.pad(fk[..., :lead], ((0, 0), (0, 0), (0, 0), (0, LANES - lead)),
                      constant_values=float("inf"))
    fk_main = fk[..., lead:].reshape(B, G, hg, n_blk, blk).transpose(0, 1, 3, 2, 4)
    wcols = hg * dim
    kern = functools.partial(_fox_prompt_kernel, lead=lead, blk=blk, n_blk=n_blk, heads=hg, dim=dim)
    return pl.pallas_call(
        kern, grid=(B, G),
        in_specs=[pl.BlockSpec((1, SQ, wcols), lambda b, g: (b, 0, g)),
                  pl.BlockSpec((1, SQ, wcols), lambda b, g: (b, 0, G + g)),
                  pl.BlockSpec((1, SK, wcols), lambda b, g: (b, 0, g)),
                  pl.BlockSpec((1, SK, wcols), lambda b, g: (b, 0, g)),
                  pl.BlockSpec((1, 1, SQ, hg), lambda b, g: (b, g, 0, 0)),
                  pl.BlockSpec((1, 1, hg, LANES), lambda b, g: (b, g, 0, 0)),
                  pl.BlockSpec((1, 1, n_blk, hg, blk), lambda b, g: (b, g, 0, 0, 0))],
        out_specs=pl.BlockSpec((1, SQ, wcols), lambda b, g: (b, 0, g)),
        out_shape=jax.ShapeDtypeStruct((B, SQ, W), BF16),
        scratch_shapes=[pltpu.VMEM((hg, n_blk, blk, blk), F32)]
        + [pltpu.VMEM((hg, blk, LANES), F32)] * 4 + [pltpu.VMEM((hg, blk, dim), F32)],
        compiler_params=_params("parallel", "parallel"), name="fox_prompt",
    )(qz, qz, kb, vb, fq, fk_lead, fk_main)


def _pack_heads_kernel(x_ref, o_ref, *, heads, dim):
    rows = o_ref.shape[1]
    for h in range(heads):
        o_ref[0, :, h * dim:(h + 1) * dim] = x_ref[0, pl.ds(h, rows, stride=heads), :].astype(o_ref.dtype)


def _pack_heads(x, dtype):
    B, P, NH, dim = x.shape
    rows = _tile(P, 512)
    return pl.pallas_call(
        functools.partial(_pack_heads_kernel, heads=NH, dim=dim), grid=(B, P // rows),
        in_specs=[pl.BlockSpec((1, rows * NH, dim), lambda b, i: (b, i, 0))],
        out_specs=pl.BlockSpec((1, rows, NH * dim), lambda b, i: (b, i, 0)),
        out_shape=jax.ShapeDtypeStruct((B, P, NH * dim), dtype),
        compiler_params=_params("parallel", "parallel"), name="pack_heads",
    )(x.reshape(B, P * NH, dim))


def _fox_decode_kernel(q_ref, z_ref, ck_ref, cv_ref, k_ref, v_ref, fq_ref, fkc_ref, fkn_ref, o_ref,
                       *, heads, dim):
    Q = q_ref.shape[1]
    row = lax.broadcasted_iota(jnp.int32, (Q, Q), 0)
    col = lax.broadcasted_iota(jnp.int32, (Q, Q), 1)
    outs = []
    for hh in range(heads):
        cs = slice(hh * dim, (hh + 1) * dim)
        q = q_ref[0, :, cs]
        fq = fq_ref[0, 0, :, hh:hh + 1]
        s_c = _dot_nt(q, ck_ref[0, :, cs].astype(BF16)) + fq - fkc_ref[0, 0, hh:hh + 1, :]
        s_n = jnp.where(col <= row, _dot_nt(q, k_ref[0, :, cs]) + fq - fkn_ref[0, 0, hh:hh + 1, :],
                        NEG_INF)
        m = jnp.maximum(jnp.max(s_c, axis=-1, keepdims=True), jnp.max(s_n, axis=-1, keepdims=True))
        p_c = jnp.exp2(s_c - m)
        p_n = jnp.exp2(s_n - m)
        l = jnp.sum(p_c, axis=-1, keepdims=True) + jnp.sum(p_n, axis=-1, keepdims=True)
        acc = _dot(p_c.astype(BF16), cv_ref[0, :, cs].astype(BF16)) + _dot(p_n.astype(BF16), v_ref[0, :, cs])
        outs.append((acc * (1.0 / l) * _silu(z_ref[0, :, cs].astype(F32))).astype(BF16))
    o_ref[0] = jnp.concatenate(outs, axis=-1)


def _fox_decode(qz, cache_k, cache_v, kb, vb, f_cum, heads_per_step=4):
    B, Q, W2 = qz.shape
    W = W2 // 2
    P = cache_k.shape[1]
    NH = f_cum.shape[-1]
    dim = W // NH
    hg = min(heads_per_step, NH)
    G = NH // hg
    fq = f_cum[:, P:].reshape(B, Q, G, hg).transpose(0, 2, 1, 3)
    fk = f_cum.transpose(0, 2, 1).reshape(B, G, hg, P + Q)
    wcols = hg * dim
    kern = functools.partial(_fox_decode_kernel, heads=hg, dim=dim)
    return pl.pallas_call(
        kern, grid=(B, G),
        in_specs=[pl.BlockSpec((1, Q, wcols), lambda b, g: (b, 0, g)),
                  pl.BlockSpec((1, Q, wcols), lambda b, g: (b, 0, G + g)),
                  pl.BlockSpec((1, P, wcols), lambda b, g: (b, 0, g)),
                  pl.BlockSpec((1, P, wcols), lambda b, g: (b, 0, g)),
                  pl.BlockSpec((1, Q, wcols), lambda b, g: (b, 0, g)),
                  pl.BlockSpec((1, Q, wcols), lambda b, g: (b, 0, g)),
                  pl.BlockSpec((1, 1, Q, hg), lambda b, g: (b, g, 0, 0)),
                  pl.BlockSpec((1, 1, hg, P), lambda b, g: (b, g, 0, 0)),
                  pl.BlockSpec((1, 1, hg, Q), lambda b, g: (b, g, 0, 0))],
        out_specs=pl.BlockSpec((1, Q, wcols), lambda b, g: (b, 0, g)),
        out_shape=jax.ShapeDtypeStruct((B, Q, W), BF16),
        compiler_params=_params("parallel", "parallel"), name="fox_decode",
    )(qz, qz, cache_k, cache_v, kb, vb, fq, fk[..., :P], fk[..., P:])


def _mlstm_layer(x, conv_hist, state, w, lead, chunk, c_stack, layer, n_layers):
    (norm_g, w_in, conv_w, conv_b, wq, wk, wv, w_xc, w_xm, b_gate, out_g, skip, w_out) = w
    B, S, _ = x.shape
    H, DH, _ = wq.shape
    AI = H * DH
    CW = conv_w.shape[0]
    assert S >= CW - 1 and CW - 1 <= CONV_PAD
    buf = jnp.zeros((B, CONV_PAD, AI), F32)
    if conv_hist is not None:
        buf = buf.at[:, CONV_PAD - (CW - 1):].set(conv_hist.astype(F32))
    if S >= MIN_FUSED_CONV_ROWS:
        wb = w_in.astype(BF16)
        xm, xc, gates = _in_conv(x, norm_g, wb[:, :AI], buf, conv_w, conv_b,
                                 w_xc.astype(BF16), w_xm.astype(BF16), b_gate)
        gates = _cumsum(gates, first_lane=H)
        (z,) = _norm_matmul(x, norm_g, wb[:, AI:], out_dtypes=[BF16])
        z_head0 = 0
    else:
        (xm,) = _norm_matmul(x, norm_g, w_in.astype(BF16), out_dtypes=[BF16])
        xc, gates = _conv_gates(xm, buf, conv_w, conv_b, w_xc.astype(BF16), w_xm.astype(BF16),
                                b_gate)
        z, z_head0 = xm, H
    hz, c_stack, n, m = _mlstm(
        xm, z, z_head0, xc, gates, wq.astype(BF16), (wk * float(DH) ** -0.5).astype(BF16),
        wv.astype(BF16), out_g, skip.reshape(H, DH), state, lead, chunk, c_stack, layer, n_layers)
    x_new = _matmul_residual(hz, w_out.astype(BF16), x)
    new_hist = xm[:, S - (CW - 1):, :AI].astype(F32)
    return x_new, new_hist, c_stack, n, m


def _shared_kv(x, kv_norm_g, w_kvf, b_f, k_norm_g):
    NH, dim = k_norm_g.shape
    W = NH * dim
    wb = w_kvf.astype(BF16)
    k32, kb, logf = _norm_matmul(
        x, kv_norm_g, wb[:, :W], out_dtypes=[F32, BF16], head_gain=k_norm_g.reshape(1, W),
        n_head_cols=W, head_dim=dim, gate_w=wb[:, 2 * W:], gate_b=b_f, split_heads_first=True)
    v32, vb = _norm_matmul(x, kv_norm_g, wb[:, W:2 * W], out_dtypes=[F32, BF16], head_dim=dim,
                           split_heads_first=True)
    return k32, kb, v32, vb, logf


def _fox_in(x, norm_g, w_in, q_norm_g):
    NH, dim = q_norm_g.shape
    W = NH * dim
    gain = jnp.concatenate([q_norm_g.reshape(1, W).astype(F32) * (float(dim) ** -0.5 * LOG2E),
                            jnp.ones((1, W), F32)], axis=-1)
    (qz,) = _norm_matmul(x, norm_g, w_in.astype(BF16), out_dtypes=[BF16], head_gain=gain,
                         n_head_cols=W, head_dim=dim)
    return qz


def kernel(x_prompt, x_sample, cache_k, cache_v, cache_logf, state_C, state_n, state_m, state_conv,
           meta_tokens, a_norm_g, a_w_in, a_conv_w, a_conv_b, a_wq, a_wk, a_wv, a_w_gate, a_b_gate,
           a_out_g, a_skip, a_w_out, kv_norm_g, w_kvf, b_f, k_norm_g, b_norm_g, b_w_in, q_norm_g,
           b_w_out):
    B, SEQ, D = x_prompt.shape
    DB, DEC, _ = x_sample.shape
    NM = meta_tokens.shape[0]
    N_A = a_norm_g.shape[0]
    N_B = b_norm_g.shape[0]
    NH, dim = k_norm_g.shape
    W = NH * dim
    P = cache_k.shape[1]
    chunk = _tile(SEQ, 256)

    xp = jnp.concatenate(
        [jnp.broadcast_to(meta_tokens.astype(x_prompt.dtype)[None], (B, NM, D)), x_prompt], axis=1)
    xs = x_sample
    p_state, s_state = [], []
    p_C = s_C = None
    w_xc, w_xm = _fold_gate_weights(a_wq, a_wk, a_wv, a_w_gate)
    for i in range(N_A):
        w = (a_norm_g[i], a_w_in[i], a_conv_w[i], a_conv_b[i], a_wq[i], a_wk[i], a_wv[i],
             w_xc[i], w_xm[i], a_b_gate[i], a_out_g[i], a_skip[i], a_w_out[i])
        xp, hist, p_C, n, m = _mlstm_layer(xp, None, None, w, NM, chunk, p_C, i, N_A)
        p_state.append((hist, n, m))
        xs, hist, s_C, n, m = _mlstm_layer(xs, state_conv[i], (state_C, state_n, state_m),
                                           w, 0, DEC, s_C, i, N_A)
        s_state.append((hist, n, m))

    pk32, pkb, pv32, pvb, p_logf = _shared_kv(xp, kv_norm_g, w_kvf, b_f, k_norm_g)
    sk32, skb, sv32, svb, s_logf = _shared_kv(xs, kv_norm_g, w_kvf, b_f, k_norm_g)
    fp = _cumsum(p_logf, LOG2E)
    fs = _cumsum(jnp.concatenate([cache_logf.astype(F32), s_logf], axis=1), LOG2E)
    ck = _pack_heads(cache_k, BF16)
    cv = _pack_heads(cache_v, BF16)
    xp = xp[:, NM:]
    for j in range(N_B):
        qz = _fox_in(xp, b_norm_g[j], b_w_in[j], q_norm_g[j])
        xp = _matmul_residual(_fox_prompt(qz, pkb, pvb, fp, NM), b_w_out[j].astype(BF16), xp)
        qz = _fox_in(xs, b_norm_g[j], b_w_in[j], q_norm_g[j])
        xs = _matmul_residual(_fox_decode(qz, ck, cv, skb, svb, fs), b_w_out[j].astype(BF16), xs)

    def stack(states, idx):
        return jnp.stack([st[idx] for st in states])

    return (xp, xs, pk32, pv32, p_logf,
            p_C, stack(p_state, 1), stack(p_state, 2), stack(p_state, 0),
            sk32, sv32, s_logf,
            s_C, stack(s_state, 1), stack(s_state, 2), stack(s_state, 0))
```

```python
import functools

import jax
import jax.numpy as jnp
from jax import lax
from jax.experimental import pallas as pl
from jax.experimental.pallas import tpu as pltpu

F32 = jnp.float32
BF16 = jnp.bfloat16
EPS = 1e-6
CONV_PAD = 8
BF16_ROWS = 16
LANES = 128
MIN_FUSED_CONV_ROWS = 256
VMEM_LIMIT_BYTES = 56 * 1024 * 1024
HIGHEST = lax.Precision.HIGHEST
NEG_INF = float("-inf")
LOG2E = 1.4426950408889634


def _params(*sem):
    return pltpu.CompilerParams(dimension_semantics=sem, vmem_limit_bytes=VMEM_LIMIT_BYTES)


def _tile(n, cap, mult=BF16_ROWS):
    best = None
    for t in range(mult, min(n, cap) + 1, mult):
        if n % t == 0:
            best = t
    return best if best is not None else n


def _dot(a, b):
    return jnp.dot(a, b, preferred_element_type=F32)


def _dot_nt(a, b):
    return lax.dot_general(a, b, (((1,), (1,)), ((), ())), preferred_element_type=F32)


def _dot_tn(a, b):
    return lax.dot_general(a, b, (((0,), (0,)), ((), ())), preferred_element_type=F32)


def _silu(x):
    return x / (1.0 + jnp.exp(-x))


def _log_sigmoid(x):
    return jnp.minimum(x, 0.0) - jnp.log1p(jnp.exp(-jnp.abs(x)))


def _norm_matmul_kernel(*refs, n_out, n_head_blocks, all_head_blocks, head_dim, has_gain,
                        has_gate):
    x_ref, g_ref, w_ref = refs[:3]
    pos = 3
    if has_gain:
        gain_ref = refs[pos]
        pos += 1
    if has_gate:
        gw_ref, gb_ref = refs[pos:pos + 2]
        pos += 2
    out_refs = refs[pos:pos + n_out]
    pos += n_out
    if has_gate:
        gate_out_ref = refs[pos]
        pos += 1
    xn_ref = refs[pos]
    j = pl.program_id(2)

    @pl.when(j == 0)
    def _():
        x = x_ref[0].astype(F32)
        ms = jnp.mean(x * x, axis=-1, keepdims=True)
        xn_ref[...] = (x * lax.rsqrt(ms + EPS) * g_ref[...]).astype(BF16)
        if has_gate:
            gate_out_ref[0] = _log_sigmoid(_dot(xn_ref[...], gw_ref[...]) + gb_ref[...])

    acc = _dot(xn_ref[...], w_ref[...])

    def store(val):
        for o in out_refs:
            if len(o.shape) == 4:
                o[0] = val.reshape(o.shape[1:]).astype(o.dtype)
            else:
                o[0] = val.astype(o.dtype)

    if has_gain:
        def normed():
            cols = []
            for c in range(acc.shape[1] // head_dim):
                blk = acc[:, c * head_dim:(c + 1) * head_dim]
                ms = jnp.mean(blk * blk, axis=-1, keepdims=True)
                cols.append(blk * lax.rsqrt(ms + EPS))
            return jnp.concatenate(cols, axis=-1) * gain_ref[...]

        if all_head_blocks:
            store(normed())
        else:
            @pl.when(j < n_head_blocks)
            def _():
                store(normed())

            @pl.when(j >= n_head_blocks)
            def _():
                store(acc)
    else:
        store(acc)


def _norm_matmul(x, g, w, *, out_dtypes, head_gain=None, n_head_cols=0, head_dim=128,
                 gate_w=None, gate_b=None, split_heads_first=False):
    lead_shape = x.shape[:2]
    x = x.reshape(1, -1, x.shape[-1])
    B, S, D = x.shape
    N = w.shape[1]
    tm = _tile(S, 1024)
    tn = _tile(N, 1024, 128)
    if head_gain is not None:
        tn = _tile(n_head_cols, tn, 128)
        assert N % tn == 0
    has_gain = head_gain is not None
    has_gate = gate_w is not None
    in_specs = [
        pl.BlockSpec((1, tm, D), lambda b, i, j: (b, i, 0)),
        pl.BlockSpec((1, D), lambda b, i, j: (0, 0)),
        pl.BlockSpec((D, tn), lambda b, i, j: (0, j)),
    ]
    args = [x, g.reshape(1, D).astype(F32), w]
    if has_gain:
        in_specs.append(pl.BlockSpec((1, tn), lambda b, i, j: (0, j)))
        args.append(head_gain.astype(F32))
    if has_gate:
        G = gate_w.shape[1]
        in_specs += [pl.BlockSpec((D, G), lambda b, i, j: (0, 0)),
                     pl.BlockSpec((1, G), lambda b, i, j: (0, 0))]
        args += [gate_w, gate_b.reshape(1, G).astype(F32)]
    out_shape = [jax.ShapeDtypeStruct((B, S, N), dt) for dt in out_dtypes]
    out_specs = [pl.BlockSpec((1, tm, tn), lambda b, i, j: (b, i, j)) for _ in out_dtypes]
    if split_heads_first:
        out_shape[0] = jax.ShapeDtypeStruct((B, S, N // head_dim, head_dim), out_dtypes[0])
        out_specs[0] = pl.BlockSpec((1, tm, tn // head_dim, head_dim), lambda b, i, j: (b, i, j, 0))
    if has_gate:
        out_shape.append(jax.ShapeDtypeStruct((B, S, G), F32))
        out_specs.append(pl.BlockSpec((1, tm, G), lambda b, i, j: (b, i, 0)))
    kern = functools.partial(
        _norm_matmul_kernel, n_out=len(out_dtypes), n_head_blocks=n_head_cols // tn,
        all_head_blocks=n_head_cols == N, head_dim=head_dim, has_gain=has_gain, has_gate=has_gate)
    outs = pl.pallas_call(
        kern, grid=(B, S // tm, N // tn), in_specs=in_specs, out_specs=out_specs,
        out_shape=out_shape, scratch_shapes=[pltpu.VMEM((tm, D), BF16)],
        compiler_params=_params("parallel", "parallel", "arbitrary"), name="norm_matmul",
    )(*args)
    return [o.reshape(lead_shape + o.shape[2:]) for o in outs]


def _matmul_residual_kernel(a_ref, w_ref, x_ref, o_ref):
    o_ref[0] = x_ref[0] + _dot(a_ref[0], w_ref[...])


def _matmul_residual(a, w, x):
    out_shape = x.shape
    a = a.reshape(1, -1, a.shape[-1])
    x = x.reshape(1, -1, x.shape[-1])
    B, S, K = a.shape
    N = w.shape[1]
    tm = _tile(S, 1024)
    tn = _tile(N, 1024, 128)
    return pl.pallas_call(
        _matmul_residual_kernel, grid=(B, S // tm, N // tn),
        in_specs=[pl.BlockSpec((1, tm, K), lambda b, i, j: (b, i, 0)),
                  pl.BlockSpec((K, tn), lambda b, i, j: (0, j)),
                  pl.BlockSpec((1, tm, tn), lambda b, i, j: (b, i, j))],
        out_specs=pl.BlockSpec((1, tm, tn), lambda b, i, j: (b, i, j)),
        out_shape=jax.ShapeDtypeStruct((B, S, N), F32),
        compiler_params=_params("parallel", "parallel", "arbitrary"), name="matmul_residual",
    )(a, w, x).reshape(out_shape)


def _fold_gate_kernel(wq_ref, wk_ref, wv_ref, wg_ref, a_ref, b_ref, *, k_scale):
    def hdot(a, b):
        return jnp.dot(a, b, precision=HIGHEST, preferred_element_type=F32)

    a_ref[0, 0] = hdot(wq_ref[0, 0], wg_ref[0, 0, 0]) + hdot(wk_ref[0, 0], wg_ref[0, 1, 0]) * k_scale
    b_ref[0, 0] = hdot(wv_ref[0, 0], wg_ref[0, 2, 0])


def _fold_gate_weights(wq, wk, wv, w_gate):
    NA, H, DH, _ = wq.shape
    G = w_gate.shape[-1]
    w_spec = pl.BlockSpec((1, 1, DH, DH), lambda i, h: (i, h, 0, 0))
    o_spec = pl.BlockSpec((1, 1, DH, G), lambda i, h: (i, h, 0, 0))
    out = jax.ShapeDtypeStruct((NA, H, DH, G), F32)
    return pl.pallas_call(
        functools.partial(_fold_gate_kernel, k_scale=float(DH) ** -0.5), grid=(NA, H),
        in_specs=[w_spec, w_spec, w_spec,
                  pl.BlockSpec((1, 3, 1, DH, G), lambda i, h: (i, 0, h, 0, 0))],
        out_specs=[o_spec, o_spec], out_shape=[out, out],
        compiler_params=_params("parallel", "parallel"), name="fold_gate_weights",
    )(wq, wk, wv, w_gate.reshape(NA, 3, H, DH, G))


def _conv_gates_kernel(xm_ref, buf_ref, cw_ref, cb_ref, wa_ref, wb_ref, bg_ref,
                       xc_ref, g_ref, seq_ref, *, rows, cum_rows, conv_w, n_heads):
    h = pl.program_id(1)
    S = xm_ref.shape[1]
    seq_ref[0:CONV_PAD, :] = buf_ref[0]
    seq_ref[CONV_PAD:CONV_PAD + S, :] = xm_ref[0].astype(F32)
    for r in range(S // rows):
        base = r * rows
        acc = jnp.broadcast_to(cb_ref[0], (rows, cb_ref.shape[-1]))
        for i in range(conv_w):
            start = base + CONV_PAD - (conv_w - 1) + i
            acc = acc + seq_ref[start:start + rows, :] * cw_ref[0, i:i + 1, :]
        xc = _silu(acc).astype(BF16)
        xc_ref[0, base:base + rows, :] = xc
        part = _dot(xc, wa_ref[0]) + _dot(xm_ref[0, base:base + rows, :], wb_ref[0])

        @pl.when(h == 0)
        def _():
            g_ref[0, base:base + rows, :] = part + bg_ref[...]

        @pl.when(h > 0)
        def _():
            g_ref[0, base:base + rows, :] = g_ref[0, base:base + rows, :] + part

    @pl.when(h == n_heads - 1)
    def _():
        r = lax.broadcasted_iota(jnp.int32, (cum_rows, cum_rows), 0)
        c = lax.broadcasted_iota(jnp.int32, (cum_rows, cum_rows), 1)
        trilf = (c <= r).astype(F32)
        lane = lax.broadcasted_iota(jnp.int32, (cum_rows, g_ref.shape[-1]), 1)
        carry = jnp.zeros((1, g_ref.shape[-1]), F32)
        for i in range(S // cum_rows):
            g = g_ref[0, i * cum_rows:(i + 1) * cum_rows, :]
            lf = jnp.where(lane >= n_heads, _log_sigmoid(g), 0.0)
            cum = jnp.dot(trilf, lf, precision=HIGHEST, preferred_element_type=F32) + carry
            g_ref[0, i * cum_rows:(i + 1) * cum_rows, :] = jnp.where(lane >= n_heads, cum, g)
            carry = cum[cum_rows - 1:cum_rows, :]


def _conv_gates(xz, buf, conv_w, conv_b, w_xc, w_xm, b_gate):
    B, S, _ = xz.shape
    H, DH, G = w_xc.shape
    AI = H * DH
    CW = conv_w.shape[0]
    rows = _tile(S, 768)
    act_spec = pl.BlockSpec((1, S, DH), lambda b, h: (b, 0, h))
    w_spec = pl.BlockSpec((1, DH, G), lambda b, h: (h, 0, 0))
    kern = functools.partial(_conv_gates_kernel, rows=rows, cum_rows=_tile(S, 512, 8), conv_w=CW,
                             n_heads=H)
    return pl.pallas_call(
        kern, grid=(B, H),
        in_specs=[act_spec,
                  pl.BlockSpec((1, CONV_PAD, DH), lambda b, h: (b, 0, h)),
                  pl.BlockSpec((1, CW, DH), lambda b, h: (h, 0, 0)),
                  pl.BlockSpec((1, 1, DH), lambda b, h: (h, 0, 0)),
                  w_spec, w_spec,
                  pl.BlockSpec((1, G), lambda b, h: (0, 0))],
        out_specs=[act_spec, pl.BlockSpec((1, S, G), lambda b, h: (b, 0, 0))],
        out_shape=[jax.ShapeDtypeStruct((B, S, AI), BF16), jax.ShapeDtypeStruct((B, S, G), F32)],
        scratch_shapes=[pltpu.VMEM((S + CONV_PAD, DH), F32)],
        compiler_params=_params("parallel", "arbitrary"), name="conv_gates",
    )(xz, buf,
      conv_w.reshape(CW, H, DH).transpose(1, 0, 2).astype(F32),
      conv_b.reshape(H, 1, DH).astype(F32), w_xc, w_xm, b_gate.reshape(1, G).astype(F32))


def _in_conv_kernel(x_ref, g_ref, w_ref, wz_ref, buf_ref, cw_ref, cb_ref, wa_ref, wb_ref, bg_ref,
                    xm_ref, z_ref, xc_ref, gate_ref, xn_ref, hist_ref, *, conv_w, n_heads):
    i = pl.program_id(1)
    j = pl.program_id(2)
    tm = x_ref.shape[1]

    @pl.when(j == 0)
    def _():
        x = x_ref[0].astype(F32)
        ms = jnp.mean(x * x, axis=-1, keepdims=True)
        xn_ref[...] = (x * lax.rsqrt(ms + EPS) * g_ref[...]).astype(BF16)

    prev = jnp.where(i == 0, buf_ref[0], hist_ref[j])
    row = lax.broadcasted_iota(jnp.int32, prev.shape, 0)
    mid = tm // 2 // BF16_ROWS * BF16_ROWS
    parts = []
    for r0, r1 in ((0, mid), (mid, tm)) if mid else ((0, tm),):
        acc = _dot(xn_ref[r0:r1, :], w_ref[...])
        xm = acc.astype(BF16)
        xm_ref[0, r0:r1, :] = xm
        y = cb_ref[...] + acc * cw_ref[conv_w - 1:conv_w, :]
        for shift in range(1, conv_w):
            rolled = pltpu.roll(acc, shift, 0)
            head = jnp.where(row < shift, pltpu.roll(prev, shift, 0), rolled[0:CONV_PAD, :])
            shifted = jnp.concatenate([head, rolled[CONV_PAD:, :]], axis=0)
            y = y + shifted * cw_ref[conv_w - 1 - shift:conv_w - shift, :]
        xc = _silu(y).astype(BF16)
        xc_ref[0, r0:r1, :] = xc
        z_ref[0, r0:r1, :] = _dot(xn_ref[r0:r1, :], wz_ref[...]).astype(BF16)
        parts.append(_dot(xc, wa_ref[...]) + _dot(xm, wb_ref[...]))
        prev = acc[r1 - r0 - CONV_PAD:r1 - r0, :]
    hist_ref[j] = prev
    part = jnp.concatenate(parts, axis=0)

    @pl.when(j == 0)
    def _():
        gate_ref[0] = part + bg_ref[...]

    @pl.when(j > 0)
    def _():
        gate_ref[0] = gate_ref[0] + part

    @pl.when(j == pl.num_programs(2) - 1)
    def _():
        g = gate_ref[0]
        lane = lax.broadcasted_iota(jnp.int32, g.shape, 1)
        gate_ref[0] = jnp.where(lane >= n_heads, _log_sigmoid(g), g)


def _in_conv(x, norm_g, w_xm_in, w_z_in, buf, conv_w, conv_b, w_xc, w_xm, b_gate):
    B, S, D = x.shape
    H, DH, G = w_xc.shape
    AI = H * DH
    CW = conv_w.shape[0]
    tm = _tile(S, 1024)
    tn = _tile(AI, 1024, LANES)
    act_spec = pl.BlockSpec((1, tm, tn), lambda b, i, j: (b, i, j))
    act = jax.ShapeDtypeStruct((B, S, AI), BF16)
    kern = functools.partial(_in_conv_kernel, conv_w=CW, n_heads=H)
    return pl.pallas_call(
        kern, grid=(B, S // tm, AI // tn),
        in_specs=[pl.BlockSpec((1, tm, D), lambda b, i, j: (b, i, 0)),
                  pl.BlockSpec((1, D), lambda b, i, j: (0, 0)),
                  pl.BlockSpec((D, tn), lambda b, i, j: (0, j)),
                  pl.BlockSpec((D, tn), lambda b, i, j: (0, j)),
                  pl.BlockSpec((1, CONV_PAD, tn), lambda b, i, j: (b, 0, j)),
                  pl.BlockSpec((CW, tn), lambda b, i, j: (0, j)),
                  pl.BlockSpec((1, tn), lambda b, i, j: (0, j)),
                  pl.BlockSpec((tn, G), lambda b, i, j: (j, 0)),
                  pl.BlockSpec((tn, G), lambda b, i, j: (j, 0)),
                  pl.BlockSpec((1, G), lambda b, i, j: (0, 0))],
        out_specs=[act_spec, act_spec, act_spec,
                   pl.BlockSpec((1, tm, G), lambda b, i, j: (b, i, 0))],
        out_shape=[act, act, act, jax.ShapeDtypeStruct((B, S, G), F32)],
        scratch_shapes=[pltpu.VMEM((tm, D), BF16), pltpu.VMEM((AI // tn, CONV_PAD, tn), F32)],
        compiler_params=_params("parallel", "arbitrary", "arbitrary"), name="in_conv",
    )(x, norm_g.reshape(1, D).astype(F32), w_xm_in, w_z_in, buf, conv_w.astype(F32),
      conv_b.reshape(1, AI).astype(F32), w_xc.reshape(AI, G), w_xm.reshape(AI, G),
      b_gate.reshape(1, G).astype(F32))


def _mlstm_project(xm, xc, wq, wk, wv):
    return (_dot(xc, wq).astype(BF16), _dot(xc, wk).astype(BF16), _dot(xm, wv).astype(BF16))


def _mlstm_chunk(q, k, v, xc, z, g_rows, g_cols, head, n_heads, og, skip,
                 C_ref, n_ref, m_ref, b_ref):
    L = q.shape[0]
    row = lax.broadcasted_iota(jnp.int32, (L, L), 0)
    col = lax.broadcasted_iota(jnp.int32, (L, L), 1)
    tril = col <= row
    lane = lax.broadcasted_iota(jnp.int32, g_cols.shape, 1)
    ig_col = jnp.sum(jnp.where(lane == head, g_cols, 0.0), axis=-1, keepdims=True)
    f_col = jnp.sum(jnp.where(lane == head + n_heads, g_cols, 0.0), axis=-1, keepdims=True)
    ig_row = g_rows[0:1, :]
    f_row = g_rows[1:2, :]
    m_prev = m_ref[...]
    f_prev = b_ref[...]
    log_d = jnp.where(tril, f_col - f_row + ig_row, NEG_INF)
    inter = f_col - f_prev + m_prev
    m_t = jnp.maximum(jnp.max(log_d, axis=-1, keepdims=True), inter)
    d = jnp.exp(log_d - m_t)
    a = jnp.exp(inter - m_t)
    s = _dot_nt(q, k) * d
    C = C_ref[...]
    n = n_ref[...]
    num = _dot(s.astype(BF16), v) + a * _dot(q, C.astype(BF16))
    qn = _dot_nt(q, jnp.broadcast_to(n, (8, n.shape[1])).astype(BF16))[:, 0:1]
    den = jnp.sum(s, axis=-1, keepdims=True) + a * qn
    hid = num * (1.0 / jnp.maximum(jnp.abs(den), jnp.exp(-m_t)))
    m_new = m_t[L - 1:L, :]
    f_last = f_row[:, L - 1:L]
    decay = jnp.exp(f_last - f_prev + m_prev - m_new)
    w_col = jnp.exp(f_last - f_col + ig_col - m_new)
    C_ref[...] = decay * C + _dot_tn(k, (w_col * v.astype(F32)).astype(BF16))
    n_ref[...] = decay * n + jnp.sum(w_col * k.astype(F32), axis=0, keepdims=True)
    m_ref[...] = m_new
    b_ref[...] = f_last
    hc = hid - jnp.mean(hid, axis=-1, keepdims=True)
    hn = hc * lax.rsqrt(jnp.mean(hc * hc, axis=-1, keepdims=True) + EPS) * og
    return ((hn + skip * xc.astype(F32)) * _silu(z.astype(F32))).astype(BF16)


def _mlstm_kernel(*refs, lead, chunk, n_chunks, n_heads, hps, has_state):
    xm_ref, xc_ref, z_ref, wq_ref, wk_ref, wv_ref = refs[:6]
    pos = 6
    if lead:
        gl_ref = refs[pos]
        pos += 1
    gm_ref, gc_ref, og_ref, skip_ref = refs[pos:pos + 4]
    pos += 4
    if has_state:
        c0_ref, n0_ref, m0_ref = refs[pos:pos + 3]
        pos += 3
    pos += 1
    o_ref, C_ref, n_ref, m_ref, b_ref = refs[pos:pos + 5]
    DH = wq_ref.shape[-1]
    for hh in range(hps):
        if has_state:
            C_ref[0, 0, hh] = c0_ref[0, 0, hh]
            n_ref[0, hh] = n0_ref[0, 0, hh]
            m_ref[0, hh] = m0_ref[0, 0, hh]
        else:
            C_ref[0, 0, hh] = jnp.zeros((DH, DH), F32)
            n_ref[0, hh] = jnp.zeros((1, DH), F32)
            m_ref[0, hh] = jnp.zeros((1, 1), F32)
    b_ref[...] = jnp.zeros(b_ref.shape, F32)
    head0 = pl.program_id(1) * hps

    def chunk_rows(c):
        return pl.ds(pl.multiple_of(lead + c * chunk, BF16_ROWS), chunk)

    def project(rows, hh):
        cs = slice(hh * DH, (hh + 1) * DH)
        return _mlstm_project(xm_ref[0, rows, cs], xc_ref[0, rows, cs], wq_ref[hh], wk_ref[hh],
                              wv_ref[hh])

    def recur(rows, hh, qkv, g_rows):
        cs = slice(hh * DH, (hh + 1) * DH)
        o_ref[0, rows, cs] = _mlstm_chunk(
            *qkv, xc_ref[0, rows, cs], z_ref[0, rows, cs], g_rows, gc_ref[0, rows, :],
            head0 + hh, n_heads, og_ref[hh], skip_ref[hh], C_ref.at[0, 0, hh], n_ref.at[0, hh],
            m_ref.at[0, hh], b_ref.at[hh])

    for hh in range(hps):
        if lead:
            recur(pl.ds(0, lead), hh, project(pl.ds(0, lead), hh), gl_ref[0, hh])

    def body(c, carry):
        for hh in range(hps):
            recur(chunk_rows(c), hh, project(chunk_rows(c), hh), gm_ref[0, hh, c])
        return carry

    lax.fori_loop(0, n_chunks, body, 0, unroll=_tile(n_chunks, 2, 1))


def _mlstm(xm, z, z_head0, xc, gates, wq, wk, wv, out_g, skip, state, lead, chunk, c_stack, layer,
           n_layers, heads_per_step=1):
    B, S, AI = xc.shape
    H, DH = out_g.shape
    n_chunks = (S - lead) // chunk
    assert lead + n_chunks * chunk == S
    g = gates.reshape(B, S, 2, H).transpose(0, 3, 2, 1)
    g_main = g[..., lead:].reshape(B, H, 2, n_chunks, chunk).transpose(0, 1, 3, 2, 4)
    hps = heads_per_step if H % heads_per_step == 0 and z_head0 % heads_per_step == 0 else 1
    act_spec = pl.BlockSpec((1, S, hps * DH), lambda b, h: (b, 0, h))
    w_spec = pl.BlockSpec((hps, DH, DH), lambda b, h: (h, 0, 0))
    in_specs = [act_spec, act_spec,
                pl.BlockSpec((1, S, hps * DH), lambda b, h: (b, 0, z_head0 // hps + h)),
                w_spec, w_spec, w_spec]
    args = [xm, xc, z, wq, wk, wv]
    if lead:
        in_specs.append(pl.BlockSpec((1, hps, 2, lead), lambda b, h: (b, h, 0, 0)))
        args.append(g[..., :lead])
    in_specs += [pl.BlockSpec((1, hps, n_chunks, 2, chunk), lambda b, h: (b, h, 0, 0, 0)),
                 pl.BlockSpec((1, S, 2 * H), lambda b, h: (b, 0, 0)),
                 pl.BlockSpec((hps, 1, DH), lambda b, h: (h, 0, 0)),
                 pl.BlockSpec((hps, 1, DH), lambda b, h: (h, 0, 0))]
    args += [g_main, gates, out_g.reshape(H, 1, DH).astype(F32), skip.reshape(H, 1, DH).astype(F32)]
    n_spec = pl.BlockSpec((1, hps, 1, DH), lambda b, h: (b, h, 0, 0))
    m_spec = pl.BlockSpec((1, hps, 1, 1), lambda b, h: (b, h, 0, 0))
    has_state = state is not None
    if has_state:
        C0, n0, m0 = state
        NL = C0.shape[0]
        in_specs += [pl.BlockSpec((1, 1, hps, DH, DH), lambda b, h: (layer, b, h, 0, 0)),
                     pl.BlockSpec((1, 1, hps, 1, DH), lambda b, h: (layer, b, h, 0, 0)),
                     pl.BlockSpec((1, 1, hps, 1, 1), lambda b, h: (layer, b, h, 0, 0))]
        args += [C0.astype(F32), n0.reshape(NL, B, H, 1, DH).astype(F32),
                 m0.reshape(NL, B, H, 1, 1).astype(F32)]
    if c_stack is None:
        c_stack = jnp.zeros((n_layers, B, H, DH, DH), F32)
    aliases = {len(args): 1}
    in_specs.append(pl.BlockSpec(memory_space=pl.ANY))
    args.append(c_stack)
    kern = functools.partial(_mlstm_kernel, lead=lead, chunk=chunk, n_chunks=n_chunks, n_heads=H,
                             hps=hps, has_state=has_state)
    hz, C, n, m = pl.pallas_call(
        kern, grid=(B, H // hps), in_specs=in_specs,
        out_specs=[act_spec,
                   pl.BlockSpec((1, 1, hps, DH, DH), lambda b, h: (layer, b, h, 0, 0)),
                   n_spec, m_spec],
        out_shape=[jax.ShapeDtypeStruct((B, S, AI), BF16),
                   jax.ShapeDtypeStruct((n_layers, B, H, DH, DH), F32),
                   jax.ShapeDtypeStruct((B, H, 1, DH), F32),
                   jax.ShapeDtypeStruct((B, H, 1, 1), F32)],
        scratch_shapes=[pltpu.VMEM((hps, 1, 1), F32)],
        input_output_aliases=aliases,
        compiler_params=_params("parallel", "parallel"), name="mlstm",
    )(*args)
    return hz, C, n.reshape(B, H, DH), m.reshape(B, H)


def _cumsum_kernel(x_ref, o_ref, *, scale, first_lane):
    x = x_ref[0]
    S = x.shape[0]
    summed = lax.broadcasted_iota(jnp.int32, x.shape, 1) >= first_lane
    row = lax.broadcasted_iota(jnp.int32, x.shape, 0)
    y = jnp.where(summed, x, 0.0)
    step = 1
    while step < S:
        y = y + jnp.where(row >= step, pltpu.roll(y, step, 0), 0.0)
        step *= 2
    o_ref[0] = jnp.where(summed, y * scale, x)


def _cumsum(x, scale=1.0, first_lane=0):
    B, S, G = x.shape
    spec = pl.BlockSpec((1, S, G), lambda b: (b, 0, 0))
    return pl.pallas_call(
        functools.partial(_cumsum_kernel, scale=scale, first_lane=first_lane),
        grid=(B,), in_specs=[spec], out_specs=spec, out_shape=jax.ShapeDtypeStruct((B, S, G), F32),
        compiler_params=_params("parallel"), name="cumsum",
    )(x)


def _lane_fold(op, acc, x):
    for c in range(x.shape[1] // LANES):
        acc = op(acc, x[:, c * LANES:(c + 1) * LANES])
    return acc


def _fox_prompt_kernel(q_ref, z_ref, k_ref, v_ref, fq_ref, fkl_ref, fkm_ref, o_ref,
                       s_ref, sl_ref, fq_rep, m_ref, l_ref, acc_ref, *, lead, blk, n_blk, heads,
                       dim):
    n_lane = blk // LANES
    cols = [slice(hh * dim, (hh + 1) * dim) for hh in range(heads)]

    def key_rows(j):
        return pl.ds(pl.multiple_of(lead + j * blk, BF16_ROWS), blk)

    def tiled(x):
        return jnp.concatenate([x] * n_lane, axis=1)

    def q_block(i, carry):
        qrows = pl.ds(pl.multiple_of(i * blk, blk), blk)

        def logits(hh, j):
            return (_dot_nt(q_ref[0, qrows, cols[hh]], k_ref[0, key_rows(j), cols[hh]])
                    + tiled(fq_rep[hh]) - fkm_ref[0, 0, j, hh:hh + 1, :])

        for hh in range(heads):
            fq_rep[hh] = jnp.broadcast_to(fq_ref[0, 0, qrows, hh:hh + 1], (blk, LANES))
        if lead:
            for hh in range(heads):
                s = (_dot_nt(q_ref[0, qrows, cols[hh]], k_ref[0, 0:LANES, cols[hh]])
                     + fq_rep[hh] - fkl_ref[0, 0, hh:hh + 1, :])
                sl_ref[hh] = s
                m_ref[hh] = s
        else:
            m_ref[...] = jnp.full(m_ref.shape, NEG_INF, F32)

        def pass1(j, c):
            for hh in range(heads):
                s = logits(hh, j)
                s_ref[hh, j] = s
                m_ref[hh] = _lane_fold(jnp.maximum, m_ref[hh], s)
            return c

        lax.fori_loop(0, i, pass1, 0)
        row = lax.broadcasted_iota(jnp.int32, (blk, blk), 0)
        col = lax.broadcasted_iota(jnp.int32, (blk, blk), 1)
        for hh in range(heads):
            s = jnp.where(col <= row, logits(hh, i), NEG_INF)
            s_ref[hh, i] = s
            m = jnp.max(_lane_fold(jnp.maximum, m_ref[hh], s), axis=-1, keepdims=True)
            m_ref[hh] = jnp.broadcast_to(m, (blk, LANES))
        if lead:
            for hh in range(heads):
                p = jnp.exp2(sl_ref[hh] - m_ref[hh])
                l_ref[hh] = p
                acc_ref[hh] = _dot(p.astype(BF16), v_ref[0, 0:LANES, cols[hh]])
        else:
            l_ref[...] = jnp.zeros(l_ref.shape, F32)
            acc_ref[...] = jnp.zeros(acc_ref.shape, F32)

        def pass2(j, c):
            for hh in range(heads):
                p = jnp.exp2(s_ref[hh, j] - tiled(m_ref[hh]))
                l_ref[hh] = _lane_fold(jnp.add, l_ref[hh], p)
                acc_ref[hh] += _dot(p.astype(BF16), v_ref[0, key_rows(j), cols[hh]])
            return c

        lax.fori_loop(0, i + 1, pass2, 0)
        for hh in range(heads):
            l = jnp.sum(l_ref[hh], axis=-1, keepdims=True)
            o_ref[0, qrows, cols[hh]] = (acc_ref[hh] * (1.0 / l)
                                         * _silu(z_ref[0, qrows, cols[hh]].astype(F32))).astype(BF16)
        return carry

    lax.fori_loop(0, n_blk, q_block, 0)


def _fox_prompt(qz, kb, vb, f_cum, lead, heads_per_step=4, blk=512):
    B, SQ, W2 = qz.shape
    W = W2 // 2
    NH = f_cum.shape[-1]
    dim = W // NH
    hg = min(heads_per_step, NH)
    G = NH // hg
    blk = _tile(SQ, blk, LANES)
    n_blk = SQ // blk
    SK = lead + SQ
    assert lead <= LANES <= SK
    fq = f_cum[:, lead:].reshape(B, SQ, G, hg).transpose(0, 2, 1, 3)
    fk = f_cum.transpose(0, 2, 1).reshape(B, G, hg, SK)
    fk_lead = jnp.pad(fk[..., :lead], ((0, 0), (0, 0), (0, 0), (0, LANES - lead)),
                      constant_values=float("inf"))
    fk_main = fk[..., lead:].reshape(B, G, hg, n_blk, blk).transpose(0, 1, 3, 2, 4)
    wcols = hg * dim
    kern = functools.partial(_fox_prompt_kernel, lead=lead, blk=blk, n_blk=n_blk, heads=hg, dim=dim)
    return pl.pallas_call(
        kern, grid=(B, G),
        in_specs=[pl.BlockSpec((1, SQ, wcols), lambda b, g: (b, 0, g)),
                  pl.BlockSpec((1, SQ, wcols), lambda b, g: (b, 0, G + g)),
                  pl.BlockSpec((1, SK, wcols), lambda b, g: (b, 0, g)),
                  pl.BlockSpec((1, SK, wcols), lambda b, g: (b, 0, g)),
                  pl.BlockSpec((1, 1, SQ, hg), lambda b, g: (b, g, 0, 0)),
                  pl.BlockSpec((1, 1, hg, LANES), lambda b, g: (b, g, 0, 0)),
                  pl.BlockSpec((1, 1, n_blk, hg, blk), lambda b, g: (b, g, 0, 0, 0))],
        out_specs=pl.BlockSpec((1, SQ, wcols), lambda b, g: (b, 0, g)),
        out_shape=jax.ShapeDtypeStruct((B, SQ, W), BF16),
        scratch_shapes=[pltpu.VMEM((hg, n_blk, blk, blk), F32)]
        + [pltpu.VMEM((hg, blk, LANES), F32)] * 4 + [pltpu.VMEM((hg, blk, dim), F32)],
        compiler_params=_params("parallel", "parallel"), name="fox_prompt",
    )(qz, qz, kb, vb, fq, fk_lead, fk_main)


def _pack_heads_kernel(x_ref, o_ref):
    o_ref[0] = pltpu.einshape("khd->k(hd)", x_ref[0]).astype(o_ref.dtype)


def _pack_heads(x, dtype):
    B, P, NH, dim = x.shape
    rows = _tile(P, 512)
    return pl.pallas_call(
        _pack_heads_kernel, grid=(B, P // rows),
        in_specs=[pl.BlockSpec((1, rows, NH, dim), lambda b, i: (b, i, 0, 0))],
        out_specs=pl.BlockSpec((1, rows, NH * dim), lambda b, i: (b, i, 0)),
        out_shape=jax.ShapeDtypeStruct((B, P, NH * dim), dtype),
        compiler_params=_params("parallel", "parallel"), name="pack_heads",
    )(x)


def _fox_decode_kernel(q_ref, z_ref, ck_ref, cv_ref, k_ref, v_ref, fq_ref, fkc_ref, fkn_ref, o_ref,
                       *, heads, dim):
    Q = q_ref.shape[1]
    row = lax.broadcasted_iota(jnp.int32, (Q, Q), 0)
    col = lax.broadcasted_iota(jnp.int32, (Q, Q), 1)
    outs = []
    for hh in range(heads):
        cs = slice(hh * dim, (hh + 1) * dim)
        q = q_ref[0, :, cs]
        fq = fq_ref[0, 0, :, hh:hh + 1]
        s_c = _dot_nt(q, ck_ref[0, :, cs].astype(BF16)) + fq - fkc_ref[0, 0, hh:hh + 1, :]
        s_n = jnp.where(col <= row, _dot_nt(q, k_ref[0, :, cs]) + fq - fkn_ref[0, 0, hh:hh + 1, :],
                        NEG_INF)
        m = jnp.maximum(jnp.max(s_c, axis=-1, keepdims=True), jnp.max(s_n, axis=-1, keepdims=True))
        p_c = jnp.exp2(s_c - m)
        p_n = jnp.exp2(s_n - m)
        l = jnp.sum(p_c, axis=-1, keepdims=True) + jnp.sum(p_n, axis=-1, keepdims=True)
        acc = _dot(p_c.astype(BF16), cv_ref[0, :, cs].astype(BF16)) + _dot(p_n.astype(BF16), v_ref[0, :, cs])
        outs.append((acc * (1.0 / l) * _silu(z_ref[0, :, cs].astype(F32))).astype(BF16))
    o_ref[0] = jnp.concatenate(outs, axis=-1)


def _fox_decode(qz, cache_k, cache_v, kb, vb, f_cum, heads_per_step=4):
    B, Q, W2 = qz.shape
    W = W2 // 2
    P = cache_k.shape[1]
    NH = f_cum.shape[-1]
    dim = W // NH
    hg = min(heads_per_step, NH)
    G = NH // hg
    fq = f_cum[:, P:].reshape(B, Q, G, hg).transpose(0, 2, 1, 3)
    fk = f_cum.transpose(0, 2, 1).reshape(B, G, hg, P + Q)
    wcols = hg * dim
    kern = functools.partial(_fox_decode_kernel, heads=hg, dim=dim)
    return pl.pallas_call(
        kern, grid=(B, G),
        in_specs=[pl.BlockSpec((1, Q, wcols), lambda b, g: (b, 0, g)),
                  pl.BlockSpec((1, Q, wcols), lambda b, g: (b, 0, G + g)),
                  pl.BlockSpec((1, P, wcols), lambda b, g: (b, 0, g)),
                  pl.BlockSpec((1, P, wcols), lambda b, g: (b, 0, g)),
                  pl.BlockSpec((1, Q, wcols), lambda b, g: (b, 0, g)),
                  pl.BlockSpec((1, Q, wcols), lambda b, g: (b, 0, g)),
                  pl.BlockSpec((1, 1, Q, hg), lambda b, g: (b, g, 0, 0)),
                  pl.BlockSpec((1, 1, hg, P), lambda b, g: (b, g, 0, 0)),
                  pl.BlockSpec((1, 1, hg, Q), lambda b, g: (b, g, 0, 0))],
        out_specs=pl.BlockSpec((1, Q, wcols), lambda b, g: (b, 0, g)),
        out_shape=jax.ShapeDtypeStruct((B, Q, W), BF16),
        compiler_params=_params("parallel", "parallel"), name="fox_decode",
    )(qz, qz, cache_k, cache_v, kb, vb, fq, fk[..., :P], fk[..., P:])


def _mlstm_layer(x, conv_hist, state, w, lead, chunk, c_stack, layer, n_layers):
    (norm_g, w_in, conv_w, conv_b, wq, wk, wv, w_xc, w_xm, b_gate, out_g, skip, w_out) = w
    B, S, _ = x.shape
    H, DH, _ = wq.shape
    AI = H * DH
    CW = conv_w.shape[0]
    assert S >= CW - 1 and CW - 1 <= CONV_PAD
    buf = jnp.zeros((B, CONV_PAD, AI), F32)
    if conv_hist is not None:
        buf = buf.at[:, CONV_PAD - (CW - 1):].set(conv_hist.astype(F32))
    if S >= MIN_FUSED_CONV_ROWS:
        wb = w_in.astype(BF16)
        xm, z, xc, gates = _in_conv(x, norm_g, wb[:, :AI], wb[:, AI:], buf, conv_w, conv_b,
                                    w_xc.astype(BF16), w_xm.astype(BF16), b_gate)
        gates = _cumsum(gates, first_lane=H)
        z_head0 = 0
    else:
        (xm,) = _norm_matmul(x, norm_g, w_in.astype(BF16), out_dtypes=[BF16])
        xc, gates = _conv_gates(xm, buf, conv_w, conv_b, w_xc.astype(BF16), w_xm.astype(BF16),
                                b_gate)
        z, z_head0 = xm, H
    hz, c_stack, n, m = _mlstm(
        xm, z, z_head0, xc, gates, wq.astype(BF16), (wk * float(DH) ** -0.5).astype(BF16),
        wv.astype(BF16), out_g, skip.reshape(H, DH), state, lead, chunk, c_stack, layer, n_layers)
    x_new = _matmul_residual(hz, w_out.astype(BF16), x)
    new_hist = xm[:, S - (CW - 1):, :AI].astype(F32)
    return x_new, new_hist, c_stack, n, m


def _shared_kv(x, kv_norm_g, w_kvf, b_f, k_norm_g):
    NH, dim = k_norm_g.shape
    W = NH * dim
    wb = w_kvf.astype(BF16)
    k32, kb, logf = _norm_matmul(
        x, kv_norm_g, wb[:, :W], out_dtypes=[F32, BF16], head_gain=k_norm_g.reshape(1, W),
        n_head_cols=W, head_dim=dim, gate_w=wb[:, 2 * W:], gate_b=b_f, split_heads_first=True)
    v32, vb = _norm_matmul(x, kv_norm_g, wb[:, W:2 * W], out_dtypes=[F32, BF16], head_dim=dim,
                           split_heads_first=True)
    return k32, kb, v32, vb, logf


def _fox_in(x, norm_g, w_in, q_norm_g):
    NH, dim = q_norm_g.shape
    W = NH * dim
    gain = jnp.concatenate([q_norm_g.reshape(1, W).astype(F32) * (float(dim) ** -0.5 * LOG2E),
                            jnp.ones((1, W), F32)], axis=-1)
    (qz,) = _norm_matmul(x, norm_g, w_in.astype(BF16), out_dtypes=[BF16], head_gain=gain,
                         n_head_cols=W, head_dim=dim)
    return qz


def kernel(x_prompt, x_sample, cache_k, cache_v, cache_logf, state_C, state_n, state_m, state_conv,
           meta_tokens, a_norm_g, a_w_in, a_conv_w, a_conv_b, a_wq, a_wk, a_wv, a_w_gate, a_b_gate,
           a_out_g, a_skip, a_w_out, kv_norm_g, w_kvf, b_f, k_norm_g, b_norm_g, b_w_in, q_norm_g,
           b_w_out):
    B, SEQ, D = x_prompt.shape
    DB, DEC, _ = x_sample.shape
    NM = meta_tokens.shape[0]
    N_A = a_norm_g.shape[0]
    N_B = b_norm_g.shape[0]
    NH, dim = k_norm_g.shape
    W = NH * dim
    P = cache_k.shape[1]
    chunk = _tile(SEQ, 256)

    xp = jnp.concatenate(
        [jnp.broadcast_to(meta_tokens.astype(x_prompt.dtype)[None], (B, NM, D)), x_prompt], axis=1)
    xs = x_sample
    p_state, s_state = [], []
    p_C = s_C = None
    w_xc, w_xm = _fold_gate_weights(a_wq, a_wk, a_wv, a_w_gate)
    for i in range(N_A):
        w = (a_norm_g[i], a_w_in[i], a_conv_w[i], a_conv_b[i], a_wq[i], a_wk[i], a_wv[i],
             w_xc[i], w_xm[i], a_b_gate[i], a_out_g[i], a_skip[i], a_w_out[i])
        xp, hist, p_C, n, m = _mlstm_layer(xp, None, None, w, NM, chunk, p_C, i, N_A)
        p_state.append((hist, n, m))
        xs, hist, s_C, n, m = _mlstm_layer(xs, state_conv[i], (state_C, state_n, state_m),
                                           w, 0, DEC, s_C, i, N_A)
        s_state.append((hist, n, m))

    pk32, pkb, pv32, pvb, p_logf = _shared_kv(xp, kv_norm_g, w_kvf, b_f, k_norm_g)
    sk32, skb, sv32, svb, s_logf = _shared_kv(xs, kv_norm_g, w_kvf, b_f, k_norm_g)
    fp = _cumsum(p_logf, LOG2E)
    fs = _cumsum(jnp.concatenate([cache_logf.astype(F32), s_logf], axis=1), LOG2E)
    ck = _pack_heads(cache_k, BF16)
    cv = _pack_heads(cache_v, BF16)
    xp = xp[:, NM:]
    for j in range(N_B):
        qz = _fox_in(xp, b_norm_g[j], b_w_in[j], q_norm_g[j])
        xp = _matmul_residual(_fox_prompt(qz, pkb, pvb, fp, NM), b_w_out[j].astype(BF16), xp)
        qz = _fox_in(xs, b_norm_g[j], b_w_in[j], q_norm_g[j])
        xs = _matmul_residual(_fox_decode(qz, ck, cv, skb, svb, fs), b_w_out[j].astype(BF16), xs)

    def stack(states, idx):
        return jnp.stack([st[idx] for st in states])

    return (xp, xs, pk32, pv32, p_logf,
            p_C, stack(p_state, 1), stack(p_state, 2), stack(p_state, 0),
            sk32, sv32, s_logf,
            s_C, stack(s_state, 1), stack(s_state, 2), stack(s_state, 0))
```

```python
import functools

import jax
import jax.numpy as jnp
from jax import lax
from jax.experimental import pallas as pl
from jax.experimental.pallas import tpu as pltpu

F32 = jnp.float32
BF16 = jnp.bfloat16
EPS = 1e-6
CONV_PAD = 8
BF16_ROWS = 16
LANES = 128
MIN_FUSED_CONV_ROWS = 256
VMEM_LIMIT_BYTES = 56 * 1024 * 1024
HIGHEST = lax.Precision.HIGHEST
NEG_INF = float("-inf")
LOG2E = 1.4426950408889634


def _params(*sem):
    return pltpu.CompilerParams(dimension_semantics=sem, vmem_limit_bytes=VMEM_LIMIT_BYTES)


def _tile(n, cap, mult=BF16_ROWS):
    best = None
    for t in range(mult, min(n, cap) + 1, mult):
        if n % t == 0:
            best = t
    return best if best is not None else n


def _dot(a, b):
    return jnp.dot(a, b, preferred_element_type=F32)


def _dot_nt(a, b):
    return lax.dot_general(a, b, (((1,), (1,)), ((), ())), preferred_element_type=F32)


def _dot_tn(a, b):
    return lax.dot_general(a, b, (((0,), (0,)), ((), ())), preferred_element_type=F32)


def _silu(x):
    return x / (1.0 + jnp.exp(-x))


def _log_sigmoid(x):
    return jnp.minimum(x, 0.0) - jnp.log1p(jnp.exp(-jnp.abs(x)))


def _norm_matmul_kernel(x_ref, g_ref, w_ref, o_ref, xn_ref):
    @pl.when(pl.program_id(2) == 0)
    def _():
        x = x_ref[0].astype(F32)
        ms = jnp.mean(x * x, axis=-1, keepdims=True)
        xn_ref[...] = (x * lax.rsqrt(ms + EPS) * g_ref[...]).astype(BF16)

    o_ref[0] = _dot(xn_ref[...], w_ref[...]).astype(o_ref.dtype)


def _norm_matmul(x, g, w):
    lead_shape = x.shape[:2]
    x = x.reshape(1, -1, x.shape[-1])
    B, S, D = x.shape
    N = w.shape[1]
    tm = _tile(S, 1024)
    tn = _tile(N, 1024, LANES)
    out = pl.pallas_call(
        _norm_matmul_kernel, grid=(B, S // tm, N // tn),
        in_specs=[pl.BlockSpec((1, tm, D), lambda b, i, j: (b, i, 0)),
                  pl.BlockSpec((1, D), lambda b, i, j: (0, 0)),
                  pl.BlockSpec((D, tn), lambda b, i, j: (0, j))],
        out_specs=pl.BlockSpec((1, tm, tn), lambda b, i, j: (b, i, j)),
        out_shape=jax.ShapeDtypeStruct((B, S, N), BF16),
        scratch_shapes=[pltpu.VMEM((tm, D), BF16)],
        compiler_params=_params("parallel", "parallel", "arbitrary"), name="norm_matmul",
    )(x, g.reshape(1, D).astype(F32), w)
    return out.reshape(lead_shape + (N,))


def _dual_norm_matmul_kernel(*refs, n_a, n_b, head_dim, has_gate):
    x_ref, g_ref, wa_ref, gain_ref, wb_ref = refs[:5]
    pos = 5
    if has_gate:
        gw_ref, gb_ref = refs[pos:pos + 2]
        pos += 2
    a_refs = refs[pos:pos + n_a]
    b_refs = refs[pos + n_a:pos + n_a + n_b]
    pos += n_a + n_b
    if has_gate:
        gate_out_ref = refs[pos]
        pos += 1
    xn_ref = refs[pos]

    @pl.when(pl.program_id(2) == 0)
    def _():
        x = x_ref[0].astype(F32)
        ms = jnp.mean(x * x, axis=-1, keepdims=True)
        xn_ref[...] = (x * lax.rsqrt(ms + EPS) * g_ref[...]).astype(BF16)
        if has_gate:
            gate_out_ref[0] = _log_sigmoid(_dot(xn_ref[...], gw_ref[...]) + gb_ref[...])

    def store(out_refs, val):
        for o in out_refs:
            if len(o.shape) == 4:
                o[0] = val.reshape(o.shape[1:]).astype(o.dtype)
            else:
                o[0] = val.astype(o.dtype)

    acc = _dot(xn_ref[...], wa_ref[...])
    cols = []
    for c in range(acc.shape[1] // head_dim):
        blk = acc[:, c * head_dim:(c + 1) * head_dim]
        ms = jnp.mean(blk * blk, axis=-1, keepdims=True)
        cols.append(blk * lax.rsqrt(ms + EPS))
    store(a_refs, jnp.concatenate(cols, axis=-1) * gain_ref[...])
    store(b_refs, _dot(xn_ref[...], wb_ref[...]))


def _dual_norm_matmul(x, g, w_a, gain_a, w_b, *, a_dtypes, b_dtypes, head_dim, split_heads_first=False,
                      gate_w=None, gate_b=None):
    lead_shape = x.shape[:2]
    x = x.reshape(1, -1, x.shape[-1])
    B, S, D = x.shape
    N = w_a.shape[1]
    assert w_b.shape[1] == N
    tm = _tile(S, 1024)
    tn = _tile(N, 1024, LANES)
    has_gate = gate_w is not None
    w_spec = pl.BlockSpec((D, tn), lambda b, i, j: (0, j))
    in_specs = [pl.BlockSpec((1, tm, D), lambda b, i, j: (b, i, 0)),
                pl.BlockSpec((1, D), lambda b, i, j: (0, 0)),
                w_spec, pl.BlockSpec((1, tn), lambda b, i, j: (0, j)), w_spec]
    args = [x, g.reshape(1, D).astype(F32), w_a, gain_a.astype(F32), w_b]
    if has_gate:
        G = gate_w.shape[1]
        in_specs += [pl.BlockSpec((D, G), lambda b, i, j: (0, 0)),
                     pl.BlockSpec((1, G), lambda b, i, j: (0, 0))]
        args += [gate_w, gate_b.reshape(1, G).astype(F32)]
    out_shape, out_specs = [], []
    for dtypes in (a_dtypes, b_dtypes):
        for k, dt in enumerate(dtypes):
            if split_heads_first and k == 0:
                out_shape.append(jax.ShapeDtypeStruct((B, S, N // head_dim, head_dim), dt))
                out_specs.append(pl.BlockSpec((1, tm, tn // head_dim, head_dim),
                                              lambda b, i, j: (b, i, j, 0)))
            else:
                out_shape.append(jax.ShapeDtypeStruct((B, S, N), dt))
                out_specs.append(pl.BlockSpec((1, tm, tn), lambda b, i, j: (b, i, j)))
    if has_gate:
        out_shape.append(jax.ShapeDtypeStruct((B, S, G), F32))
        out_specs.append(pl.BlockSpec((1, tm, G), lambda b, i, j: (b, i, 0)))
    kern = functools.partial(_dual_norm_matmul_kernel, n_a=len(a_dtypes), n_b=len(b_dtypes),
                             head_dim=head_dim, has_gate=has_gate)
    outs = pl.pallas_call(
        kern, grid=(B, S // tm, N // tn), in_specs=in_specs, out_specs=out_specs,
        out_shape=out_shape, scratch_shapes=[pltpu.VMEM((tm, D), BF16)],
        compiler_params=_params("parallel", "parallel", "arbitrary"), name="dual_norm_matmul",
    )(*args)
    return [o.reshape(lead_shape + o.shape[2:]) for o in outs]


def _matmul_residual_kernel(a_ref, w_ref, x_ref, o_ref):
    o_ref[0] = x_ref[0] + _dot(a_ref[0], w_ref[...])


def _matmul_residual(a, w, x):
    out_shape = x.shape
    a = a.reshape(1, -1, a.shape[-1])
    x = x.reshape(1, -1, x.shape[-1])
    B, S, K = a.shape
    N = w.shape[1]
    tm = _tile(S, 1024)
    tn = _tile(N, 1024, 128)
    return pl.pallas_call(
        _matmul_residual_kernel, grid=(B, S // tm, N // tn),
        in_specs=[pl.BlockSpec((1, tm, K), lambda b, i, j: (b, i, 0)),
                  pl.BlockSpec((K, tn), lambda b, i, j: (0, j)),
                  pl.BlockSpec((1, tm, tn), lambda b, i, j: (b, i, j))],
        out_specs=pl.BlockSpec((1, tm, tn), lambda b, i, j: (b, i, j)),
        out_shape=jax.ShapeDtypeStruct((B, S, N), F32),
        compiler_params=_params("parallel", "parallel", "arbitrary"), name="matmul_residual",
    )(a, w, x).reshape(out_shape)


def _fold_gate_kernel(wq_ref, wk_ref, wv_ref, wg_ref, a_ref, b_ref, *, k_scale):
    def hdot(a, b):
        return jnp.dot(a, b, precision=HIGHEST, preferred_element_type=F32)

    a_ref[0, 0] = hdot(wq_ref[0, 0], wg_ref[0, 0, 0]) + hdot(wk_ref[0, 0], wg_ref[0, 1, 0]) * k_scale
    b_ref[0, 0] = hdot(wv_ref[0, 0], wg_ref[0, 2, 0])


def _fold_gate_weights(wq, wk, wv, w_gate):
    NA, H, DH, _ = wq.shape
    G = w_gate.shape[-1]
    w_spec = pl.BlockSpec((1, 1, DH, DH), lambda i, h: (i, h, 0, 0))
    o_spec = pl.BlockSpec((1, 1, DH, G), lambda i, h: (i, h, 0, 0))
    out = jax.ShapeDtypeStruct((NA, H, DH, G), F32)
    return pl.pallas_call(
        functools.partial(_fold_gate_kernel, k_scale=float(DH) ** -0.5), grid=(NA, H),
        in_specs=[w_spec, w_spec, w_spec,
                  pl.BlockSpec((1, 3, 1, DH, G), lambda i, h: (i, 0, h, 0, 0))],
        out_specs=[o_spec, o_spec], out_shape=[out, out],
        compiler_params=_params("parallel", "parallel"), name="fold_gate_weights",
    )(wq, wk, wv, w_gate.reshape(NA, 3, H, DH, G))


def _conv_gates_kernel(xm_ref, buf_ref, cw_ref, cb_ref, wa_ref, wb_ref, bg_ref,
                       xc_ref, g_ref, seq_ref, *, rows, cum_rows, conv_w, n_heads):
    h = pl.program_id(1)
    S = xm_ref.shape[1]
    seq_ref[0:CONV_PAD, :] = buf_ref[0]
    seq_ref[CONV_PAD:CONV_PAD + S, :] = xm_ref[0].astype(F32)
    for r in range(S // rows):
        base = r * rows
        acc = jnp.broadcast_to(cb_ref[0], (rows, cb_ref.shape[-1]))
        for i in range(conv_w):
            start = base + CONV_PAD - (conv_w - 1) + i
            acc = acc + seq_ref[start:start + rows, :] * cw_ref[0, i:i + 1, :]
        xc = _silu(acc).astype(BF16)
        xc_ref[0, base:base + rows, :] = xc
        part = _dot(xc, wa_ref[0]) + _dot(xm_ref[0, base:base + rows, :], wb_ref[0])

        @pl.when(h == 0)
        def _():
            g_ref[0, base:base + rows, :] = part + bg_ref[...]

        @pl.when(h > 0)
        def _():
            g_ref[0, base:base + rows, :] = g_ref[0, base:base + rows, :] + part

    @pl.when(h == n_heads - 1)
    def _():
        r = lax.broadcasted_iota(jnp.int32, (cum_rows, cum_rows), 0)
        c = lax.broadcasted_iota(jnp.int32, (cum_rows, cum_rows), 1)
        trilf = (c <= r).astype(F32)
        lane = lax.broadcasted_iota(jnp.int32, (cum_rows, g_ref.shape[-1]), 1)
        carry = jnp.zeros((1, g_ref.shape[-1]), F32)
        for i in range(S // cum_rows):
            g = g_ref[0, i * cum_rows:(i + 1) * cum_rows, :]
            lf = jnp.where(lane >= n_heads, _log_sigmoid(g), 0.0)
            cum = jnp.dot(trilf, lf, precision=HIGHEST, preferred_element_type=F32) + carry
            g_ref[0, i * cum_rows:(i + 1) * cum_rows, :] = jnp.where(lane >= n_heads, cum, g)
            carry = cum[cum_rows - 1:cum_rows, :]


def _conv_gates(xz, buf, conv_w, conv_b, w_xc, w_xm, b_gate):
    B, S, _ = xz.shape
    H, DH, G = w_xc.shape
    AI = H * DH
    CW = conv_w.shape[0]
    rows = _tile(S, 768)
    act_spec = pl.BlockSpec((1, S, DH), lambda b, h: (b, 0, h))
    w_spec = pl.BlockSpec((1, DH, G), lambda b, h: (h, 0, 0))
    kern = functools.partial(_conv_gates_kernel, rows=rows, cum_rows=_tile(S, 512, 8), conv_w=CW,
                             n_heads=H)
    return pl.pallas_call(
        kern, grid=(B, H),
        in_specs=[act_spec,
                  pl.BlockSpec((1, CONV_PAD, DH), lambda b, h: (b, 0, h)),
                  pl.BlockSpec((1, CW, DH), lambda b, h: (h, 0, 0)),
                  pl.BlockSpec((1, 1, DH), lambda b, h: (h, 0, 0)),
                  w_spec, w_spec,
                  pl.BlockSpec((1, G), lambda b, h: (0, 0))],
        out_specs=[act_spec, pl.BlockSpec((1, S, G), lambda b, h: (b, 0, 0))],
        out_shape=[jax.ShapeDtypeStruct((B, S, AI), BF16), jax.ShapeDtypeStruct((B, S, G), F32)],
        scratch_shapes=[pltpu.VMEM((S + CONV_PAD, DH), F32)],
        compiler_params=_params("parallel", "arbitrary"), name="conv_gates",
    )(xz, buf,
      conv_w.reshape(CW, H, DH).transpose(1, 0, 2).astype(F32),
      conv_b.reshape(H, 1, DH).astype(F32), w_xc, w_xm, b_gate.reshape(1, G).astype(F32))


def _in_conv_kernel(x_ref, g_ref, w_ref, wz_ref, buf_ref, cw_ref, cb_ref, wa_ref, wb_ref, bg_ref,
                    xm_ref, z_ref, xc_ref, gate_ref, xn_ref, hist_ref, *, conv_w, n_heads):
    i = pl.program_id(1)
    j = pl.program_id(2)
    tm = x_ref.shape[1]

    @pl.when(j == 0)
    def _():
        x = x_ref[0].astype(F32)
        ms = jnp.mean(x * x, axis=-1, keepdims=True)
        xn_ref[...] = (x * lax.rsqrt(ms + EPS) * g_ref[...]).astype(BF16)

    prev = jnp.where(i == 0, buf_ref[0], hist_ref[j])
    row = lax.broadcasted_iota(jnp.int32, prev.shape, 0)
    mid = tm // 2 // BF16_ROWS * BF16_ROWS
    parts = []
    for r0, r1 in ((0, mid), (mid, tm)) if mid else ((0, tm),):
        acc = _dot(xn_ref[r0:r1, :], w_ref[...])
        xm = acc.astype(BF16)
        xm_ref[0, r0:r1, :] = xm
        y = cb_ref[...] + acc * cw_ref[conv_w - 1:conv_w, :]
        for shift in range(1, conv_w):
            rolled = pltpu.roll(acc, shift, 0)
            head = jnp.where(row < shift, pltpu.roll(prev, shift, 0), rolled[0:CONV_PAD, :])
            shifted = jnp.concatenate([head, rolled[CONV_PAD:, :]], axis=0)
            y = y + shifted * cw_ref[conv_w - 1 - shift:conv_w - shift, :]
        xc = _silu(y).astype(BF16)
        xc_ref[0, r0:r1, :] = xc
        z_ref[0, r0:r1, :] = _dot(xn_ref[r0:r1, :], wz_ref[...]).astype(BF16)
        parts.append(_dot(xc, wa_ref[...]) + _dot(xm, wb_ref[...]))
        prev = acc[r1 - r0 - CONV_PAD:r1 - r0, :]
    hist_ref[j] = prev
    part = jnp.concatenate(parts, axis=0)

    @pl.when(j == 0)
    def _():
        gate_ref[0] = part + bg_ref[...]

    @pl.when(j > 0)
    def _():
        gate_ref[0] = gate_ref[0] + part

    @pl.when(j == pl.num_programs(2) - 1)
    def _():
        g = gate_ref[0]
        lane = lax.broadcasted_iota(jnp.int32, g.shape, 1)
        gate_ref[0] = jnp.where(lane >= n_heads, _log_sigmoid(g), g)


def _in_conv(x, norm_g, w_xm_in, w_z_in, buf, conv_w, conv_b, w_xc, w_xm, b_gate):
    B, S, D = x.shape
    H, DH, G = w_xc.shape
    AI = H * DH
    CW = conv_w.shape[0]
    tm = _tile(S, 1024)
    tn = _tile(AI, 1024, LANES)
    act_spec = pl.BlockSpec((1, tm, tn), lambda b, i, j: (b, i, j))
    act = jax.ShapeDtypeStruct((B, S, AI), BF16)
    kern = functools.partial(_in_conv_kernel, conv_w=CW, n_heads=H)
    return pl.pallas_call(
        kern, grid=(B, S // tm, AI // tn),
        in_specs=[pl.BlockSpec((1, tm, D), lambda b, i, j: (b, i, 0)),
                  pl.BlockSpec((1, D), lambda b, i, j: (0, 0)),
                  pl.BlockSpec((D, tn), lambda b, i, j: (0, j)),
                  pl.BlockSpec((D, tn), lambda b, i, j: (0, j)),
                  pl.BlockSpec((1, CONV_PAD, tn), lambda b, i, j: (b, 0, j)),
                  pl.BlockSpec((CW, tn), lambda b, i, j: (0, j)),
                  pl.BlockSpec((1, tn), lambda b, i, j: (0, j)),
                  pl.BlockSpec((tn, G), lambda b, i, j: (j, 0)),
                  pl.BlockSpec((tn, G), lambda b, i, j: (j, 0)),
                  pl.BlockSpec((1, G), lambda b, i, j: (0, 0))],
        out_specs=[act_spec, act_spec, act_spec,
                   pl.BlockSpec((1, tm, G), lambda b, i, j: (b, i, 0))],
        out_shape=[act, act, act, jax.ShapeDtypeStruct((B, S, G), F32)],
        scratch_shapes=[pltpu.VMEM((tm, D), BF16), pltpu.VMEM((AI // tn, CONV_PAD, tn), F32)],
        compiler_params=_params("parallel", "arbitrary", "arbitrary"), name="in_conv",
    )(x, norm_g.reshape(1, D).astype(F32), w_xm_in, w_z_in, buf, conv_w.astype(F32),
      conv_b.reshape(1, AI).astype(F32), w_xc.reshape(AI, G), w_xm.reshape(AI, G),
      b_gate.reshape(1, G).astype(F32))


def _mlstm_project(xm, xc, wq, wk, wv):
    return (_dot(xc, wq).astype(BF16), _dot(xc, wk).astype(BF16), _dot(xm, wv).astype(BF16))


def _mlstm_chunk(q, k, v, xc, z, g_rows, g_cols, head, n_heads, og, skip,
                 C_ref, n_ref, m_ref, b_ref):
    L = q.shape[0]
    row = lax.broadcasted_iota(jnp.int32, (L, L), 0)
    col = lax.broadcasted_iota(jnp.int32, (L, L), 1)
    tril = col <= row
    lane = lax.broadcasted_iota(jnp.int32, g_cols.shape, 1)
    ig_col = jnp.sum(jnp.where(lane == head, g_cols, 0.0), axis=-1, keepdims=True)
    f_col = jnp.sum(jnp.where(lane == head + n_heads, g_cols, 0.0), axis=-1, keepdims=True)
    ig_row = g_rows[0:1, :]
    f_row = g_rows[1:2, :]
    m_prev = m_ref[...]
    f_prev = b_ref[...]
    log_d = jnp.where(tril, f_col - f_row + ig_row, NEG_INF)
    inter = f_col - f_prev + m_prev
    m_t = jnp.maximum(jnp.max(log_d, axis=-1, keepdims=True), inter)
    d = jnp.exp(log_d - m_t)
    a = jnp.exp(inter - m_t)
    s = _dot_nt(q, k) * d
    C = C_ref[...]
    n = n_ref[...]
    num = _dot(s.astype(BF16), v) + a * _dot(q, C.astype(BF16))
    qn = _dot_nt(q, jnp.broadcast_to(n, (8, n.shape[1])).astype(BF16))[:, 0:1]
    den = jnp.sum(s, axis=-1, keepdims=True) + a * qn
    hid = num * (1.0 / jnp.maximum(jnp.abs(den), jnp.exp(-m_t)))
    m_new = m_t[L - 1:L, :]
    f_last = f_row[:, L - 1:L]
    decay = jnp.exp(f_last - f_prev + m_prev - m_new)
    w_col = jnp.exp(f_last - f_col + ig_col - m_new)
    C_ref[...] = decay * C + _dot_tn(k, (w_col * v.astype(F32)).astype(BF16))
    n_ref[...] = decay * n + jnp.sum(w_col * k.astype(F32), axis=0, keepdims=True)
    m_ref[...] = m_new
    b_ref[...] = f_last
    hc = hid - jnp.mean(hid, axis=-1, keepdims=True)
    hn = hc * lax.rsqrt(jnp.mean(hc * hc, axis=-1, keepdims=True) + EPS) * og
    return ((hn + skip * xc.astype(F32)) * _silu(z.astype(F32))).astype(BF16)


def _mlstm_kernel(*refs, lead, chunk, n_chunks, n_heads, hps, has_state):
    xm_ref, xc_ref, z_ref, wq_ref, wk_ref, wv_ref = refs[:6]
    pos = 6
    if lead:
        gl_ref = refs[pos]
        pos += 1
    gm_ref, gc_ref, og_ref, skip_ref = refs[pos:pos + 4]
    pos += 4
    if has_state:
        c0_ref, n0_ref, m0_ref = refs[pos:pos + 3]
        pos += 3
    pos += 1
    o_ref, C_ref, n_ref, m_ref, b_ref = refs[pos:pos + 5]
    DH = wq_ref.shape[-1]
    for hh in range(hps):
        if has_state:
            C_ref[0, 0, hh] = c0_ref[0, 0, hh]
            n_ref[0, hh] = n0_ref[0, 0, hh]
            m_ref[0, hh] = m0_ref[0, 0, hh]
        else:
            C_ref[0, 0, hh] = jnp.zeros((DH, DH), F32)
            n_ref[0, hh] = jnp.zeros((1, DH), F32)
            m_ref[0, hh] = jnp.zeros((1, 1), F32)
    b_ref[...] = jnp.zeros(b_ref.shape, F32)
    head0 = pl.program_id(1) * hps

    def chunk_rows(c):
        return pl.ds(pl.multiple_of(lead + c * chunk, BF16_ROWS), chunk)

    def project(rows, hh):
        cs = slice(hh * DH, (hh + 1) * DH)
        return _mlstm_project(xm_ref[0, rows, cs], xc_ref[0, rows, cs], wq_ref[hh], wk_ref[hh],
                              wv_ref[hh])

    def recur(rows, hh, qkv, g_rows):
        cs = slice(hh * DH, (hh + 1) * DH)
        o_ref[0, rows, cs] = _mlstm_chunk(
            *qkv, xc_ref[0, rows, cs], z_ref[0, rows, cs], g_rows, gc_ref[0, rows, :],
            head0 + hh, n_heads, og_ref[hh], skip_ref[hh], C_ref.at[0, 0, hh], n_ref.at[0, hh],
            m_ref.at[0, hh], b_ref.at[hh])

    for hh in range(hps):
        if lead:
            recur(pl.ds(0, lead), hh, project(pl.ds(0, lead), hh), gl_ref[0, hh])

    def body(c, carry):
        for hh in range(hps):
            recur(chunk_rows(c), hh, project(chunk_rows(c), hh), gm_ref[0, hh, c])
        return carry

    lax.fori_loop(0, n_chunks, body, 0, unroll=_tile(n_chunks, 2, 1))


def _mlstm(xm, z, z_head0, xc, gates, wq, wk, wv, out_g, skip, state, lead, chunk, c_stack, layer,
           n_layers, heads_per_step=1):
    B, S, AI = xc.shape
    H, DH = out_g.shape
    n_chunks = (S - lead) // chunk
    assert lead + n_chunks * chunk == S
    g = gates.reshape(B, S, 2, H).transpose(0, 3, 2, 1)
    g_main = g[..., lead:].reshape(B, H, 2, n_chunks, chunk).transpose(0, 1, 3, 2, 4)
    hps = heads_per_step if H % heads_per_step == 0 and z_head0 % heads_per_step == 0 else 1
    act_spec = pl.BlockSpec((1, S, hps * DH), lambda b, h: (b, 0, h))
    w_spec = pl.BlockSpec((hps, DH, DH), lambda b, h: (h, 0, 0))
    in_specs = [act_spec, act_spec,
                pl.BlockSpec((1, S, hps * DH), lambda b, h: (b, 0, z_head0 // hps + h)),
                w_spec, w_spec, w_spec]
    args = [xm, xc, z, wq, wk, wv]
    if lead:
        in_specs.append(pl.BlockSpec((1, hps, 2, lead), lambda b, h: (b, h, 0, 0)))
        args.append(g[..., :lead])
    in_specs += [pl.BlockSpec((1, hps, n_chunks, 2, chunk), lambda b, h: (b, h, 0, 0, 0)),
                 pl.BlockSpec((1, S, 2 * H), lambda b, h: (b, 0, 0)),
                 pl.BlockSpec((hps, 1, DH), lambda b, h: (h, 0, 0)),
                 pl.BlockSpec((hps, 1, DH), lambda b, h: (h, 0, 0))]
    args += [g_main, gates, out_g.reshape(H, 1, DH).astype(F32), skip.reshape(H, 1, DH).astype(F32)]
    n_spec = pl.BlockSpec((1, hps, 1, DH), lambda b, h: (b, h, 0, 0))
    m_spec = pl.BlockSpec((1, hps, 1, 1), lambda b, h: (b, h, 0, 0))
    has_state = state is not None
    if has_state:
        C0, n0, m0 = state
        NL = C0.shape[0]
        in_specs += [pl.BlockSpec((1, 1, hps, DH, DH), lambda b, h: (layer, b, h, 0, 0)),
                     pl.BlockSpec((1, 1, hps, 1, DH), lambda b, h: (layer, b, h, 0, 0)),
                     pl.BlockSpec((1, 1, hps, 1, 1), lambda b, h: (layer, b, h, 0, 0))]
        args += [C0.astype(F32), n0.reshape(NL, B, H, 1, DH).astype(F32),
                 m0.reshape(NL, B, H, 1, 1).astype(F32)]
    if c_stack is None:
        c_stack = jnp.zeros((n_layers, B, H, DH, DH), F32)
    aliases = {len(args): 1}
    in_specs.append(pl.BlockSpec(memory_space=pl.ANY))
    args.append(c_stack)
    kern = functools.partial(_mlstm_kernel, lead=lead, chunk=chunk, n_chunks=n_chunks, n_heads=H,
                             hps=hps, has_state=has_state)
    hz, C, n, m = pl.pallas_call(
        kern, grid=(B, H // hps), in_specs=in_specs,
        out_specs=[act_spec,
                   pl.BlockSpec((1, 1, hps, DH, DH), lambda b, h: (layer, b, h, 0, 0)),
                   n_spec, m_spec],
        out_shape=[jax.ShapeDtypeStruct((B, S, AI), BF16),
                   jax.ShapeDtypeStruct((n_layers, B, H, DH, DH), F32),
                   jax.ShapeDtypeStruct((B, H, 1, DH), F32),
                   jax.ShapeDtypeStruct((B, H, 1, 1), F32)],
        scratch_shapes=[pltpu.VMEM((hps, 1, 1), F32)],
        input_output_aliases=aliases,
        compiler_params=_params("parallel", "parallel"), name="mlstm",
    )(*args)
    return hz, C, n.reshape(B, H, DH), m.reshape(B, H)


def _cumsum_kernel(x_ref, o_ref, *, scale, first_lane):
    x = x_ref[0]
    S = x.shape[0]
    summed = lax.broadcasted_iota(jnp.int32, x.shape, 1) >= first_lane
    row = lax.broadcasted_iota(jnp.int32, x.shape, 0)
    y = jnp.where(summed, x, 0.0)
    step = 1
    while step < S:
        y = y + jnp.where(row >= step, pltpu.roll(y, step, 0), 0.0)
        step *= 2
    o_ref[0] = jnp.where(summed, y * scale, x)


def _cumsum(x, scale=1.0, first_lane=0):
    B, S, G = x.shape
    spec = pl.BlockSpec((1, S, G), lambda b: (b, 0, 0))
    return pl.pallas_call(
        functools.partial(_cumsum_kernel, scale=scale, first_lane=first_lane),
        grid=(B,), in_specs=[spec], out_specs=spec, out_shape=jax.ShapeDtypeStruct((B, S, G), F32),
        compiler_params=_params("parallel"), name="cumsum",
    )(x)


def _lane_fold(op, acc, x):
    for c in range(x.shape[1] // LANES):
        acc = op(acc, x[:, c * LANES:(c + 1) * LANES])
    return acc


def _fox_prompt_kernel(q_ref, z_ref, k_ref, v_ref, fq_ref, fkl_ref, fkm_ref, o_ref,
                       s_ref, sl_ref, fq_rep, m_ref, l_ref, acc_ref, *, lead, blk, n_blk, heads,
                       dim):
    n_lane = blk // LANES
    cols = [slice(hh * dim, (hh + 1) * dim) for hh in range(heads)]

    def key_rows(j):
        return pl.ds(pl.multiple_of(lead + j * blk, BF16_ROWS), blk)

    def tiled(x):
        return jnp.concatenate([x] * n_lane, axis=1)

    def q_block(i, carry):
        qrows = pl.ds(pl.multiple_of(i * blk, blk), blk)

        def logits(hh, j):
            return (_dot_nt(q_ref[0, qrows, cols[hh]], k_ref[0, key_rows(j), cols[hh]])
                    + tiled(fq_rep[hh]) - fkm_ref[0, 0, j, hh:hh + 1, :])

        for hh in range(heads):
            fq_rep[hh] = jnp.broadcast_to(fq_ref[0, 0, qrows, hh:hh + 1], (blk, LANES))
        if lead:
            for hh in range(heads):
                s = (_dot_nt(q_ref[0, qrows, cols[hh]], k_ref[0, 0:LANES, cols[hh]])
                     + fq_rep[hh] - fkl_ref[0, 0, hh:hh + 1, :])
                sl_ref[hh] = s
                m_ref[hh] = s
        else:
            m_ref[...] = jnp.full(m_ref.shape, NEG_INF, F32)

        def pass1(j, c):
            for hh in range(heads):
                s = logits(hh, j)
                s_ref[hh, j] = s
                m_ref[hh] = _lane_fold(jnp.maximum, m_ref[hh], s)
            return c

        lax.fori_loop(0, i, pass1, 0)
        row = lax.broadcasted_iota(jnp.int32, (blk, blk), 0)
        col = lax.broadcasted_iota(jnp.int32, (blk, blk), 1)
        for hh in range(heads):
            s = jnp.where(col <= row, logits(hh, i), NEG_INF)
            s_ref[hh, i] = s
            m = jnp.max(_lane_fold(jnp.maximum, m_ref[hh], s), axis=-1, keepdims=True)
            m_ref[hh] = jnp.broadcast_to(m, (blk, LANES))
        if lead:
            for hh in range(heads):
                p = jnp.exp2(sl_ref[hh] - m_ref[hh])
                l_ref[hh] = p
                acc_ref[hh] = _dot(p.astype(BF16), v_ref[0, 0:LANES, cols[hh]])
        else:
            l_ref[...] = jnp.zeros(l_ref.shape, F32)
            acc_ref[...] = jnp.zeros(acc_ref.shape, F32)

        def pass2(j, c):
            for hh in range(heads):
                p = jnp.exp2(s_ref[hh, j] - tiled(m_ref[hh]))
                l_ref[hh] = _lane_fold(jnp.add, l_ref[hh], p)
                acc_ref[hh] += _dot(p.astype(BF16), v_ref[0, key_rows(j), cols[hh]])
            return c

        lax.fori_loop(0, i + 1, pass2, 0)
        for hh in range(heads):
            l = jnp.sum(l_ref[hh], axis=-1, keepdims=True)
            o_ref[0, qrows, cols[hh]] = (acc_ref[hh] * (1.0 / l)
                                         * _silu(z_ref[0, qrows, cols[hh]].astype(F32))).astype(BF16)
        return carry

    lax.fori_loop(0, n_blk, q_block, 0)


def _fox_prompt(q, z, kb, vb, f_cum, lead, heads_per_step=4, blk=512):
    B, SQ, W = q.shape
    NH = f_cum.shape[-1]
    dim = W // NH
    hg = min(heads_per_step, NH)
    G = NH // hg
    blk = _tile(SQ, blk, LANES)
    n_blk = SQ // blk
    SK = lead + SQ
    assert lead <= LANES <= SK
    fq = f_cum[:, lead:].reshape(B, SQ, G, hg).transpose(0, 2, 1, 3)
    fk = f_cum.transpose(0, 2, 1).reshape(B, G, hg, SK)
    fk_lead = jnp.pad(fk[..., :lead], ((0, 0), (0, 0), (0, 0), (0, LANES - lead)),
                      constant_values=float("inf"))
    fk_main = fk[..., lead:].reshape(B, G, hg, n_blk, blk).transpose(0, 1, 3, 2, 4)
    wcols = hg * dim
    kern = functools.partial(_fox_prompt_kernel, lead=lead, blk=blk, n_blk=n_blk, heads=hg, dim=dim)
    return pl.pallas_call(
        kern, grid=(B, G),
        in_specs=[pl.BlockSpec((1, SQ, wcols), lambda b, g: (b, 0, g)),
                  pl.BlockSpec((1, SQ, wcols), lambda b, g: (b, 0, g)),
                  pl.BlockSpec((1, SK, wcols), lambda b, g: (b, 0, g)),
                  pl.BlockSpec((1, SK, wcols), lambda b, g: (b, 0, g)),
                  pl.BlockSpec((1, 1, SQ, hg), lambda b, g: (b, g, 0, 0)),
                  pl.BlockSpec((1, 1, hg, LANES), lambda b, g: (b, g, 0, 0)),
                  pl.BlockSpec((1, 1, n_blk, hg, blk), lambda b, g: (b, g, 0, 0, 0))],
        out_specs=pl.BlockSpec((1, SQ, wcols), lambda b, g: (b, 0, g)),
        out_shape=jax.ShapeDtypeStruct((B, SQ, W), BF16),
        scratch_shapes=[pltpu.VMEM((hg, n_blk, blk, blk), F32)]
        + [pltpu.VMEM((hg, blk, LANES), F32)] * 4 + [pltpu.VMEM((hg, blk, dim), F32)],
        compiler_params=_params("parallel", "parallel"), name="fox_prompt",
    )(q, z, kb, vb, fq, fk_lead, fk_main)


def _pack_heads_kernel(x_ref, o_ref):
    o_ref[0] = pltpu.einshape("khd->k(hd)", x_ref[0]).astype(o_ref.dtype)


def _pack_heads(x, dtype):
    B, P, NH, dim = x.shape
    rows = _tile(P, 512)
    return pl.pallas_call(
        _pack_heads_kernel, grid=(B, P // rows),
        in_specs=[pl.BlockSpec((1, rows, NH, dim), lambda b, i: (b, i, 0, 0))],
        out_specs=pl.BlockSpec((1, rows, NH * dim), lambda b, i: (b, i, 0)),
        out_shape=jax.ShapeDtypeStruct((B, P, NH * dim), dtype),
        compiler_params=_params("parallel", "parallel"), name="pack_heads",
    )(x)


def _fox_decode_kernel(q_ref, z_ref, ck_ref, cv_ref, k_ref, v_ref, fq_ref, fkc_ref, fkn_ref, o_ref,
                       *, heads, dim):
    Q = q_ref.shape[1]
    row = lax.broadcasted_iota(jnp.int32, (Q, Q), 0)
    col = lax.broadcasted_iota(jnp.int32, (Q, Q), 1)
    outs = []
    for hh in range(heads):
        cs = slice(hh * dim, (hh + 1) * dim)
        q = q_ref[0, :, cs]
        fq = fq_ref[0, 0, :, hh:hh + 1]
        s_c = _dot_nt(q, ck_ref[0, :, cs].astype(BF16)) + fq - fkc_ref[0, 0, hh:hh + 1, :]
        s_n = jnp.where(col <= row, _dot_nt(q, k_ref[0, :, cs]) + fq - fkn_ref[0, 0, hh:hh + 1, :],
                        NEG_INF)
        m = jnp.maximum(jnp.max(s_c, axis=-1, keepdims=True), jnp.max(s_n, axis=-1, keepdims=True))
        p_c = jnp.exp2(s_c - m)
        p_n = jnp.exp2(s_n - m)
        l = jnp.sum(p_c, axis=-1, keepdims=True) + jnp.sum(p_n, axis=-1, keepdims=True)
        acc = _dot(p_c.astype(BF16), cv_ref[0, :, cs].astype(BF16)) + _dot(p_n.astype(BF16), v_ref[0, :, cs])
        outs.append((acc * (1.0 / l) * _silu(z_ref[0, :, cs].astype(F32))).astype(BF16))
    o_ref[0] = jnp.concatenate(outs, axis=-1)


def _fox_decode(q, z, cache_k, cache_v, kb, vb, f_cum, heads_per_step=4):
    B, Q, W = q.shape
    P = cache_k.shape[1]
    NH = f_cum.shape[-1]
    dim = W // NH
    hg = min(heads_per_step, NH)
    G = NH // hg
    fq = f_cum[:, P:].reshape(B, Q, G, hg).transpose(0, 2, 1, 3)
    fk = f_cum.transpose(0, 2, 1).reshape(B, G, hg, P + Q)
    wcols = hg * dim
    kern = functools.partial(_fox_decode_kernel, heads=hg, dim=dim)
    return pl.pallas_call(
        kern, grid=(B, G),
        in_specs=[pl.BlockSpec((1, Q, wcols), lambda b, g: (b, 0, g)),
                  pl.BlockSpec((1, Q, wcols), lambda b, g: (b, 0, g)),
                  pl.BlockSpec((1, P, wcols), lambda b, g: (b, 0, g)),
                  pl.BlockSpec((1, P, wcols), lambda b, g: (b, 0, g)),
                  pl.BlockSpec((1, Q, wcols), lambda b, g: (b, 0, g)),
                  pl.BlockSpec((1, Q, wcols), lambda b, g: (b, 0, g)),
                  pl.BlockSpec((1, 1, Q, hg), lambda b, g: (b, g, 0, 0)),
                  pl.BlockSpec((1, 1, hg, P), lambda b, g: (b, g, 0, 0)),
                  pl.BlockSpec((1, 1, hg, Q), lambda b, g: (b, g, 0, 0))],
        out_specs=pl.BlockSpec((1, Q, wcols), lambda b, g: (b, 0, g)),
        out_shape=jax.ShapeDtypeStruct((B, Q, W), BF16),
        compiler_params=_params("parallel", "parallel"), name="fox_decode",
    )(q, z, cache_k, cache_v, kb, vb, fq, fk[..., :P], fk[..., P:])


def _mlstm_layer(x, conv_hist, state, w, lead, chunk, c_stack, layer, n_layers):
    (norm_g, w_in, conv_w, conv_b, wq, wk, wv, w_xc, w_xm, b_gate, out_g, skip, w_out) = w
    B, S, _ = x.shape
    H, DH, _ = wq.shape
    AI = H * DH
    CW = conv_w.shape[0]
    assert S >= CW - 1 and CW - 1 <= CONV_PAD
    buf = jnp.zeros((B, CONV_PAD, AI), F32)
    if conv_hist is not None:
        buf = buf.at[:, CONV_PAD - (CW - 1):].set(conv_hist.astype(F32))
    if S >= MIN_FUSED_CONV_ROWS:
        wb = w_in.astype(BF16)
        xm, z, xc, gates = _in_conv(x, norm_g, wb[:, :AI], wb[:, AI:], buf, conv_w, conv_b,
                                    w_xc.astype(BF16), w_xm.astype(BF16), b_gate)
        gates = _cumsum(gates, first_lane=H)
        z_head0 = 0
    else:
        xm = _norm_matmul(x, norm_g, w_in.astype(BF16))
        xc, gates = _conv_gates(xm, buf, conv_w, conv_b, w_xc.astype(BF16), w_xm.astype(BF16),
                                b_gate)
        z, z_head0 = xm, H
    hz, c_stack, n, m = _mlstm(
        xm, z, z_head0, xc, gates, wq.astype(BF16), (wk * float(DH) ** -0.5).astype(BF16),
        wv.astype(BF16), out_g, skip.reshape(H, DH), state, lead, chunk, c_stack, layer, n_layers)
    x_new = _matmul_residual(hz, w_out.astype(BF16), x)
    new_hist = xm[:, S - (CW - 1):, :AI].astype(F32)
    return x_new, new_hist, c_stack, n, m


def _shared_kv(x, kv_norm_g, w_kvf, b_f, k_norm_g):
    NH, dim = k_norm_g.shape
    W = NH * dim
    wb = w_kvf.astype(BF16)
    k32, kb, v32, vb, logf = _dual_norm_matmul(
        x, kv_norm_g, wb[:, :W], k_norm_g.reshape(1, W), wb[:, W:2 * W], a_dtypes=[F32, BF16],
        b_dtypes=[F32, BF16], head_dim=dim, split_heads_first=True, gate_w=wb[:, 2 * W:],
        gate_b=b_f)
    return k32, kb, v32, vb, logf


def _fox_in(x, norm_g, w_in, q_norm_g):
    NH, dim = q_norm_g.shape
    W = NH * dim
    wb = w_in.astype(BF16)
    gain = q_norm_g.reshape(1, W).astype(F32) * (float(dim) ** -0.5 * LOG2E)
    q, z = _dual_norm_matmul(x, norm_g, wb[:, :W], gain, wb[:, W:], a_dtypes=[BF16],
                             b_dtypes=[BF16], head_dim=dim)
    return q, z


def kernel(x_prompt, x_sample, cache_k, cache_v, cache_logf, state_C, state_n, state_m, state_conv,
           meta_tokens, a_norm_g, a_w_in, a_conv_w, a_conv_b, a_wq, a_wk, a_wv, a_w_gate, a_b_gate,
           a_out_g, a_skip, a_w_out, kv_norm_g, w_kvf, b_f, k_norm_g, b_norm_g, b_w_in, q_norm_g,
           b_w_out):
    B, SEQ, D = x_prompt.shape
    DB, DEC, _ = x_sample.shape
    NM = meta_tokens.shape[0]
    N_A = a_norm_g.shape[0]
    N_B = b_norm_g.shape[0]
    NH, dim = k_norm_g.shape
    W = NH * dim
    P = cache_k.shape[1]
    chunk = _tile(SEQ, 256)

    xp = jnp.concatenate(
        [jnp.broadcast_to(meta_tokens.astype(x_prompt.dtype)[None], (B, NM, D)), x_prompt], axis=1)
    xs = x_sample
    p_state, s_state = [], []
    p_C = s_C = None
    w_xc, w_xm = _fold_gate_weights(a_wq, a_wk, a_wv, a_w_gate)
    for i in range(N_A):
        w = (a_norm_g[i], a_w_in[i], a_conv_w[i], a_conv_b[i], a_wq[i], a_wk[i], a_wv[i],
             w_xc[i], w_xm[i], a_b_gate[i], a_out_g[i], a_skip[i], a_w_out[i])
        xp, hist, p_C, n, m = _mlstm_layer(xp, None, None, w, NM, chunk, p_C, i, N_A)
        p_state.append((hist, n, m))
        xs, hist, s_C, n, m = _mlstm_layer(xs, state_conv[i], (state_C, state_n, state_m),
                                           w, 0, DEC, s_C, i, N_A)
        s_state.append((hist, n, m))

    pk32, pkb, pv32, pvb, p_logf = _shared_kv(xp, kv_norm_g, w_kvf, b_f, k_norm_g)
    sk32, skb, sv32, svb, s_logf = _shared_kv(xs, kv_norm_g, w_kvf, b_f, k_norm_g)
    fp = _cumsum(p_logf, LOG2E)
    fs = _cumsum(jnp.concatenate([cache_logf.astype(F32), s_logf], axis=1), LOG2E)
    ck = _pack_heads(cache_k, BF16)
    cv = _pack_heads(cache_v, BF16)
    xp = xp[:, NM:]
    for j in range(N_B):
        q, z = _fox_in(xp, b_norm_g[j], b_w_in[j], q_norm_g[j])
        xp = _matmul_residual(_fox_prompt(q, z, pkb, pvb, fp, NM), b_w_out[j].astype(BF16), xp)
        q, z = _fox_in(xs, b_norm_g[j], b_w_in[j], q_norm_g[j])
        xs = _matmul_residual(_fox_decode(q, z, ck, cv, skb, svb, fs), b_w_out[j].astype(BF16), xs)

    def stack(states, idx):
        return jnp.stack([st[idx] for st in states])

    return (xp, xs, pk32, pv32, p_logf,
            p_C, stack(p_state, 1), stack(p_state, 2), stack(p_state, 0),
            sk32, sv32, s_logf,
            s_C, stack(s_state, 1), stack(s_state, 2), stack(s_state, 0))
```

```python
import functools

import jax
import jax.numpy as jnp
from jax import lax
from jax.experimental import pallas as pl
from jax.experimental.pallas import tpu as pltpu

F32 = jnp.float32
BF16 = jnp.bfloat16
EPS = 1e-6
CONV_PAD = 8
BF16_ROWS = 16
LANES = 128
MIN_FUSED_CONV_ROWS = 256
VMEM_LIMIT_BYTES = 56 * 1024 * 1024
HIGHEST = lax.Precision.HIGHEST
NEG_INF = float("-inf")
LOG2E = 1.4426950408889634


def _params(*sem):
    return pltpu.CompilerParams(dimension_semantics=sem, vmem_limit_bytes=VMEM_LIMIT_BYTES)


def _tile(n, cap, mult=BF16_ROWS):
    best = None
    for t in range(mult, min(n, cap) + 1, mult):
        if n % t == 0:
            best = t
    return best if best is not None else n


def _dot(a, b):
    return jnp.dot(a, b, preferred_element_type=F32)


def _dot_nt(a, b):
    return lax.dot_general(a, b, (((1,), (1,)), ((), ())), preferred_element_type=F32)


def _dot_tn(a, b):
    return lax.dot_general(a, b, (((0,), (0,)), ((), ())), preferred_element_type=F32)


def _silu(x):
    return x / (1.0 + jnp.exp(-x))


def _log_sigmoid(x):
    return jnp.minimum(x, 0.0) - jnp.log1p(jnp.exp(-jnp.abs(x)))


def _norm_matmul_kernel(x_ref, g_ref, w_ref, o_ref, xn_ref):
    @pl.when(pl.program_id(2) == 0)
    def _():
        x = x_ref[0].astype(F32)
        ms = jnp.mean(x * x, axis=-1, keepdims=True)
        xn_ref[...] = (x * lax.rsqrt(ms + EPS) * g_ref[...]).astype(BF16)

    o_ref[0] = _dot(xn_ref[...], w_ref[...]).astype(o_ref.dtype)


def _norm_matmul(x, g, w):
    lead_shape = x.shape[:2]
    x = x.reshape(1, -1, x.shape[-1])
    B, S, D = x.shape
    N = w.shape[1]
    tm = _tile(S, 1024)
    tn = _tile(N, 1024, LANES)
    out = pl.pallas_call(
        _norm_matmul_kernel, grid=(B, S // tm, N // tn),
        in_specs=[pl.BlockSpec((1, tm, D), lambda b, i, j: (b, i, 0)),
                  pl.BlockSpec((1, D), lambda b, i, j: (0, 0)),
                  pl.BlockSpec((D, tn), lambda b, i, j: (0, j))],
        out_specs=pl.BlockSpec((1, tm, tn), lambda b, i, j: (b, i, j)),
        out_shape=jax.ShapeDtypeStruct((B, S, N), BF16),
        scratch_shapes=[pltpu.VMEM((tm, D), BF16)],
        compiler_params=_params("parallel", "parallel", "arbitrary"), name="norm_matmul",
    )(x, g.reshape(1, D).astype(F32), w)
    return out.reshape(lead_shape + (N,))


def _dual_norm_matmul_kernel(*refs, n_a, n_b, head_dim, has_gate):
    x_ref, g_ref, wa_ref, gain_ref, wb_ref = refs[:5]
    pos = 5
    if has_gate:
        gw_ref, gb_ref = refs[pos:pos + 2]
        pos += 2
    a_refs = refs[pos:pos + n_a]
    b_refs = refs[pos + n_a:pos + n_a + n_b]
    pos += n_a + n_b
    if has_gate:
        gate_out_ref = refs[pos]
        pos += 1
    xn_ref = refs[pos]

    @pl.when(pl.program_id(2) == 0)
    def _():
        x = x_ref[0].astype(F32)
        ms = jnp.mean(x * x, axis=-1, keepdims=True)
        xn_ref[...] = (x * lax.rsqrt(ms + EPS) * g_ref[...]).astype(BF16)
        if has_gate:
            gate_out_ref[0] = _log_sigmoid(_dot(xn_ref[...], gw_ref[...]) + gb_ref[...])

    def store(out_refs, val):
        for o in out_refs:
            if len(o.shape) == 4:
                o[0] = val.reshape(o.shape[1:]).astype(o.dtype)
            else:
                o[0] = val.astype(o.dtype)

    acc = _dot(xn_ref[...], wa_ref[...])
    cols = []
    for c in range(acc.shape[1] // head_dim):
        blk = acc[:, c * head_dim:(c + 1) * head_dim]
        ms = jnp.mean(blk * blk, axis=-1, keepdims=True)
        cols.append(blk * lax.rsqrt(ms + EPS))
    store(a_refs, jnp.concatenate(cols, axis=-1) * gain_ref[...])
    store(b_refs, _dot(xn_ref[...], wb_ref[...]))


def _dual_norm_matmul(x, g, w, N, gain_a, *, a_dtypes, b_dtypes, head_dim, split_heads_first=False,
                      gate_w=None, gate_b=None):
    lead_shape = x.shape[:2]
    x = x.reshape(1, -1, x.shape[-1])
    B, S, D = x.shape
    tm = _tile(S, 1024)
    tn = _tile(N, 1024, LANES)
    has_gate = gate_w is not None
    in_specs = [pl.BlockSpec((1, tm, D), lambda b, i, j: (b, i, 0)),
                pl.BlockSpec((1, D), lambda b, i, j: (0, 0)),
                pl.BlockSpec((D, tn), lambda b, i, j: (0, j)),
                pl.BlockSpec((1, tn), lambda b, i, j: (0, j)),
                pl.BlockSpec((D, tn), lambda b, i, j: (0, N // tn + j))]
    args = [x, g.reshape(1, D).astype(F32), w, gain_a.astype(F32), w]
    if has_gate:
        G = gate_w.shape[1]
        in_specs += [pl.BlockSpec((D, G), lambda b, i, j: (0, 0)),
                     pl.BlockSpec((1, G), lambda b, i, j: (0, 0))]
        args += [gate_w, gate_b.reshape(1, G).astype(F32)]
    out_shape, out_specs = [], []
    for dtypes in (a_dtypes, b_dtypes):
        for k, dt in enumerate(dtypes):
            if split_heads_first and k == 0:
                out_shape.append(jax.ShapeDtypeStruct((B, S, N // head_dim, head_dim), dt))
                out_specs.append(pl.BlockSpec((1, tm, tn // head_dim, head_dim),
                                              lambda b, i, j: (b, i, j, 0)))
            else:
                out_shape.append(jax.ShapeDtypeStruct((B, S, N), dt))
                out_specs.append(pl.BlockSpec((1, tm, tn), lambda b, i, j: (b, i, j)))
    if has_gate:
        out_shape.append(jax.ShapeDtypeStruct((B, S, G), F32))
        out_specs.append(pl.BlockSpec((1, tm, G), lambda b, i, j: (b, i, 0)))
    kern = functools.partial(_dual_norm_matmul_kernel, n_a=len(a_dtypes), n_b=len(b_dtypes),
                             head_dim=head_dim, has_gate=has_gate)
    outs = pl.pallas_call(
        kern, grid=(B, S // tm, N // tn), in_specs=in_specs, out_specs=out_specs,
        out_shape=out_shape, scratch_shapes=[pltpu.VMEM((tm, D), BF16)],
        compiler_params=_params("parallel", "parallel", "arbitrary"), name="dual_norm_matmul",
    )(*args)
    return [o.reshape(lead_shape + o.shape[2:]) for o in outs]


def _matmul_residual_kernel(a_ref, w_ref, x_ref, o_ref):
    o_ref[0] = x_ref[0] + _dot(a_ref[0], w_ref[...])


def _matmul_residual(a, w, x):
    out_shape = x.shape
    a = a.reshape(1, -1, a.shape[-1])
    x = x.reshape(1, -1, x.shape[-1])
    B, S, K = a.shape
    N = w.shape[1]
    tm = _tile(S, 1024)
    tn = _tile(N, 1024, 128)
    return pl.pallas_call(
        _matmul_residual_kernel, grid=(B, S // tm, N // tn),
        in_specs=[pl.BlockSpec((1, tm, K), lambda b, i, j: (b, i, 0)),
                  pl.BlockSpec((K, tn), lambda b, i, j: (0, j)),
                  pl.BlockSpec((1, tm, tn), lambda b, i, j: (b, i, j))],
        out_specs=pl.BlockSpec((1, tm, tn), lambda b, i, j: (b, i, j)),
        out_shape=jax.ShapeDtypeStruct((B, S, N), F32),
        compiler_params=_params("parallel", "parallel", "arbitrary"), name="matmul_residual",
    )(a, w, x).reshape(out_shape)


def _fold_gate_kernel(wq_ref, wk_ref, wv_ref, wg_ref, a_ref, b_ref, *, k_scale):
    def hdot(a, b):
        return jnp.dot(a, b, precision=HIGHEST, preferred_element_type=F32)

    a_ref[0, 0] = hdot(wq_ref[0, 0], wg_ref[0, 0, 0]) + hdot(wk_ref[0, 0], wg_ref[0, 1, 0]) * k_scale
    b_ref[0, 0] = hdot(wv_ref[0, 0], wg_ref[0, 2, 0])


def _fold_gate_weights(wq, wk, wv, w_gate):
    NA, H, DH, _ = wq.shape
    G = w_gate.shape[-1]
    w_spec = pl.BlockSpec((1, 1, DH, DH), lambda i, h: (i, h, 0, 0))
    o_spec = pl.BlockSpec((1, 1, DH, G), lambda i, h: (i, h, 0, 0))
    out = jax.ShapeDtypeStruct((NA, H, DH, G), F32)
    return pl.pallas_call(
        functools.partial(_fold_gate_kernel, k_scale=float(DH) ** -0.5), grid=(NA, H),
        in_specs=[w_spec, w_spec, w_spec,
                  pl.BlockSpec((1, 3, 1, DH, G), lambda i, h: (i, 0, h, 0, 0))],
        out_specs=[o_spec, o_spec], out_shape=[out, out],
        compiler_params=_params("parallel", "parallel"), name="fold_gate_weights",
    )(wq, wk, wv, w_gate.reshape(NA, 3, H, DH, G))


def _conv_gates_kernel(xm_ref, buf_ref, cw_ref, cb_ref, wa_ref, wb_ref, bg_ref,
                       xc_ref, g_ref, seq_ref, *, rows, cum_rows, conv_w, n_heads):
    h = pl.program_id(1)
    S = xm_ref.shape[1]
    seq_ref[0:CONV_PAD, :] = buf_ref[0]
    seq_ref[CONV_PAD:CONV_PAD + S, :] = xm_ref[0].astype(F32)
    for r in range(S // rows):
        base = r * rows
        acc = jnp.broadcast_to(cb_ref[0], (rows, cb_ref.shape[-1]))
        for i in range(conv_w):
            start = base + CONV_PAD - (conv_w - 1) + i
            acc = acc + seq_ref[start:start + rows, :] * cw_ref[0, i:i + 1, :]
        xc = _silu(acc).astype(BF16)
        xc_ref[0, base:base + rows, :] = xc
        part = _dot(xc, wa_ref[0]) + _dot(xm_ref[0, base:base + rows, :], wb_ref[0])

        @pl.when(h == 0)
        def _():
            g_ref[0, base:base + rows, :] = part + bg_ref[...]

        @pl.when(h > 0)
        def _():
            g_ref[0, base:base + rows, :] = g_ref[0, base:base + rows, :] + part

    @pl.when(h == n_heads - 1)
    def _():
        r = lax.broadcasted_iota(jnp.int32, (cum_rows, cum_rows), 0)
        c = lax.broadcasted_iota(jnp.int32, (cum_rows, cum_rows), 1)
        trilf = (c <= r).astype(F32)
        lane = lax.broadcasted_iota(jnp.int32, (cum_rows, g_ref.shape[-1]), 1)
        carry = jnp.zeros((1, g_ref.shape[-1]), F32)
        for i in range(S // cum_rows):
            g = g_ref[0, i * cum_rows:(i + 1) * cum_rows, :]
            lf = jnp.where(lane >= n_heads, _log_sigmoid(g), 0.0)
            cum = jnp.dot(trilf, lf, precision=HIGHEST, preferred_element_type=F32) + carry
            g_ref[0, i * cum_rows:(i + 1) * cum_rows, :] = jnp.where(lane >= n_heads, cum, g)
            carry = cum[cum_rows - 1:cum_rows, :]


def _conv_gates(xz, buf, conv_w, conv_b, w_xc, w_xm, b_gate):
    B, S, _ = xz.shape
    H, DH, G = w_xc.shape
    AI = H * DH
    CW = conv_w.shape[0]
    rows = _tile(S, 768)
    act_spec = pl.BlockSpec((1, S, DH), lambda b, h: (b, 0, h))
    w_spec = pl.BlockSpec((1, DH, G), lambda b, h: (h, 0, 0))
    kern = functools.partial(_conv_gates_kernel, rows=rows, cum_rows=_tile(S, 512, 8), conv_w=CW,
                             n_heads=H)
    return pl.pallas_call(
        kern, grid=(B, H),
        in_specs=[act_spec,
                  pl.BlockSpec((1, CONV_PAD, DH), lambda b, h: (b, 0, h)),
                  pl.BlockSpec((1, CW, DH), lambda b, h: (h, 0, 0)),
                  pl.BlockSpec((1, 1, DH), lambda b, h: (h, 0, 0)),
                  w_spec, w_spec,
                  pl.BlockSpec((1, G), lambda b, h: (0, 0))],
        out_specs=[act_spec, pl.BlockSpec((1, S, G), lambda b, h: (b, 0, 0))],
        out_shape=[jax.ShapeDtypeStruct((B, S, AI), BF16), jax.ShapeDtypeStruct((B, S, G), F32)],
        scratch_shapes=[pltpu.VMEM((S + CONV_PAD, DH), F32)],
        compiler_params=_params("parallel", "arbitrary"), name="conv_gates",
    )(xz, buf,
      conv_w.reshape(CW, H, DH).transpose(1, 0, 2).astype(F32),
      conv_b.reshape(H, 1, DH).astype(F32), w_xc, w_xm, b_gate.reshape(1, G).astype(F32))


def _in_conv_kernel(x_ref, g_ref, w_ref, wz_ref, buf_ref, cw_ref, cb_ref, wa_ref, wb_ref, bg_ref,
                    xm_ref, z_ref, xc_ref, gate_ref, xn_ref, hist_ref, *, conv_w, n_heads):
    i = pl.program_id(1)
    j = pl.program_id(2)
    tm = x_ref.shape[1]

    @pl.when(j == 0)
    def _():
        x = x_ref[0].astype(F32)
        ms = jnp.mean(x * x, axis=-1, keepdims=True)
        xn_ref[...] = (x * lax.rsqrt(ms + EPS) * g_ref[...]).astype(BF16)

    prev = jnp.where(i == 0, buf_ref[0], hist_ref[j])
    row = lax.broadcasted_iota(jnp.int32, prev.shape, 0)
    mid = tm // 2 // BF16_ROWS * BF16_ROWS
    parts = []
    for r0, r1 in ((0, mid), (mid, tm)) if mid else ((0, tm),):
        acc = _dot(xn_ref[r0:r1, :], w_ref[...])
        xm = acc.astype(BF16)
        xm_ref[0, r0:r1, :] = xm
        y = cb_ref[...] + acc * cw_ref[conv_w - 1:conv_w, :]
        for shift in range(1, conv_w):
            rolled = pltpu.roll(acc, shift, 0)
            head = jnp.where(row < shift, pltpu.roll(prev, shift, 0), rolled[0:CONV_PAD, :])
            shifted = jnp.concatenate([head, rolled[CONV_PAD:, :]], axis=0)
            y = y + shifted * cw_ref[conv_w - 1 - shift:conv_w - shift, :]
        xc = _silu(y).astype(BF16)
        xc_ref[0, r0:r1, :] = xc
        z_ref[0, r0:r1, :] = _dot(xn_ref[r0:r1, :], wz_ref[...]).astype(BF16)
        parts.append(_dot(xc, wa_ref[...]) + _dot(xm, wb_ref[...]))
        prev = acc[r1 - r0 - CONV_PAD:r1 - r0, :]
    hist_ref[j] = prev
    part = jnp.concatenate(parts, axis=0)

    @pl.when(j == 0)
    def _():
        gate_ref[0] = part + bg_ref[...]

    @pl.when(j > 0)
    def _():
        gate_ref[0] = gate_ref[0] + part

    @pl.when(j == pl.num_programs(2) - 1)
    def _():
        g = gate_ref[0]
        lane = lax.broadcasted_iota(jnp.int32, g.shape, 1)
        gate_ref[0] = jnp.where(lane >= n_heads, _log_sigmoid(g), g)


def _in_conv(x, norm_g, w_in, buf, conv_w, conv_b, w_xc, w_xm, b_gate):
    B, S, D = x.shape
    H, DH, G = w_xc.shape
    AI = H * DH
    CW = conv_w.shape[0]
    tm = _tile(S, 1024)
    tn = _tile(AI, 1024, LANES)
    act_spec = pl.BlockSpec((1, tm, tn), lambda b, i, j: (b, i, j))
    act = jax.ShapeDtypeStruct((B, S, AI), BF16)
    kern = functools.partial(_in_conv_kernel, conv_w=CW, n_heads=H)
    return pl.pallas_call(
        kern, grid=(B, S // tm, AI // tn),
        in_specs=[pl.BlockSpec((1, tm, D), lambda b, i, j: (b, i, 0)),
                  pl.BlockSpec((1, D), lambda b, i, j: (0, 0)),
                  pl.BlockSpec((D, tn), lambda b, i, j: (0, j)),
                  pl.BlockSpec((D, tn), lambda b, i, j: (0, AI // tn + j)),
                  pl.BlockSpec((1, CONV_PAD, tn), lambda b, i, j: (b, 0, j)),
                  pl.BlockSpec((CW, tn), lambda b, i, j: (0, j)),
                  pl.BlockSpec((1, tn), lambda b, i, j: (0, j)),
                  pl.BlockSpec((tn, G), lambda b, i, j: (j, 0)),
                  pl.BlockSpec((tn, G), lambda b, i, j: (j, 0)),
                  pl.BlockSpec((1, G), lambda b, i, j: (0, 0))],
        out_specs=[act_spec, act_spec, act_spec,
                   pl.BlockSpec((1, tm, G), lambda b, i, j: (b, i, 0))],
        out_shape=[act, act, act, jax.ShapeDtypeStruct((B, S, G), F32)],
        scratch_shapes=[pltpu.VMEM((tm, D), BF16), pltpu.VMEM((AI // tn, CONV_PAD, tn), F32)],
        compiler_params=_params("parallel", "arbitrary", "arbitrary"), name="in_conv",
    )(x, norm_g.reshape(1, D).astype(F32), w_in, w_in, buf, conv_w.astype(F32),
      conv_b.reshape(1, AI).astype(F32), w_xc.reshape(AI, G), w_xm.reshape(AI, G),
      b_gate.reshape(1, G).astype(F32))


def _mlstm_core(xm, xc, wq, wk, wv, g_rows, g_cols, head, n_heads, C_ref, n_ref, m_ref, b_ref):
    L = xm.shape[0]
    q = _dot(xc, wq).astype(BF16)
    k = _dot(xc, wk).astype(BF16)
    v = _dot(xm, wv).astype(BF16)
    row = lax.broadcasted_iota(jnp.int32, (L, L), 0)
    col = lax.broadcasted_iota(jnp.int32, (L, L), 1)
    tril = col <= row
    lane = lax.broadcasted_iota(jnp.int32, g_cols.shape, 1)
    ig_col = jnp.sum(jnp.where(lane == head, g_cols, 0.0), axis=-1, keepdims=True)
    f_col = jnp.sum(jnp.where(lane == head + n_heads, g_cols, 0.0), axis=-1, keepdims=True)
    ig_row = g_rows[0:1, :]
    f_row = g_rows[1:2, :]
    m_prev = m_ref[...]
    f_prev = b_ref[...]
    log_d = jnp.where(tril, f_col - f_row + ig_row, NEG_INF)
    inter = f_col - f_prev + m_prev
    m_t = jnp.maximum(jnp.max(log_d, axis=-1, keepdims=True), inter)
    d = jnp.exp(log_d - m_t)
    a = jnp.exp(inter - m_t)
    s = _dot_nt(q, k) * d
    C = C_ref[...]
    n = n_ref[...]
    num = _dot(s.astype(BF16), v) + a * _dot(q, C.astype(BF16))
    qn = _dot_nt(q, jnp.broadcast_to(n, (8, n.shape[1])).astype(BF16))[:, 0:1]
    den = jnp.sum(s, axis=-1, keepdims=True) + a * qn
    m_new = m_t[L - 1:L, :]
    f_last = f_row[:, L - 1:L]
    decay = jnp.exp(f_last - f_prev + m_prev - m_new)
    w_col = jnp.exp(f_last - f_col + ig_col - m_new)
    C_ref[...] = decay * C + _dot_tn(k, (w_col * v.astype(F32)).astype(BF16))
    n_ref[...] = decay * n + jnp.sum(w_col * k.astype(F32), axis=0, keepdims=True)
    m_ref[...] = m_new
    b_ref[...] = f_last
    return num * (1.0 / jnp.maximum(jnp.abs(den), jnp.exp(-m_t)))


def _mlstm_finish(hid, xc, z, og, skip):
    hc = hid - jnp.mean(hid, axis=-1, keepdims=True)
    hn = hc * lax.rsqrt(jnp.mean(hc * hc, axis=-1, keepdims=True) + EPS) * og
    return ((hn + skip * xc.astype(F32)) * _silu(z.astype(F32))).astype(BF16)


def _mlstm_kernel(*refs, lead, chunk, n_chunks, n_heads, has_state, merge_lead):
    xm_ref, xc_ref, z_ref, wq_ref, wk_ref, wv_ref = refs[:6]
    pos = 6
    if lead:
        gl_ref = refs[pos]
        pos += 1
    gm_ref, gc_ref, og_ref, skip_ref = refs[pos:pos + 4]
    pos += 4
    if has_state:
        c0_ref, n0_ref, m0_ref = refs[pos:pos + 3]
        pos += 3
    pos += 1
    o_ref, C_ref, n_ref, m_ref, b_ref, hid_ref = refs[pos:pos + 6]
    C_st, n_st, m_st = C_ref.at[0, 0, 0], n_ref.at[0, 0], m_ref.at[0, 0]
    if has_state:
        C_st[...] = c0_ref[0, 0, 0]
        n_st[...] = n0_ref[0, 0, 0]
        m_st[...] = m0_ref[0, 0, 0]
    else:
        C_st[...] = jnp.zeros(C_st.shape, F32)
        n_st[...] = jnp.zeros(n_st.shape, F32)
        m_st[...] = jnp.zeros(m_st.shape, F32)
    b_ref[...] = jnp.zeros(b_ref.shape, F32)
    head = pl.program_id(1)

    def chunk_rows(c):
        return pl.ds(pl.multiple_of(lead + c * chunk, BF16_ROWS), chunk)

    def core(rows, g_rows):
        return _mlstm_core(xm_ref[0, rows, :], xc_ref[0, rows, :], wq_ref[0], wk_ref[0], wv_ref[0],
                           g_rows, gc_ref[0, rows, :], head, n_heads, C_st, n_st, m_st, b_ref)

    def finish(rows, hid):
        o_ref[0, rows, :] = _mlstm_finish(hid, xc_ref[0, rows, :], z_ref[0, rows, :], og_ref[0],
                                          skip_ref[0])

    if merge_lead:
        first = pl.ds(0, lead + chunk)
        finish(first, core(first, gl_ref[0, 0]))
        hid_ref[0] = core(chunk_rows(1), gm_ref[0, 0, 1])

        def pair(p, carry):
            c = 2 * p + 2
            hid_ref[1] = core(chunk_rows(c), gm_ref[0, 0, c])
            finish(chunk_rows(c - 1), hid_ref[0])
            hid_ref[0] = core(chunk_rows(c + 1), gm_ref[0, 0, c + 1])
            finish(chunk_rows(c), hid_ref[1])
            return carry

        lax.fori_loop(0, n_chunks // 2 - 1, pair, 0)
        finish(chunk_rows(n_chunks - 1), hid_ref[0])
    else:
        if lead:
            finish(pl.ds(0, lead), core(pl.ds(0, lead), gl_ref[0, 0]))

        def body(c, carry):
            finish(chunk_rows(c), core(chunk_rows(c), gm_ref[0, 0, c]))
            return carry

        lax.fori_loop(0, n_chunks, body, 0, unroll=_tile(n_chunks, 2, 1))


def _mlstm(xm, z, z_head0, xc, gates, wq, wk, wv, out_g, skip, state, lead, chunk, c_stack, layer,
           n_layers):
    B, S, AI = xc.shape
    H, DH = out_g.shape
    n_chunks = (S - lead) // chunk
    assert lead + n_chunks * chunk == S
    g = gates.reshape(B, S, 2, H).transpose(0, 3, 2, 1)
    g_main = g[..., lead:].reshape(B, H, 2, n_chunks, chunk).transpose(0, 1, 3, 2, 4)
    hps = 1
    act_spec = pl.BlockSpec((1, S, hps * DH), lambda b, h: (b, 0, h))
    w_spec = pl.BlockSpec((hps, DH, DH), lambda b, h: (h, 0, 0))
    in_specs = [act_spec, act_spec,
                pl.BlockSpec((1, S, hps * DH), lambda b, h: (b, 0, z_head0 // hps + h)),
                w_spec, w_spec, w_spec]
    args = [xm, xc, z, wq, wk, wv]
    merge_lead = bool(lead) and n_chunks % 2 == 0
    if lead:
        n_first = lead + chunk if merge_lead else lead
        in_specs.append(pl.BlockSpec((1, hps, 2, n_first), lambda b, h: (b, h, 0, 0)))
        args.append(g[..., :n_first])
    in_specs += [pl.BlockSpec((1, hps, n_chunks, 2, chunk), lambda b, h: (b, h, 0, 0, 0)),
                 pl.BlockSpec((1, S, 2 * H), lambda b, h: (b, 0, 0)),
                 pl.BlockSpec((hps, 1, DH), lambda b, h: (h, 0, 0)),
                 pl.BlockSpec((hps, 1, DH), lambda b, h: (h, 0, 0))]
    args += [g_main, gates, out_g.reshape(H, 1, DH).astype(F32), skip.reshape(H, 1, DH).astype(F32)]
    n_spec = pl.BlockSpec((1, hps, 1, DH), lambda b, h: (b, h, 0, 0))
    m_spec = pl.BlockSpec((1, hps, 1, 1), lambda b, h: (b, h, 0, 0))
    has_state = state is not None
    if has_state:
        C0, n0, m0 = state
        NL = C0.shape[0]
        in_specs += [pl.BlockSpec((1, 1, hps, DH, DH), lambda b, h: (layer, b, h, 0, 0)),
                     pl.BlockSpec((1, 1, hps, 1, DH), lambda b, h: (layer, b, h, 0, 0)),
                     pl.BlockSpec((1, 1, hps, 1, 1), lambda b, h: (layer, b, h, 0, 0))]
        args += [C0.astype(F32), n0.reshape(NL, B, H, 1, DH).astype(F32),
                 m0.reshape(NL, B, H, 1, 1).astype(F32)]
    if c_stack is None:
        c_stack = jnp.zeros((n_layers, B, H, DH, DH), F32)
    aliases = {len(args): 1}
    in_specs.append(pl.BlockSpec(memory_space=pl.ANY))
    args.append(c_stack)
    kern = functools.partial(_mlstm_kernel, lead=lead, chunk=chunk, n_chunks=n_chunks, n_heads=H,
                             has_state=has_state, merge_lead=merge_lead)
    hz, C, n, m = pl.pallas_call(
        kern, grid=(B, H // hps), in_specs=in_specs,
        out_specs=[act_spec,
                   pl.BlockSpec((1, 1, hps, DH, DH), lambda b, h: (layer, b, h, 0, 0)),
                   n_spec, m_spec],
        out_shape=[jax.ShapeDtypeStruct((B, S, AI), BF16),
                   jax.ShapeDtypeStruct((n_layers, B, H, DH, DH), F32),
                   jax.ShapeDtypeStruct((B, H, 1, DH), F32),
                   jax.ShapeDtypeStruct((B, H, 1, 1), F32)],
        scratch_shapes=[pltpu.VMEM((1, 1), F32), pltpu.VMEM((2, chunk, DH), F32)],
        input_output_aliases=aliases,
        compiler_params=_params("parallel", "parallel"), name="mlstm",
    )(*args)
    return hz, C, n.reshape(B, H, DH), m.reshape(B, H)


def _cumsum_kernel(x_ref, o_ref, *, scale, first_lane):
    x = x_ref[0]
    S = x.shape[0]
    summed = lax.broadcasted_iota(jnp.int32, x.shape, 1) >= first_lane
    row = lax.broadcasted_iota(jnp.int32, x.shape, 0)
    y = jnp.where(summed, x, 0.0)
    step = 1
    while step < S:
        y = y + jnp.where(row >= step, pltpu.roll(y, step, 0), 0.0)
        step *= 2
    o_ref[0] = jnp.where(summed, y * scale, x)


def _cumsum(x, scale=1.0, first_lane=0):
    B, S, G = x.shape
    spec = pl.BlockSpec((1, S, G), lambda b: (b, 0, 0))
    return pl.pallas_call(
        functools.partial(_cumsum_kernel, scale=scale, first_lane=first_lane),
        grid=(B,), in_specs=[spec], out_specs=spec, out_shape=jax.ShapeDtypeStruct((B, S, G), F32),
        compiler_params=_params("parallel"), name="cumsum",
    )(x)


def _lane_fold(op, acc, x):
    for c in range(x.shape[1] // LANES):
        acc = op(acc, x[:, c * LANES:(c + 1) * LANES])
    return acc


def _fox_prompt_kernel(q_ref, z_ref, k_ref, v_ref, fq_ref, fkl_ref, fkm_ref, o_ref,
                       s_ref, sl_ref, fq_rep, m_ref, l_ref, acc_ref, *, lead, blk, n_blk, heads,
                       dim):
    n_lane = blk // LANES
    cols = [slice(hh * dim, (hh + 1) * dim) for hh in range(heads)]

    def key_rows(j):
        return pl.ds(pl.multiple_of(lead + j * blk, BF16_ROWS), blk)

    def tiled(x):
        return jnp.concatenate([x] * n_lane, axis=1)

    def q_block(i, carry):
        qrows = pl.ds(pl.multiple_of(i * blk, blk), blk)

        def logits(hh, j):
            return (_dot_nt(q_ref[0, qrows, cols[hh]], k_ref[0, key_rows(j), cols[hh]])
                    + tiled(fq_rep[hh]) - fkm_ref[0, 0, j, hh:hh + 1, :])

        for hh in range(heads):
            fq_rep[hh] = jnp.broadcast_to(fq_ref[0, 0, qrows, hh:hh + 1], (blk, LANES))
        if lead:
            for hh in range(heads):
                s = (_dot_nt(q_ref[0, qrows, cols[hh]], k_ref[0, 0:LANES, cols[hh]])
                     + fq_rep[hh] - fkl_ref[0, 0, hh:hh + 1, :])
                sl_ref[hh] = s
                m_ref[hh] = s
        else:
            m_ref[...] = jnp.full(m_ref.shape, NEG_INF, F32)

        def pass1(j, c):
            for hh in range(heads):
                s = logits(hh, j)
                s_ref[hh, j] = s
                m_ref[hh] = _lane_fold(jnp.maximum, m_ref[hh], s)
            return c

        lax.fori_loop(0, i, pass1, 0)
        row = lax.broadcasted_iota(jnp.int32, (blk, blk), 0)
        col = lax.broadcasted_iota(jnp.int32, (blk, blk), 1)
        for hh in range(heads):
            s = jnp.where(col <= row, logits(hh, i), NEG_INF)
            s_ref[hh, i] = s
            m = jnp.max(_lane_fold(jnp.maximum, m_ref[hh], s), axis=-1, keepdims=True)
            m_ref[hh] = jnp.broadcast_to(m, (blk, LANES))
        if lead:
            for hh in range(heads):
                p = jnp.exp2(sl_ref[hh] - m_ref[hh])
                l_ref[hh] = p
                acc_ref[hh] = _dot(p.astype(BF16), v_ref[0, 0:LANES, cols[hh]])
        else:
            l_ref[...] = jnp.zeros(l_ref.shape, F32)
            acc_ref[...] = jnp.zeros(acc_ref.shape, F32)

        def pass2(j, c):
            for hh in range(heads):
                p = jnp.exp2(s_ref[hh, j] - tiled(m_ref[hh]))
                l_ref[hh] = _lane_fold(jnp.add, l_ref[hh], p)
                acc_ref[hh] += _dot(p.astype(BF16), v_ref[0, key_rows(j), cols[hh]])
            return c

        lax.fori_loop(0, i + 1, pass2, 0)
        for hh in range(heads):
            l = jnp.sum(l_ref[hh], axis=-1, keepdims=True)
            o_ref[0, qrows, cols[hh]] = (acc_ref[hh] * (1.0 / l)
                                         * _silu(z_ref[0, qrows, cols[hh]].astype(F32))).astype(BF16)
        return carry

    lax.fori_loop(0, n_blk, q_block, 0)


def _fox_prompt(q, z, kb, vb, f_cum, lead, heads_per_step=4, blk=512):
    B, SQ, W = q.shape
    NH = f_cum.shape[-1]
    dim = W // NH
    hg = min(heads_per_step, NH)
    G = NH // hg
    blk = _tile(SQ, blk, LANES)
    n_blk = SQ // blk
    SK = lead + SQ
    assert lead <= LANES <= SK
    fq = f_cum[:, lead:].reshape(B, SQ, G, hg).transpose(0, 2, 1, 3)
    fk = f_cum.transpose(0, 2, 1).reshape(B, G, hg, SK)
    fk_lead = jnp.pad(fk[..., :lead], ((0, 0), (0, 0), (0, 0), (0, LANES - lead)),
                      constant_values=float("inf"))
    fk_main = fk[..., lead:].reshape(B, G, hg, n_blk, blk).transpose(0, 1, 3, 2, 4)
    wcols = hg * dim
    kern = functools.partial(_fox_prompt_kernel, lead=lead, blk=blk, n_blk=n_blk, heads=hg, dim=dim)
    return pl.pallas_call(
        kern, grid=(B, G),
        in_specs=[pl.BlockSpec((1, SQ, wcols), lambda b, g: (b, 0, g)),
                  pl.BlockSpec((1, SQ, wcols), lambda b, g: (b, 0, g)),
                  pl.BlockSpec((1, SK, wcols), lambda b, g: (b, 0, g)),
                  pl.BlockSpec((1, SK, wcols), lambda b, g: (b, 0, g)),
                  pl.BlockSpec((1, 1, SQ, hg), lambda b, g: (b, g, 0, 0)),
                  pl.BlockSpec((1, 1, hg, LANES), lambda b, g: (b, g, 0, 0)),
                  pl.BlockSpec((1, 1, n_blk, hg, blk), lambda b, g: (b, g, 0, 0, 0))],
        out_specs=pl.BlockSpec((1, SQ, wcols), lambda b, g: (b, 0, g)),
        out_shape=jax.ShapeDtypeStruct((B, SQ, W), BF16),
        scratch_shapes=[pltpu.VMEM((hg, n_blk, blk, blk), F32)]
        + [pltpu.VMEM((hg, blk, LANES), F32)] * 4 + [pltpu.VMEM((hg, blk, dim), F32)],
        compiler_params=_params("parallel", "parallel"), name="fox_prompt",
    )(q, z, kb, vb, fq, fk_lead, fk_main)


def _pack_heads_kernel(x_ref, o_ref):
    o_ref[0] = pltpu.einshape("khd->k(hd)", x_ref[0]).astype(o_ref.dtype)


def _pack_heads(x, dtype):
    B, P, NH, dim = x.shape
    rows = _tile(P, 512)
    return pl.pallas_call(
        _pack_heads_kernel, grid=(B, P // rows),
        in_specs=[pl.BlockSpec((1, rows, NH, dim), lambda b, i: (b, i, 0, 0))],
        out_specs=pl.BlockSpec((1, rows, NH * dim), lambda b, i: (b, i, 0)),
        out_shape=jax.ShapeDtypeStruct((B, P, NH * dim), dtype),
        compiler_params=_params("parallel", "parallel"), name="pack_heads",
    )(x)


def _fox_decode_kernel(q_ref, z_ref, ck_ref, cv_ref, k_ref, v_ref, fq_ref, fkc_ref, fkn_ref, o_ref,
                       *, heads, dim):
    Q = q_ref.shape[1]
    row = lax.broadcasted_iota(jnp.int32, (Q, Q), 0)
    col = lax.broadcasted_iota(jnp.int32, (Q, Q), 1)
    outs = []
    for hh in range(heads):
        cs = slice(hh * dim, (hh + 1) * dim)
        q = q_ref[0, :, cs]
        fq = fq_ref[0, 0, :, hh:hh + 1]
        s_c = _dot_nt(q, ck_ref[0, :, cs].astype(BF16)) + fq - fkc_ref[0, 0, hh:hh + 1, :]
        s_n = jnp.where(col <= row, _dot_nt(q, k_ref[0, :, cs]) + fq - fkn_ref[0, 0, hh:hh + 1, :],
                        NEG_INF)
        m = jnp.maximum(jnp.max(s_c, axis=-1, keepdims=True), jnp.max(s_n, axis=-1, keepdims=True))
        p_c = jnp.exp2(s_c - m)
        p_n = jnp.exp2(s_n - m)
        l = jnp.sum(p_c, axis=-1, keepdims=True) + jnp.sum(p_n, axis=-1, keepdims=True)
        acc = _dot(p_c.astype(BF16), cv_ref[0, :, cs].astype(BF16)) + _dot(p_n.astype(BF16), v_ref[0, :, cs])
        outs.append((acc * (1.0 / l) * _silu(z_ref[0, :, cs].astype(F32))).astype(BF16))
    o_ref[0] = jnp.concatenate(outs, axis=-1)


def _fox_decode(q, z, cache_k, cache_v, kb, vb, f_cum, heads_per_step=4):
    B, Q, W = q.shape
    P = cache_k.shape[1]
    NH = f_cum.shape[-1]
    dim = W // NH
    hg = min(heads_per_step, NH)
    G = NH // hg
    fq = f_cum[:, P:].reshape(B, Q, G, hg).transpose(0, 2, 1, 3)
    fk = f_cum.transpose(0, 2, 1).reshape(B, G, hg, P + Q)
    wcols = hg * dim
    kern = functools.partial(_fox_decode_kernel, heads=hg, dim=dim)
    return pl.pallas_call(
        kern, grid=(B, G),
        in_specs=[pl.BlockSpec((1, Q, wcols), lambda b, g: (b, 0, g)),
                  pl.BlockSpec((1, Q, wcols), lambda b, g: (b, 0, g)),
                  pl.BlockSpec((1, P, wcols), lambda b, g: (b, 0, g)),
                  pl.BlockSpec((1, P, wcols), lambda b, g: (b, 0, g)),
                  pl.BlockSpec((1, Q, wcols), lambda b, g: (b, 0, g)),
                  pl.BlockSpec((1, Q, wcols), lambda b, g: (b, 0, g)),
                  pl.BlockSpec((1, 1, Q, hg), lambda b, g: (b, g, 0, 0)),
                  pl.BlockSpec((1, 1, hg, P), lambda b, g: (b, g, 0, 0)),
                  pl.BlockSpec((1, 1, hg, Q), lambda b, g: (b, g, 0, 0))],
        out_specs=pl.BlockSpec((1, Q, wcols), lambda b, g: (b, 0, g)),
        out_shape=jax.ShapeDtypeStruct((B, Q, W), BF16),
        compiler_params=_params("parallel", "parallel"), name="fox_decode",
    )(q, z, cache_k, cache_v, kb, vb, fq, fk[..., :P], fk[..., P:])


def _mlstm_layer(x, conv_hist, state, w, lead, chunk, c_stack, layer, n_layers):
    (norm_g, w_in, conv_w, conv_b, wq, wk, wv, w_xc, w_xm, b_gate, out_g, skip, w_out) = w
    B, S, _ = x.shape
    H, DH, _ = wq.shape
    AI = H * DH
    CW = conv_w.shape[0]
    assert S >= CW - 1 and CW - 1 <= CONV_PAD
    buf = jnp.zeros((B, CONV_PAD, AI), F32)
    if conv_hist is not None:
        buf = buf.at[:, CONV_PAD - (CW - 1):].set(conv_hist.astype(F32))
    if S >= MIN_FUSED_CONV_ROWS:
        wb = w_in.astype(BF16)
        xm, z, xc, gates = _in_conv(x, norm_g, wb, buf, conv_w, conv_b,
                                    w_xc.astype(BF16), w_xm.astype(BF16), b_gate)
        gates = _cumsum(gates, first_lane=H)
        z_head0 = 0
    else:
        xm = _norm_matmul(x, norm_g, w_in.astype(BF16))
        xc, gates = _conv_gates(xm, buf, conv_w, conv_b, w_xc.astype(BF16), w_xm.astype(BF16),
                                b_gate)
        z, z_head0 = xm, H
    hz, c_stack, n, m = _mlstm(
        xm, z, z_head0, xc, gates, wq.astype(BF16), (wk * float(DH) ** -0.5).astype(BF16),
        wv.astype(BF16), out_g, skip.reshape(H, DH), state, lead, chunk, c_stack, layer, n_layers)
    x_new = _matmul_residual(hz, w_out.astype(BF16), x)
    new_hist = xm[:, S - (CW - 1):, :AI].astype(F32)
    return x_new, new_hist, c_stack, n, m


def _shared_kv(x, kv_norm_g, w_kvf, b_f, k_norm_g):
    NH, dim = k_norm_g.shape
    W = NH * dim
    wb = w_kvf.astype(BF16)
    k32, kb, v32, vb, logf = _dual_norm_matmul(
        x, kv_norm_g, wb, W, k_norm_g.reshape(1, W), a_dtypes=[F32, BF16],
        b_dtypes=[F32, BF16], head_dim=dim, split_heads_first=True, gate_w=wb[:, 2 * W:],
        gate_b=b_f)
    return k32, kb, v32, vb, logf


def _fox_in(x, norm_g, w_in, q_norm_g):
    NH, dim = q_norm_g.shape
    W = NH * dim
    wb = w_in.astype(BF16)
    gain = q_norm_g.reshape(1, W).astype(F32) * (float(dim) ** -0.5 * LOG2E)
    q, z = _dual_norm_matmul(x, norm_g, wb, W, gain, a_dtypes=[BF16],
                             b_dtypes=[BF16], head_dim=dim)
    return q, z


def kernel(x_prompt, x_sample, cache_k, cache_v, cache_logf, state_C, state_n, state_m, state_conv,
           meta_tokens, a_norm_g, a_w_in, a_conv_w, a_conv_b, a_wq, a_wk, a_wv, a_w_gate, a_b_gate,
           a_out_g, a_skip, a_w_out, kv_norm_g, w_kvf, b_f, k_norm_g, b_norm_g, b_w_in, q_norm_g,
           b_w_out):
    B, SEQ, D = x_prompt.shape
    DB, DEC, _ = x_sample.shape
    NM = meta_tokens.shape[0]
    N_A = a_norm_g.shape[0]
    N_B = b_norm_g.shape[0]
    NH, dim = k_norm_g.shape
    W = NH * dim
    P = cache_k.shape[1]
    chunk = _tile(SEQ, 256)

    xp = jnp.concatenate(
        [jnp.broadcast_to(meta_tokens.astype(x_prompt.dtype)[None], (B, NM, D)), x_prompt], axis=1)
    xs = x_sample
    p_state, s_state = [], []
    p_C = s_C = None
    w_xc, w_xm = _fold_gate_weights(a_wq, a_wk, a_wv, a_w_gate)
    for i in range(N_A):
        w = (a_norm_g[i], a_w_in[i], a_conv_w[i], a_conv_b[i], a_wq[i], a_wk[i], a_wv[i],
             w_xc[i], w_xm[i], a_b_gate[i], a_out_g[i], a_skip[i], a_w_out[i])
        xp, hist, p_C, n, m = _mlstm_layer(xp, None, None, w, NM, chunk, p_C, i, N_A)
        p_state.append((hist, n, m))
        xs, hist, s_C, n, m = _mlstm_layer(xs, state_conv[i], (state_C, state_n, state_m),
                                           w, 0, DEC, s_C, i, N_A)
        s_state.append((hist, n, m))

    pk32, pkb, pv32, pvb, p_logf = _shared_kv(xp, kv_norm_g, w_kvf, b_f, k_norm_g)
    sk32, skb, sv32, svb, s_logf = _shared_kv(xs, kv_norm_g, w_kvf, b_f, k_norm_g)
    fp = _cumsum(p_logf, LOG2E)
    fs = _cumsum(jnp.concatenate([cache_logf.astype(F32), s_logf], axis=1), LOG2E)
    ck = _pack_heads(cache_k, BF16)
    cv = _pack_heads(cache_v, BF16)
    xp = xp[:, NM:]
    for j in range(N_B):
        q, z = _fox_in(xp, b_norm_g[j], b_w_in[j], q_norm_g[j])
        xp = _matmul_residual(_fox_prompt(q, z, pkb, pvb, fp, NM), b_w_out[j].astype(BF16), xp)
        q, z = _fox_in(xs, b_norm_g[j], b_w_in[j], q_norm_g[j])
        xs = _matmul_residual(_fox_decode(q, z, ck, cv, skb, svb, fs), b_w_out[j].astype(BF16), xs)

    def stack(states, idx):
        return jnp.stack([st[idx] for st in states])

    return (xp, xs, pk32, pv32, p_logf,
            p_C, stack(p_state, 1), stack(p_state, 2), stack(p_state, 0),
            sk32, sv32, s_logf,
            s_C, stack(s_state, 1), stack(s_state, 2), stack(s_state, 0))
```

```python
import functools

import jax
import jax.numpy as jnp
from jax import lax
from jax.experimental import pallas as pl
from jax.experimental.pallas import tpu as pltpu

F32 = jnp.float32
BF16 = jnp.bfloat16
EPS = 1e-6
CONV_PAD = 8
BF16_ROWS = 16
LANES = 128
MIN_FUSED_CONV_ROWS = 256
VMEM_LIMIT_BYTES = 56 * 1024 * 1024
HIGHEST = lax.Precision.HIGHEST
NEG_INF = float("-inf")
LOG2E = 1.4426950408889634


def _params(*sem):
    return pltpu.CompilerParams(dimension_semantics=sem, vmem_limit_bytes=VMEM_LIMIT_BYTES)


def _tile(n, cap, mult=BF16_ROWS):
    best = None
    for t in range(mult, min(n, cap) + 1, mult):
        if n % t == 0:
            best = t
    return best if best is not None else n


def _dot(a, b):
    return jnp.dot(a, b, preferred_element_type=F32)


def _dot_nt(a, b):
    return lax.dot_general(a, b, (((1,), (1,)), ((), ())), preferred_element_type=F32)


def _dot_tn(a, b):
    return lax.dot_general(a, b, (((0,), (0,)), ((), ())), preferred_element_type=F32)


def _silu(x):
    return x / (1.0 + jnp.exp(-x))


def _log_sigmoid(x):
    return jnp.minimum(x, 0.0) - jnp.log1p(jnp.exp(-jnp.abs(x)))


def _norm_matmul_kernel(x_ref, g_ref, w_ref, o_ref, xn_ref):
    @pl.when(pl.program_id(2) == 0)
    def _():
        x = x_ref[0].astype(F32)
        ms = jnp.mean(x * x, axis=-1, keepdims=True)
        xn_ref[...] = (x * lax.rsqrt(ms + EPS) * g_ref[...]).astype(BF16)

    o_ref[0] = _dot(xn_ref[...], w_ref[...]).astype(o_ref.dtype)


def _norm_matmul(x, g, w):
    lead_shape = x.shape[:2]
    x = x.reshape(1, -1, x.shape[-1])
    B, S, D = x.shape
    N = w.shape[1]
    tm = _tile(S, 1024)
    tn = _tile(N, 1024, LANES)
    out = pl.pallas_call(
        _norm_matmul_kernel, grid=(B, S // tm, N // tn),
        in_specs=[pl.BlockSpec((1, tm, D), lambda b, i, j: (b, i, 0)),
                  pl.BlockSpec((1, D), lambda b, i, j: (0, 0)),
                  pl.BlockSpec((D, tn), lambda b, i, j: (0, j))],
        out_specs=pl.BlockSpec((1, tm, tn), lambda b, i, j: (b, i, j)),
        out_shape=jax.ShapeDtypeStruct((B, S, N), BF16),
        scratch_shapes=[pltpu.VMEM((tm, D), BF16)],
        compiler_params=_params("parallel", "parallel", "arbitrary"), name="norm_matmul",
    )(x, g.reshape(1, D).astype(F32), w)
    return out.reshape(lead_shape + (N,))


def _dual_norm_matmul_kernel(*refs, n_a, n_b, head_dim, has_gate):
    x_ref, g_ref, wa_ref, gain_ref, wb_ref = refs[:5]
    pos = 5
    if has_gate:
        gw_ref, gb_ref = refs[pos:pos + 2]
        pos += 2
    a_refs = refs[pos:pos + n_a]
    b_refs = refs[pos + n_a:pos + n_a + n_b]
    pos += n_a + n_b
    if has_gate:
        gate_out_ref = refs[pos]
        pos += 1
    xn_ref = refs[pos]

    @pl.when(pl.program_id(2) == 0)
    def _():
        x = x_ref[0].astype(F32)
        ms = jnp.mean(x * x, axis=-1, keepdims=True)
        xn_ref[...] = (x * lax.rsqrt(ms + EPS) * g_ref[...]).astype(BF16)
        if has_gate:
            gate_out_ref[0] = _log_sigmoid(_dot(xn_ref[...], gw_ref[...]) + gb_ref[...])

    def store(out_refs, val):
        for o in out_refs:
            if len(o.shape) == 4:
                o[0] = val.reshape(o.shape[1:]).astype(o.dtype)
            else:
                o[0] = val.astype(o.dtype)

    acc = _dot(xn_ref[...], wa_ref[...])
    cols = []
    for c in range(acc.shape[1] // head_dim):
        blk = acc[:, c * head_dim:(c + 1) * head_dim]
        ms = jnp.mean(blk * blk, axis=-1, keepdims=True)
        cols.append(blk * lax.rsqrt(ms + EPS))
    store(a_refs, jnp.concatenate(cols, axis=-1) * gain_ref[...])
    store(b_refs, _dot(xn_ref[...], wb_ref[...]))


def _dual_norm_matmul(x, g, w, N, gain_a, *, a_dtypes, b_dtypes, head_dim, split_heads_first=False,
                      gate_w=None, gate_b=None):
    lead_shape = x.shape[:2]
    x = x.reshape(1, -1, x.shape[-1])
    B, S, D = x.shape
    tm = _tile(S, 1024)
    tn = _tile(N, 1024, LANES)
    has_gate = gate_w is not None
    in_specs = [pl.BlockSpec((1, tm, D), lambda b, i, j: (b, i, 0)),
                pl.BlockSpec((1, D), lambda b, i, j: (0, 0)),
                pl.BlockSpec((D, tn), lambda b, i, j: (0, j)),
                pl.BlockSpec((1, tn), lambda b, i, j: (0, j)),
                pl.BlockSpec((D, tn), lambda b, i, j: (0, N // tn + j))]
    args = [x, g.reshape(1, D).astype(F32), w, gain_a.astype(F32), w]
    if has_gate:
        G = gate_w.shape[1]
        in_specs += [pl.BlockSpec((D, G), lambda b, i, j: (0, 0)),
                     pl.BlockSpec((1, G), lambda b, i, j: (0, 0))]
        args += [gate_w, gate_b.reshape(1, G).astype(F32)]
    out_shape, out_specs = [], []
    for dtypes in (a_dtypes, b_dtypes):
        for k, dt in enumerate(dtypes):
            if split_heads_first and k == 0:
                out_shape.append(jax.ShapeDtypeStruct((B, S, N // head_dim, head_dim), dt))
                out_specs.append(pl.BlockSpec((1, tm, tn // head_dim, head_dim),
                                              lambda b, i, j: (b, i, j, 0)))
            else:
                out_shape.append(jax.ShapeDtypeStruct((B, S, N), dt))
                out_specs.append(pl.BlockSpec((1, tm, tn), lambda b, i, j: (b, i, j)))
    if has_gate:
        out_shape.append(jax.ShapeDtypeStruct((B, S, G), F32))
        out_specs.append(pl.BlockSpec((1, tm, G), lambda b, i, j: (b, i, 0)))
    kern = functools.partial(_dual_norm_matmul_kernel, n_a=len(a_dtypes), n_b=len(b_dtypes),
                             head_dim=head_dim, has_gate=has_gate)
    outs = pl.pallas_call(
        kern, grid=(B, S // tm, N // tn), in_specs=in_specs, out_specs=out_specs,
        out_shape=out_shape, scratch_shapes=[pltpu.VMEM((tm, D), BF16)],
        compiler_params=_params("parallel", "parallel", "arbitrary"), name="dual_norm_matmul",
    )(*args)
    return [o.reshape(lead_shape + o.shape[2:]) for o in outs]


def _matmul_residual_kernel(a_ref, w_ref, x_ref, o_ref):
    o_ref[0] = x_ref[0] + _dot(a_ref[0], w_ref[...])


def _matmul_residual(a, w, x):
    out_shape = x.shape
    a = a.reshape(1, -1, a.shape[-1])
    x = x.reshape(1, -1, x.shape[-1])
    B, S, K = a.shape
    N = w.shape[1]
    tm = _tile(S, 1024)
    tn = _tile(N, 1024, 128)
    return pl.pallas_call(
        _matmul_residual_kernel, grid=(B, S // tm, N // tn),
        in_specs=[pl.BlockSpec((1, tm, K), lambda b, i, j: (b, i, 0)),
                  pl.BlockSpec((K, tn), lambda b, i, j: (0, j)),
                  pl.BlockSpec((1, tm, tn), lambda b, i, j: (b, i, j))],
        out_specs=pl.BlockSpec((1, tm, tn), lambda b, i, j: (b, i, j)),
        out_shape=jax.ShapeDtypeStruct((B, S, N), F32),
        compiler_params=_params("parallel", "parallel", "arbitrary"), name="matmul_residual",
    )(a, w, x).reshape(out_shape)


def _fold_gate_kernel(wq_ref, wk_ref, wv_ref, wg_ref, a_ref, b_ref, *, k_scale):
    def hdot(a, b):
        return jnp.dot(a, b, precision=HIGHEST, preferred_element_type=F32)

    a_ref[0, 0] = hdot(wq_ref[0, 0], wg_ref[0, 0, 0]) + hdot(wk_ref[0, 0], wg_ref[0, 1, 0]) * k_scale
    b_ref[0, 0] = hdot(wv_ref[0, 0], wg_ref[0, 2, 0])


def _fold_gate_weights(wq, wk, wv, w_gate):
    NA, H, DH, _ = wq.shape
    G = w_gate.shape[-1]
    w_spec = pl.BlockSpec((1, 1, DH, DH), lambda i, h: (i, h, 0, 0))
    o_spec = pl.BlockSpec((1, 1, DH, G), lambda i, h: (i, h, 0, 0))
    out = jax.ShapeDtypeStruct((NA, H, DH, G), F32)
    return pl.pallas_call(
        functools.partial(_fold_gate_kernel, k_scale=float(DH) ** -0.5), grid=(NA, H),
        in_specs=[w_spec, w_spec, w_spec,
                  pl.BlockSpec((1, 3, 1, DH, G), lambda i, h: (i, 0, h, 0, 0))],
        out_specs=[o_spec, o_spec], out_shape=[out, out],
        compiler_params=_params("parallel", "parallel"), name="fold_gate_weights",
    )(wq, wk, wv, w_gate.reshape(NA, 3, H, DH, G))


def _conv_gates_kernel(xm_ref, buf_ref, cw_ref, cb_ref, wa_ref, wb_ref, bg_ref,
                       xc_ref, g_ref, seq_ref, *, rows, cum_rows, conv_w, n_heads):
    h = pl.program_id(1)
    S = xm_ref.shape[1]
    seq_ref[0:CONV_PAD, :] = buf_ref[0]
    seq_ref[CONV_PAD:CONV_PAD + S, :] = xm_ref[0].astype(F32)
    for r in range(S // rows):
        base = r * rows
        acc = jnp.broadcast_to(cb_ref[0], (rows, cb_ref.shape[-1]))
        for i in range(conv_w):
            start = base + CONV_PAD - (conv_w - 1) + i
            acc = acc + seq_ref[start:start + rows, :] * cw_ref[0, i:i + 1, :]
        xc = _silu(acc).astype(BF16)
        xc_ref[0, base:base + rows, :] = xc
        part = _dot(xc, wa_ref[0]) + _dot(xm_ref[0, base:base + rows, :], wb_ref[0])

        @pl.when(h == 0)
        def _():
            g_ref[0, base:base + rows, :] = part + bg_ref[...]

        @pl.when(h > 0)
        def _():
            g_ref[0, base:base + rows, :] = g_ref[0, base:base + rows, :] + part

    @pl.when(h == n_heads - 1)
    def _():
        r = lax.broadcasted_iota(jnp.int32, (cum_rows, cum_rows), 0)
        c = lax.broadcasted_iota(jnp.int32, (cum_rows, cum_rows), 1)
        trilf = (c <= r).astype(F32)
        lane = lax.broadcasted_iota(jnp.int32, (cum_rows, g_ref.shape[-1]), 1)
        carry = jnp.zeros((1, g_ref.shape[-1]), F32)
        for i in range(S // cum_rows):
            g = g_ref[0, i * cum_rows:(i + 1) * cum_rows, :]
            lf = jnp.where(lane >= n_heads, _log_sigmoid(g), 0.0)
            cum = jnp.dot(trilf, lf, precision=HIGHEST, preferred_element_type=F32) + carry
            g_ref[0, i * cum_rows:(i + 1) * cum_rows, :] = jnp.where(lane >= n_heads, cum, g)
            carry = cum[cum_rows - 1:cum_rows, :]


def _conv_gates(xz, buf, conv_w, conv_b, w_xc, w_xm, b_gate):
    B, S, _ = xz.shape
    H, DH, G = w_xc.shape
    AI = H * DH
    CW = conv_w.shape[0]
    rows = _tile(S, 768)
    act_spec = pl.BlockSpec((1, S, DH), lambda b, h: (b, 0, h))
    w_spec = pl.BlockSpec((1, DH, G), lambda b, h: (h, 0, 0))
    kern = functools.partial(_conv_gates_kernel, rows=rows, cum_rows=_tile(S, 512, 8), conv_w=CW,
                             n_heads=H)
    return pl.pallas_call(
        kern, grid=(B, H),
        in_specs=[act_spec,
                  pl.BlockSpec((1, CONV_PAD, DH), lambda b, h: (b, 0, h)),
                  pl.BlockSpec((1, CW, DH), lambda b, h: (h, 0, 0)),
                  pl.BlockSpec((1, 1, DH), lambda b, h: (h, 0, 0)),
                  w_spec, w_spec,
                  pl.BlockSpec((1, G), lambda b, h: (0, 0))],
        out_specs=[act_spec, pl.BlockSpec((1, S, G), lambda b, h: (b, 0, 0))],
        out_shape=[jax.ShapeDtypeStruct((B, S, AI), BF16), jax.ShapeDtypeStruct((B, S, G), F32)],
        scratch_shapes=[pltpu.VMEM((S + CONV_PAD, DH), F32)],
        compiler_params=_params("parallel", "arbitrary"), name="conv_gates",
    )(xz, buf,
      conv_w.reshape(CW, H, DH).transpose(1, 0, 2).astype(F32),
      conv_b.reshape(H, 1, DH).astype(F32), w_xc, w_xm, b_gate.reshape(1, G).astype(F32))


def _in_conv_kernel(x_ref, g_ref, w_ref, wz_ref, buf_ref, cw_ref, cb_ref, wa_ref, wb_ref, bg_ref,
                    xm_ref, z_ref, xc_ref, gate_ref, xn_ref, hist_ref, *, conv_w, n_heads):
    i = pl.program_id(1)
    j = pl.program_id(2)
    tm = x_ref.shape[1]

    @pl.when(j == 0)
    def _():
        x = x_ref[0].astype(F32)
        ms = jnp.mean(x * x, axis=-1, keepdims=True)
        xn_ref[...] = (x * lax.rsqrt(ms + EPS) * g_ref[...]).astype(BF16)

    prev = jnp.where(i == 0, buf_ref[0], hist_ref[j])
    row = lax.broadcasted_iota(jnp.int32, prev.shape, 0)
    mid = tm // 2 // BF16_ROWS * BF16_ROWS
    parts = []
    for r0, r1 in ((0, mid), (mid, tm)) if mid else ((0, tm),):
        acc = _dot(xn_ref[r0:r1, :], w_ref[...])
        xm = acc.astype(BF16)
        xm_ref[0, r0:r1, :] = xm
        y = cb_ref[...] + acc * cw_ref[conv_w - 1:conv_w, :]
        for shift in range(1, conv_w):
            rolled = pltpu.roll(acc, shift, 0)
            head = jnp.where(row < shift, pltpu.roll(prev, shift, 0), rolled[0:CONV_PAD, :])
            shifted = jnp.concatenate([head, rolled[CONV_PAD:, :]], axis=0)
            y = y + shifted * cw_ref[conv_w - 1 - shift:conv_w - shift, :]
        xc = _silu(y).astype(BF16)
        xc_ref[0, r0:r1, :] = xc
        z_ref[0, r0:r1, :] = _dot(xn_ref[r0:r1, :], wz_ref[...]).astype(BF16)
        parts.append(_dot(xc, wa_ref[...]) + _dot(xm, wb_ref[...]))
        prev = acc[r1 - r0 - CONV_PAD:r1 - r0, :]
    hist_ref[j] = prev
    part = jnp.concatenate(parts, axis=0)

    @pl.when(j == 0)
    def _():
        gate_ref[0] = part + bg_ref[...]

    @pl.when(j > 0)
    def _():
        gate_ref[0] = gate_ref[0] + part

    @pl.when(j == pl.num_programs(2) - 1)
    def _():
        g = gate_ref[0]
        lane = lax.broadcasted_iota(jnp.int32, g.shape, 1)
        gate_ref[0] = jnp.where(lane >= n_heads, _log_sigmoid(g), g)


def _in_conv(x, norm_g, w_in, buf, conv_w, conv_b, w_xc, w_xm, b_gate):
    B, S, D = x.shape
    H, DH, G = w_xc.shape
    AI = H * DH
    CW = conv_w.shape[0]
    tm = _tile(S, 1024)
    tn = _tile(AI, 1024, LANES)
    act_spec = pl.BlockSpec((1, tm, tn), lambda b, i, j: (b, i, j))
    act = jax.ShapeDtypeStruct((B, S, AI), BF16)
    kern = functools.partial(_in_conv_kernel, conv_w=CW, n_heads=H)
    return pl.pallas_call(
        kern, grid=(B, S // tm, AI // tn),
        in_specs=[pl.BlockSpec((1, tm, D), lambda b, i, j: (b, i, 0)),
                  pl.BlockSpec((1, D), lambda b, i, j: (0, 0)),
                  pl.BlockSpec((D, tn), lambda b, i, j: (0, j)),
                  pl.BlockSpec((D, tn), lambda b, i, j: (0, AI // tn + j)),
                  pl.BlockSpec((1, CONV_PAD, tn), lambda b, i, j: (b, 0, j)),
                  pl.BlockSpec((CW, tn), lambda b, i, j: (0, j)),
                  pl.BlockSpec((1, tn), lambda b, i, j: (0, j)),
                  pl.BlockSpec((tn, G), lambda b, i, j: (j, 0)),
                  pl.BlockSpec((tn, G), lambda b, i, j: (j, 0)),
                  pl.BlockSpec((1, G), lambda b, i, j: (0, 0))],
        out_specs=[act_spec, act_spec, act_spec,
                   pl.BlockSpec((1, tm, G), lambda b, i, j: (b, i, 0))],
        out_shape=[act, act, act, jax.ShapeDtypeStruct((B, S, G), F32)],
        scratch_shapes=[pltpu.VMEM((tm, D), BF16), pltpu.VMEM((AI // tn, CONV_PAD, tn), F32)],
        compiler_params=_params("parallel", "arbitrary", "arbitrary"), name="in_conv",
    )(x, norm_g.reshape(1, D).astype(F32), w_in, w_in, buf, conv_w.astype(F32),
      conv_b.reshape(1, AI).astype(F32), w_xc.reshape(AI, G), w_xm.reshape(AI, G),
      b_gate.reshape(1, G).astype(F32))


def _mlstm_core(xm, xc, wq, wk, wv, g_rows, g_cols, head, n_heads, C_ref, n_ref, m_ref, b_ref):
    L = xm.shape[0]
    q = _dot(xc, wq).astype(BF16)
    k = _dot(xc, wk).astype(BF16)
    v = _dot(xm, wv).astype(BF16)
    row = lax.broadcasted_iota(jnp.int32, (L, L), 0)
    col = lax.broadcasted_iota(jnp.int32, (L, L), 1)
    tril = col <= row
    lane = lax.broadcasted_iota(jnp.int32, g_cols.shape, 1)
    ig_col = jnp.sum(jnp.where(lane == head, g_cols, 0.0), axis=-1, keepdims=True)
    f_col = jnp.sum(jnp.where(lane == head + n_heads, g_cols, 0.0), axis=-1, keepdims=True)
    ig_row = g_rows[0:1, :]
    f_row = g_rows[1:2, :]
    m_prev = m_ref[...]
    f_prev = b_ref[...]
    log_d = jnp.where(tril, f_col - f_row + ig_row, NEG_INF)
    inter = f_col - f_prev + m_prev
    m_t = jnp.maximum(jnp.max(log_d, axis=-1, keepdims=True), inter)
    d = jnp.exp(log_d - m_t)
    a = jnp.exp(inter - m_t)
    s = _dot_nt(q, k) * d
    C = C_ref[...]
    n = n_ref[...]
    num = _dot(s.astype(BF16), v) + a * _dot(q, C.astype(BF16))
    qn = _dot_nt(q, jnp.broadcast_to(n, (8, n.shape[1])).astype(BF16))[:, 0:1]
    den = jnp.sum(s, axis=-1, keepdims=True) + a * qn
    m_new = m_t[L - 1:L, :]
    f_last = f_row[:, L - 1:L]
    decay = jnp.exp(f_last - f_prev + m_prev - m_new)
    w_col = jnp.exp(f_last - f_col + ig_col - m_new)
    C_ref[...] = decay * C + _dot_tn(k, (w_col * v.astype(F32)).astype(BF16))
    n_ref[...] = decay * n + jnp.sum(w_col * k.astype(F32), axis=0, keepdims=True)
    m_ref[...] = m_new
    b_ref[...] = f_last
    return num * (1.0 / jnp.maximum(jnp.abs(den), jnp.exp(-m_t)))


def _mlstm_finish(hid, xc, z, og, skip):
    hc = hid - jnp.mean(hid, axis=-1, keepdims=True)
    hn = hc * lax.rsqrt(jnp.mean(hc * hc, axis=-1, keepdims=True) + EPS) * og
    return ((hn + skip * xc.astype(F32)) * _silu(z.astype(F32))).astype(BF16)


def _mlstm_kernel(*refs, lead, chunk, n_chunks, n_heads, has_state, merge_lead, head_axis):
    xm_ref, xc_ref, z_ref, wq_ref, wk_ref, wv_ref = refs[:6]
    pos = 6
    if lead:
        gl_ref = refs[pos]
        pos += 1
    gm_ref, gc_ref, og_ref, skip_ref = refs[pos:pos + 4]
    pos += 4
    if has_state:
        c0_ref, n0_ref, m0_ref = refs[pos:pos + 3]
        pos += 3
    pos += 1
    o_ref, C_ref, n_ref, m_ref, b_ref, hid_ref = refs[pos:pos + 6]
    C_st, n_st, m_st = C_ref.at[0, 0, 0], n_ref.at[0, 0], m_ref.at[0, 0]
    if has_state:
        C_st[...] = c0_ref[0, 0, 0]
        n_st[...] = n0_ref[0, 0, 0]
        m_st[...] = m0_ref[0, 0, 0]
    else:
        C_st[...] = jnp.zeros(C_st.shape, F32)
        n_st[...] = jnp.zeros(n_st.shape, F32)
        m_st[...] = jnp.zeros(m_st.shape, F32)
    b_ref[...] = jnp.zeros(b_ref.shape, F32)
    head = pl.program_id(head_axis)

    def chunk_rows(c):
        return pl.ds(pl.multiple_of(lead + c * chunk, BF16_ROWS), chunk)

    def core(rows, g_rows):
        return _mlstm_core(xm_ref[0, rows, :], xc_ref[0, rows, :], wq_ref[0], wk_ref[0], wv_ref[0],
                           g_rows, gc_ref[0, rows, :], head, n_heads, C_st, n_st, m_st, b_ref)

    def finish(rows, hid):
        o_ref[0, rows, :] = _mlstm_finish(hid, xc_ref[0, rows, :], z_ref[0, rows, :], og_ref[0],
                                          skip_ref[0])

    if merge_lead:
        first = pl.ds(0, lead + chunk)
        finish(first, core(first, gl_ref[0, 0]))
        hid_ref[0] = core(chunk_rows(1), gm_ref[0, 0, 1])

        def pair(p, carry):
            c = 2 * p + 2
            hid_ref[1] = core(chunk_rows(c), gm_ref[0, 0, c])
            finish(chunk_rows(c - 1), hid_ref[0])
            hid_ref[0] = core(chunk_rows(c + 1), gm_ref[0, 0, c + 1])
            finish(chunk_rows(c), hid_ref[1])
            return carry

        lax.fori_loop(0, n_chunks // 2 - 1, pair, 0)
        finish(chunk_rows(n_chunks - 1), hid_ref[0])
    else:
        if lead:
            finish(pl.ds(0, lead), core(pl.ds(0, lead), gl_ref[0, 0]))

        def body(c, carry):
            finish(chunk_rows(c), core(chunk_rows(c), gm_ref[0, 0, c]))
            return carry

        lax.fori_loop(0, n_chunks, body, 0, unroll=_tile(n_chunks, 2, 1))


def _mlstm(xm, z, z_head0, xc, gates, wq, wk, wv, out_g, skip, state, lead, chunk, c_stack, layer,
           n_layers):
    B, S, AI = xc.shape
    H, DH = out_g.shape
    n_chunks = (S - lead) // chunk
    assert lead + n_chunks * chunk == S
    g = gates.reshape(B, S, 2, H).transpose(0, 3, 2, 1)
    g_main = g[..., lead:].reshape(B, H, 2, n_chunks, chunk).transpose(0, 1, 3, 2, 4)
    has_state = state is not None
    heads_outer = has_state

    def spec(shape, index):
        return pl.BlockSpec(shape, (lambda h, b: index(b, h)) if heads_outer else index)

    act_spec = spec((1, S, DH), lambda b, h: (b, 0, h))
    w_spec = spec((1, DH, DH), lambda b, h: (h, 0, 0))
    in_specs = [act_spec, act_spec, spec((1, S, DH), lambda b, h: (b, 0, z_head0 + h)),
                w_spec, w_spec, w_spec]
    args = [xm, xc, z, wq, wk, wv]
    merge_lead = bool(lead) and n_chunks % 2 == 0
    if lead:
        n_first = lead + chunk if merge_lead else lead
        in_specs.append(spec((1, 1, 2, n_first), lambda b, h: (b, h, 0, 0)))
        args.append(g[..., :n_first])
    in_specs += [spec((1, 1, n_chunks, 2, chunk), lambda b, h: (b, h, 0, 0, 0)),
                 spec((1, S, 2 * H), lambda b, h: (b, 0, 0)),
                 spec((1, 1, DH), lambda b, h: (h, 0, 0)),
                 spec((1, 1, DH), lambda b, h: (h, 0, 0))]
    args += [g_main, gates, out_g.reshape(H, 1, DH).astype(F32), skip.reshape(H, 1, DH).astype(F32)]
    n_spec = spec((1, 1, 1, DH), lambda b, h: (b, h, 0, 0))
    m_spec = spec((1, 1, 1, 1), lambda b, h: (b, h, 0, 0))
    if has_state:
        C0, n0, m0 = state
        NL = C0.shape[0]
        in_specs += [spec((1, 1, 1, DH, DH), lambda b, h: (layer, b, h, 0, 0)),
                     spec((1, 1, 1, 1, DH), lambda b, h: (layer, b, h, 0, 0)),
                     spec((1, 1, 1, 1, 1), lambda b, h: (layer, b, h, 0, 0))]
        args += [C0.astype(F32), n0.reshape(NL, B, H, 1, DH).astype(F32),
                 m0.reshape(NL, B, H, 1, 1).astype(F32)]
    if c_stack is None:
        c_stack = jnp.zeros((n_layers, B, H, DH, DH), F32)
    aliases = {len(args): 1}
    in_specs.append(pl.BlockSpec(memory_space=pl.ANY))
    args.append(c_stack)
    kern = functools.partial(_mlstm_kernel, lead=lead, chunk=chunk, n_chunks=n_chunks, n_heads=H,
                             has_state=has_state, merge_lead=merge_lead,
                             head_axis=0 if heads_outer else 1)
    hz, C, n, m = pl.pallas_call(
        kern, grid=(H, B) if heads_outer else (B, H), in_specs=in_specs,
        out_specs=[act_spec,
                   spec((1, 1, 1, DH, DH), lambda b, h: (layer, b, h, 0, 0)),
                   n_spec, m_spec],
        out_shape=[jax.ShapeDtypeStruct((B, S, AI), BF16),
                   jax.ShapeDtypeStruct((n_layers, B, H, DH, DH), F32),
                   jax.ShapeDtypeStruct((B, H, 1, DH), F32),
                   jax.ShapeDtypeStruct((B, H, 1, 1), F32)],
        scratch_shapes=[pltpu.VMEM((1, 1), F32), pltpu.VMEM((2, chunk, DH), F32)],
        input_output_aliases=aliases,
        compiler_params=_params("parallel", "parallel"), name="mlstm",
    )(*args)
    return hz, C, n.reshape(B, H, DH), m.reshape(B, H)


def _cumsum_kernel(x_ref, o_ref, *, scale, first_lane):
    x = x_ref[0]
    S = x.shape[0]
    summed = lax.broadcasted_iota(jnp.int32, x.shape, 1) >= first_lane
    row = lax.broadcasted_iota(jnp.int32, x.shape, 0)
    y = jnp.where(summed, x, 0.0)
    step = 1
    while step < S:
        y = y + jnp.where(row >= step, pltpu.roll(y, step, 0), 0.0)
        step *= 2
    o_ref[0] = jnp.where(summed, y * scale, x)


def _cumsum(x, scale=1.0, first_lane=0):
    B, S, G = x.shape
    spec = pl.BlockSpec((1, S, G), lambda b: (b, 0, 0))
    return pl.pallas_call(
        functools.partial(_cumsum_kernel, scale=scale, first_lane=first_lane),
        grid=(B,), in_specs=[spec], out_specs=spec, out_shape=jax.ShapeDtypeStruct((B, S, G), F32),
        compiler_params=_params("parallel"), name="cumsum",
    )(x)


def _lane_fold(op, acc, x):
    for c in range(x.shape[1] // LANES):
        acc = op(acc, x[:, c * LANES:(c + 1) * LANES])
    return acc


def _fox_prompt_kernel(q_ref, z_ref, k_ref, v_ref, fq_ref, fkl_ref, fkm_ref, o_ref,
                       s_ref, sl_ref, fq_rep, m_ref, l_ref, acc_ref, *, lead, blk, n_blk, heads,
                       dim):
    n_lane = blk // LANES
    cols = [slice(hh * dim, (hh + 1) * dim) for hh in range(heads)]

    def key_rows(j):
        return pl.ds(pl.multiple_of(lead + j * blk, BF16_ROWS), blk)

    def tiled(x):
        return jnp.concatenate([x] * n_lane, axis=1)

    def q_block(i, carry):
        qrows = pl.ds(pl.multiple_of(i * blk, blk), blk)

        def logits(hh, j):
            return (_dot_nt(q_ref[0, qrows, cols[hh]], k_ref[0, key_rows(j), cols[hh]])
                    + tiled(fq_rep[hh]) - fkm_ref[0, 0, j, hh:hh + 1, :])

        for hh in range(heads):
            fq_rep[hh] = jnp.broadcast_to(fq_ref[0, 0, qrows, hh:hh + 1], (blk, LANES))
        if lead:
            for hh in range(heads):
                s = (_dot_nt(q_ref[0, qrows, cols[hh]], k_ref[0, 0:LANES, cols[hh]])
                     + fq_rep[hh] - fkl_ref[0, 0, hh:hh + 1, :])
                sl_ref[hh] = s
                m_ref[hh] = s
        else:
            m_ref[...] = jnp.full(m_ref.shape, NEG_INF, F32)

        def pass1(j, c):
            for hh in range(heads):
                s = logits(hh, j)
                s_ref[hh, j] = s
                m_ref[hh] = _lane_fold(jnp.maximum, m_ref[hh], s)
            return c

        lax.fori_loop(0, i, pass1, 0)
        row = lax.broadcasted_iota(jnp.int32, (blk, blk), 0)
        col = lax.broadcasted_iota(jnp.int32, (blk, blk), 1)
        for hh in range(heads):
            s = jnp.where(col <= row, logits(hh, i), NEG_INF)
            s_ref[hh, i] = s
            m = jnp.max(_lane_fold(jnp.maximum, m_ref[hh], s), axis=-1, keepdims=True)
            m_ref[hh] = jnp.broadcast_to(m, (blk, LANES))
        if lead:
            for hh in range(heads):
                p = jnp.exp2(sl_ref[hh] - m_ref[hh])
                l_ref[hh] = p
                acc_ref[hh] = _dot(p.astype(BF16), v_ref[0, 0:LANES, cols[hh]])
        else:
            l_ref[...] = jnp.zeros(l_ref.shape, F32)
            acc_ref[...] = jnp.zeros(acc_ref.shape, F32)

        def pass2(j, c):
            for hh in range(heads):
                p = jnp.exp2(s_ref[hh, j] - tiled(m_ref[hh]))
                l_ref[hh] = _lane_fold(jnp.add, l_ref[hh], p)
                acc_ref[hh] += _dot(p.astype(BF16), v_ref[0, key_rows(j), cols[hh]])
            return c

        lax.fori_loop(0, i + 1, pass2, 0)
        for hh in range(heads):
            l = jnp.sum(l_ref[hh], axis=-1, keepdims=True)
            o_ref[0, qrows, cols[hh]] = (acc_ref[hh] * (1.0 / l)
                                         * _silu(z_ref[0, qrows, cols[hh]].astype(F32))).astype(BF16)
        return carry

    lax.fori_loop(0, n_blk, q_block, 0)


def _fox_prompt(q, z, kb, vb, f_cum, lead, heads_per_step=4, blk=512):
    B, SQ, W = q.shape
    NH = f_cum.shape[-1]
    dim = W // NH
    hg = min(heads_per_step, NH)
    G = NH // hg
    blk = _tile(SQ, blk, LANES)
    n_blk = SQ // blk
    SK = lead + SQ
    assert lead <= LANES <= SK
    fq = f_cum[:, lead:].reshape(B, SQ, G, hg).transpose(0, 2, 1, 3)
    fk = f_cum.transpose(0, 2, 1).reshape(B, G, hg, SK)
    fk_lead = jnp.pad(fk[..., :lead], ((0, 0), (0, 0), (0, 0), (0, LANES - lead)),
                      constant_values=float("inf"))
    fk_main = fk[..., lead:].reshape(B, G, hg, n_blk, blk).transpose(0, 1, 3, 2, 4)
    wcols = hg * dim
    kern = functools.partial(_fox_prompt_kernel, lead=lead, blk=blk, n_blk=n_blk, heads=hg, dim=dim)
    return pl.pallas_call(
        kern, grid=(B, G),
        in_specs=[pl.BlockSpec((1, SQ, wcols), lambda b, g: (b, 0, g)),
                  pl.BlockSpec((1, SQ, wcols), lambda b, g: (b, 0, g)),
                  pl.BlockSpec((1, SK, wcols), lambda b, g: (b, 0, g)),
                  pl.BlockSpec((1, SK, wcols), lambda b, g: (b, 0, g)),
                  pl.BlockSpec((1, 1, SQ, hg), lambda b, g: (b, g, 0, 0)),
                  pl.BlockSpec((1, 1, hg, LANES), lambda b, g: (b, g, 0, 0)),
                  pl.BlockSpec((1, 1, n_blk, hg, blk), lambda b, g: (b, g, 0, 0, 0))],
        out_specs=pl.BlockSpec((1, SQ, wcols), lambda b, g: (b, 0, g)),
        out_shape=jax.ShapeDtypeStruct((B, SQ, W), BF16),
        scratch_shapes=[pltpu.VMEM((hg, n_blk, blk, blk), F32)]
        + [pltpu.VMEM((hg, blk, LANES), F32)] * 4 + [pltpu.VMEM((hg, blk, dim), F32)],
        compiler_params=_params("parallel", "parallel"), name="fox_prompt",
    )(q, z, kb, vb, fq, fk_lead, fk_main)


def _pack_heads_kernel(x_ref, o_ref):
    o_ref[0] = pltpu.einshape("khd->k(hd)", x_ref[0]).astype(o_ref.dtype)


def _pack_heads(x, dtype):
    B, P, NH, dim = x.shape
    rows = _tile(P, 512)
    return pl.pallas_call(
        _pack_heads_kernel, grid=(B, P // rows),
        in_specs=[pl.BlockSpec((1, rows, NH, dim), lambda b, i: (b, i, 0, 0))],
        out_specs=pl.BlockSpec((1, rows, NH * dim), lambda b, i: (b, i, 0)),
        out_shape=jax.ShapeDtypeStruct((B, P, NH * dim), dtype),
        compiler_params=_params("parallel", "parallel"), name="pack_heads",
    )(x)


def _fox_decode_kernel(q_ref, z_ref, ck_ref, cv_ref, k_ref, v_ref, fq_ref, fkc_ref, fkn_ref, o_ref,
                       *, heads, dim):
    Q = q_ref.shape[1]
    row = lax.broadcasted_iota(jnp.int32, (Q, Q), 0)
    col = lax.broadcasted_iota(jnp.int32, (Q, Q), 1)
    outs = []
    for hh in range(heads):
        cs = slice(hh * dim, (hh + 1) * dim)
        q = q_ref[0, :, cs]
        fq = fq_ref[0, 0, :, hh:hh + 1]
        s_c = _dot_nt(q, ck_ref[0, :, cs].astype(BF16)) + fq - fkc_ref[0, 0, hh:hh + 1, :]
        s_n = jnp.where(col <= row, _dot_nt(q, k_ref[0, :, cs]) + fq - fkn_ref[0, 0, hh:hh + 1, :],
                        NEG_INF)
        m = jnp.maximum(jnp.max(s_c, axis=-1, keepdims=True), jnp.max(s_n, axis=-1, keepdims=True))
        p_c = jnp.exp2(s_c - m)
        p_n = jnp.exp2(s_n - m)
        l = jnp.sum(p_c, axis=-1, keepdims=True) + jnp.sum(p_n, axis=-1, keepdims=True)
        acc = _dot(p_c.astype(BF16), cv_ref[0, :, cs].astype(BF16)) + _dot(p_n.astype(BF16), v_ref[0, :, cs])
        outs.append((acc * (1.0 / l) * _silu(z_ref[0, :, cs].astype(F32))).astype(BF16))
    o_ref[0] = jnp.concatenate(outs, axis=-1)


def _fox_decode(q, z, cache_k, cache_v, kb, vb, f_cum, heads_per_step=4):
    B, Q, W = q.shape
    P = cache_k.shape[1]
    NH = f_cum.shape[-1]
    dim = W // NH
    hg = min(heads_per_step, NH)
    G = NH // hg
    fq = f_cum[:, P:].reshape(B, Q, G, hg).transpose(0, 2, 1, 3)
    fk = f_cum.transpose(0, 2, 1).reshape(B, G, hg, P + Q)
    wcols = hg * dim
    kern = functools.partial(_fox_decode_kernel, heads=hg, dim=dim)
    return pl.pallas_call(
        kern, grid=(B, G),
        in_specs=[pl.BlockSpec((1, Q, wcols), lambda b, g: (b, 0, g)),
                  pl.BlockSpec((1, Q, wcols), lambda b, g: (b, 0, g)),
                  pl.BlockSpec((1, P, wcols), lambda b, g: (b, 0, g)),
                  pl.BlockSpec((1, P, wcols), lambda b, g: (b, 0, g)),
                  pl.BlockSpec((1, Q, wcols), lambda b, g: (b, 0, g)),
                  pl.BlockSpec((1, Q, wcols), lambda b, g: (b, 0, g)),
                  pl.BlockSpec((1, 1, Q, hg), lambda b, g: (b, g, 0, 0)),
                  pl.BlockSpec((1, 1, hg, P), lambda b, g: (b, g, 0, 0)),
                  pl.BlockSpec((1, 1, hg, Q), lambda b, g: (b, g, 0, 0))],
        out_specs=pl.BlockSpec((1, Q, wcols), lambda b, g: (b, 0, g)),
        out_shape=jax.ShapeDtypeStruct((B, Q, W), BF16),
        compiler_params=_params("parallel", "parallel"), name="fox_decode",
    )(q, z, cache_k, cache_v, kb, vb, fq, fk[..., :P], fk[..., P:])


def _mlstm_layer(x, conv_hist, state, w, lead, chunk, c_stack, layer, n_layers):
    (norm_g, w_in, conv_w, conv_b, wq, wk, wv, w_xc, w_xm, b_gate, out_g, skip, w_out) = w
    B, S, _ = x.shape
    H, DH, _ = wq.shape
    AI = H * DH
    CW = conv_w.shape[0]
    assert S >= CW - 1 and CW - 1 <= CONV_PAD
    buf = jnp.zeros((B, CONV_PAD, AI), F32)
    if conv_hist is not None:
        buf = buf.at[:, CONV_PAD - (CW - 1):].set(conv_hist.astype(F32))
    if S >= MIN_FUSED_CONV_ROWS:
        wb = w_in.astype(BF16)
        xm, z, xc, gates = _in_conv(x, norm_g, wb, buf, conv_w, conv_b,
                                    w_xc.astype(BF16), w_xm.astype(BF16), b_gate)
        gates = _cumsum(gates, first_lane=H)
        z_head0 = 0
    else:
        xm = _norm_matmul(x, norm_g, w_in.astype(BF16))
        xc, gates = _conv_gates(xm, buf, conv_w, conv_b, w_xc.astype(BF16), w_xm.astype(BF16),
                                b_gate)
        z, z_head0 = xm, H
    hz, c_stack, n, m = _mlstm(
        xm, z, z_head0, xc, gates, wq.astype(BF16), (wk * float(DH) ** -0.5).astype(BF16),
        wv.astype(BF16), out_g, skip.reshape(H, DH), state, lead, chunk, c_stack, layer, n_layers)
    x_new = _matmul_residual(hz, w_out.astype(BF16), x)
    new_hist = xm[:, S - (CW - 1):, :AI].astype(F32)
    return x_new, new_hist, c_stack, n, m


def _shared_kv(x, kv_norm_g, w_kvf, b_f, k_norm_g):
    NH, dim = k_norm_g.shape
    W = NH * dim
    wb = w_kvf.astype(BF16)
    k32, kb, v32, vb, logf = _dual_norm_matmul(
        x, kv_norm_g, wb, W, k_norm_g.reshape(1, W), a_dtypes=[F32, BF16],
        b_dtypes=[F32, BF16], head_dim=dim, split_heads_first=True, gate_w=wb[:, 2 * W:],
        gate_b=b_f)
    return k32, kb, v32, vb, logf


def _fox_in(x, norm_g, w_in, q_norm_g):
    NH, dim = q_norm_g.shape
    W = NH * dim
    wb = w_in.astype(BF16)
    gain = q_norm_g.reshape(1, W).astype(F32) * (float(dim) ** -0.5 * LOG2E)
    q, z = _dual_norm_matmul(x, norm_g, wb, W, gain, a_dtypes=[BF16],
                             b_dtypes=[BF16], head_dim=dim)
    return q, z


def kernel(x_prompt, x_sample, cache_k, cache_v, cache_logf, state_C, state_n, state_m, state_conv,
           meta_tokens, a_norm_g, a_w_in, a_conv_w, a_conv_b, a_wq, a_wk, a_wv, a_w_gate, a_b_gate,
           a_out_g, a_skip, a_w_out, kv_norm_g, w_kvf, b_f, k_norm_g, b_norm_g, b_w_in, q_norm_g,
           b_w_out):
    B, SEQ, D = x_prompt.shape
    DB, DEC, _ = x_sample.shape
    NM = meta_tokens.shape[0]
    N_A = a_norm_g.shape[0]
    N_B = b_norm_g.shape[0]
    NH, dim = k_norm_g.shape
    W = NH * dim
    P = cache_k.shape[1]
    chunk = _tile(SEQ, 256)

    xp = jnp.concatenate(
        [jnp.broadcast_to(meta_tokens.astype(x_prompt.dtype)[None], (B, NM, D)), x_prompt], axis=1)
    xs = x_sample
    p_state, s_state = [], []
    p_C = s_C = None
    w_xc, w_xm = _fold_gate_weights(a_wq, a_wk, a_wv, a_w_gate)
    for i in range(N_A):
        w = (a_norm_g[i], a_w_in[i], a_conv_w[i], a_conv_b[i], a_wq[i], a_wk[i], a_wv[i],
             w_xc[i], w_xm[i], a_b_gate[i], a_out_g[i], a_skip[i], a_w_out[i])
        xp, hist, p_C, n, m = _mlstm_layer(xp, None, None, w, NM, chunk, p_C, i, N_A)
        p_state.append((hist, n, m))
        xs, hist, s_C, n, m = _mlstm_layer(xs, state_conv[i], (state_C, state_n, state_m),
                                           w, 0, DEC, s_C, i, N_A)
        s_state.append((hist, n, m))

    pk32, pkb, pv32, pvb, p_logf = _shared_kv(xp, kv_norm_g, w_kvf, b_f, k_norm_g)
    sk32, skb, sv32, svb, s_logf = _shared_kv(xs, kv_norm_g, w_kvf, b_f, k_norm_g)
    fp = _cumsum(p_logf, LOG2E)
    fs = _cumsum(jnp.concatenate([cache_logf.astype(F32), s_logf], axis=1), LOG2E)
    ck = _pack_heads(cache_k, BF16)
    cv = _pack_heads(cache_v, BF16)
    xp = xp[:, NM:]
    for j in range(N_B):
        q, z = _fox_in(xp, b_norm_g[j], b_w_in[j], q_norm_g[j])
        xp = _matmul_residual(_fox_prompt(q, z, pkb, pvb, fp, NM), b_w_out[j].astype(BF16), xp)
        q, z = _fox_in(xs, b_norm_g[j], b_w_in[j], q_norm_g[j])
        xs = _matmul_residual(_fox_decode(q, z, ck, cv, skb, svb, fs), b_w_out[j].astype(BF16), xs)

    def stack(states, idx):
        return jnp.stack([st[idx] for st in states])

    return (xp, xs, pk32, pv32, p_logf,
            p_C, stack(p_state, 1), stack(p_state, 2), stack(p_state, 0),
            sk32, sv32, s_logf,
            s_C, stack(s_state, 1), stack(s_state, 2), stack(s_state, 0))
```

```python
import functools

import jax
import jax.numpy as jnp
from jax import lax
from jax.experimental import pallas as pl
from jax.experimental.pallas import tpu as pltpu

F32 = jnp.float32
BF16 = jnp.bfloat16
EPS = 1e-6
CONV_PAD = 8
BF16_ROWS = 16
LANES = 128
MIN_FUSED_CONV_ROWS = 256
VMEM_LIMIT_BYTES = 56 * 1024 * 1024
HIGHEST = lax.Precision.HIGHEST
NEG_INF = float("-inf")
LOG2E = 1.4426950408889634


def _params(*sem):
    return pltpu.CompilerParams(dimension_semantics=sem, vmem_limit_bytes=VMEM_LIMIT_BYTES)


def _tile(n, cap, mult=BF16_ROWS):
    best = None
    for t in range(mult, min(n, cap) + 1, mult):
        if n % t == 0:
            best = t
    return best if best is not None else n


def _dot(a, b):
    return jnp.dot(a, b, preferred_element_type=F32)


def _dot_nt(a, b):
    return lax.dot_general(a, b, (((1,), (1,)), ((), ())), preferred_element_type=F32)


def _dot_tn(a, b):
    return lax.dot_general(a, b, (((0,), (0,)), ((), ())), preferred_element_type=F32)


def _silu(x):
    return x / (1.0 + jnp.exp(-x))


def _log_sigmoid(x):
    return jnp.minimum(x, 0.0) - jnp.log1p(jnp.exp(-jnp.abs(x)))


def _norm_matmul_kernel(x_ref, g_ref, w_ref, o_ref, xn_ref):
    @pl.when(pl.program_id(2) == 0)
    def _():
        x = x_ref[0].astype(F32)
        ms = jnp.mean(x * x, axis=-1, keepdims=True)
        xn_ref[...] = (x * lax.rsqrt(ms + EPS) * g_ref[...]).astype(BF16)

    o_ref[0] = _dot(xn_ref[...], w_ref[...]).astype(o_ref.dtype)


def _norm_matmul(x, g, w):
    lead_shape = x.shape[:2]
    x = x.reshape(1, -1, x.shape[-1])
    B, S, D = x.shape
    N = w.shape[1]
    tm = _tile(S, 1024)
    tn = _tile(N, 1024, LANES)
    out = pl.pallas_call(
        _norm_matmul_kernel, grid=(B, S // tm, N // tn),
        in_specs=[pl.BlockSpec((1, tm, D), lambda b, i, j: (b, i, 0)),
                  pl.BlockSpec((1, D), lambda b, i, j: (0, 0)),
                  pl.BlockSpec((D, tn), lambda b, i, j: (0, j))],
        out_specs=pl.BlockSpec((1, tm, tn), lambda b, i, j: (b, i, j)),
        out_shape=jax.ShapeDtypeStruct((B, S, N), BF16),
        scratch_shapes=[pltpu.VMEM((tm, D), BF16)],
        compiler_params=_params("parallel", "parallel", "arbitrary"), name="norm_matmul",
    )(x, g.reshape(1, D).astype(F32), w)
    return out.reshape(lead_shape + (N,))


def _dual_norm_matmul_kernel(*refs, n_a, n_b, head_dim, has_gate):
    x_ref, g_ref, wa_ref, gain_ref, wb_ref = refs[:5]
    pos = 5
    if has_gate:
        gw_ref, gb_ref = refs[pos:pos + 2]
        pos += 2
    a_refs = refs[pos:pos + n_a]
    b_refs = refs[pos + n_a:pos + n_a + n_b]
    pos += n_a + n_b
    if has_gate:
        gate_out_ref = refs[pos]
        pos += 1
    xn_ref = refs[pos]

    @pl.when(pl.program_id(2) == 0)
    def _():
        x = x_ref[0].astype(F32)
        ms = jnp.mean(x * x, axis=-1, keepdims=True)
        xn_ref[...] = (x * lax.rsqrt(ms + EPS) * g_ref[...]).astype(BF16)
        if has_gate:
            gate_out_ref[0] = _log_sigmoid(_dot(xn_ref[...], gw_ref[...]) + gb_ref[...])

    def store(out_refs, val):
        for o in out_refs:
            if len(o.shape) == 4:
                o[0] = val.reshape(o.shape[1:]).astype(o.dtype)
            else:
                o[0] = val.astype(o.dtype)

    acc = _dot(xn_ref[...], wa_ref[...])
    cols = []
    for c in range(acc.shape[1] // head_dim):
        blk = acc[:, c * head_dim:(c + 1) * head_dim]
        ms = jnp.mean(blk * blk, axis=-1, keepdims=True)
        cols.append(blk * lax.rsqrt(ms + EPS))
    store(a_refs, jnp.concatenate(cols, axis=-1) * gain_ref[...])
    store(b_refs, _dot(xn_ref[...], wb_ref[...]))


def _dual_norm_matmul(x, g, w, N, gain_a, *, a_dtypes, b_dtypes, head_dim, split_heads_first=False,
                      gate_w=None, gate_b=None):
    lead_shape = x.shape[:2]
    x = x.reshape(1, -1, x.shape[-1])
    B, S, D = x.shape
    tm = _tile(S, 1024)
    tn = _tile(N, 1024, LANES)
    has_gate = gate_w is not None
    in_specs = [pl.BlockSpec((1, tm, D), lambda b, i, j: (b, i, 0)),
                pl.BlockSpec((1, D), lambda b, i, j: (0, 0)),
                pl.BlockSpec((D, tn), lambda b, i, j: (0, j)),
                pl.BlockSpec((1, tn), lambda b, i, j: (0, j)),
                pl.BlockSpec((D, tn), lambda b, i, j: (0, N // tn + j))]
    args = [x, g.reshape(1, D).astype(F32), w, gain_a.astype(F32), w]
    if has_gate:
        G = gate_w.shape[1]
        in_specs += [pl.BlockSpec((D, G), lambda b, i, j: (0, 0)),
                     pl.BlockSpec((1, G), lambda b, i, j: (0, 0))]
        args += [gate_w, gate_b.reshape(1, G).astype(F32)]
    out_shape, out_specs = [], []
    for dtypes in (a_dtypes, b_dtypes):
        for k, dt in enumerate(dtypes):
            if split_heads_first and k == 0:
                out_shape.append(jax.ShapeDtypeStruct((B, S, N // head_dim, head_dim), dt))
                out_specs.append(pl.BlockSpec((1, tm, tn // head_dim, head_dim),
                                              lambda b, i, j: (b, i, j, 0)))
            else:
                out_shape.append(jax.ShapeDtypeStruct((B, S, N), dt))
                out_specs.append(pl.BlockSpec((1, tm, tn), lambda b, i, j: (b, i, j)))
    if has_gate:
        out_shape.append(jax.ShapeDtypeStruct((B, S, G), F32))
        out_specs.append(pl.BlockSpec((1, tm, G), lambda b, i, j: (b, i, 0)))
    kern = functools.partial(_dual_norm_matmul_kernel, n_a=len(a_dtypes), n_b=len(b_dtypes),
                             head_dim=head_dim, has_gate=has_gate)
    outs = pl.pallas_call(
        kern, grid=(B, S // tm, N // tn), in_specs=in_specs, out_specs=out_specs,
        out_shape=out_shape, scratch_shapes=[pltpu.VMEM((tm, D), BF16)],
        compiler_params=_params("parallel", "parallel", "arbitrary"), name="dual_norm_matmul",
    )(*args)
    return [o.reshape(lead_shape + o.shape[2:]) for o in outs]


def _matmul_residual_kernel(a_ref, w_ref, x_ref, o_ref):
    o_ref[0] = x_ref[0] + _dot(a_ref[0], w_ref[...])


def _matmul_residual(a, w, x):
    out_shape = x.shape
    a = a.reshape(1, -1, a.shape[-1])
    x = x.reshape(1, -1, x.shape[-1])
    B, S, K = a.shape
    N = w.shape[1]
    tm = _tile(S, 1024)
    tn = _tile(N, 1024, 128)
    return pl.pallas_call(
        _matmul_residual_kernel, grid=(B, S // tm, N // tn),
        in_specs=[pl.BlockSpec((1, tm, K), lambda b, i, j: (b, i, 0)),
                  pl.BlockSpec((K, tn), lambda b, i, j: (0, j)),
                  pl.BlockSpec((1, tm, tn), lambda b, i, j: (b, i, j))],
        out_specs=pl.BlockSpec((1, tm, tn), lambda b, i, j: (b, i, j)),
        out_shape=jax.ShapeDtypeStruct((B, S, N), F32),
        compiler_params=_params("parallel", "parallel", "arbitrary"), name="matmul_residual",
    )(a, w, x).reshape(out_shape)


def _fold_gate_kernel(wq_ref, wk_ref, wv_ref, wg_ref, a_ref, b_ref, *, k_scale):
    def hdot(a, b):
        return jnp.dot(a, b, precision=HIGHEST, preferred_element_type=F32)

    a_ref[0, 0] = hdot(wq_ref[0, 0], wg_ref[0, 0, 0]) + hdot(wk_ref[0, 0], wg_ref[0, 1, 0]) * k_scale
    b_ref[0, 0] = hdot(wv_ref[0, 0], wg_ref[0, 2, 0])


def _fold_gate_weights(wq, wk, wv, w_gate):
    NA, H, DH, _ = wq.shape
    G = w_gate.shape[-1]
    w_spec = pl.BlockSpec((1, 1, DH, DH), lambda i, h: (i, h, 0, 0))
    o_spec = pl.BlockSpec((1, 1, DH, G), lambda i, h: (i, h, 0, 0))
    out = jax.ShapeDtypeStruct((NA, H, DH, G), F32)
    return pl.pallas_call(
        functools.partial(_fold_gate_kernel, k_scale=float(DH) ** -0.5), grid=(NA, H),
        in_specs=[w_spec, w_spec, w_spec,
                  pl.BlockSpec((1, 3, 1, DH, G), lambda i, h: (i, 0, h, 0, 0))],
        out_specs=[o_spec, o_spec], out_shape=[out, out],
        compiler_params=_params("parallel", "parallel"), name="fold_gate_weights",
    )(wq, wk, wv, w_gate.reshape(NA, 3, H, DH, G))


def _conv_gates_kernel(xm_ref, buf_ref, cw_ref, cb_ref, wa_ref, wb_ref, bg_ref,
                       xc_ref, g_ref, seq_ref, *, rows, cum_rows, conv_w, n_heads):
    h = pl.program_id(1)
    S = xm_ref.shape[1]
    seq_ref[0:CONV_PAD, :] = buf_ref[0]
    seq_ref[CONV_PAD:CONV_PAD + S, :] = xm_ref[0].astype(F32)
    for r in range(S // rows):
        base = r * rows
        acc = jnp.broadcast_to(cb_ref[0], (rows, cb_ref.shape[-1]))
        for i in range(conv_w):
            start = base + CONV_PAD - (conv_w - 1) + i
            acc = acc + seq_ref[start:start + rows, :] * cw_ref[0, i:i + 1, :]
        xc = _silu(acc).astype(BF16)
        xc_ref[0, base:base + rows, :] = xc
        part = _dot(xc, wa_ref[0]) + _dot(xm_ref[0, base:base + rows, :], wb_ref[0])

        @pl.when(h == 0)
        def _():
            g_ref[0, base:base + rows, :] = part + bg_ref[...]

        @pl.when(h > 0)
        def _():
            g_ref[0, base:base + rows, :] = g_ref[0, base:base + rows, :] + part

    @pl.when(h == n_heads - 1)
    def _():
        r = lax.broadcasted_iota(jnp.int32, (cum_rows, cum_rows), 0)
        c = lax.broadcasted_iota(jnp.int32, (cum_rows, cum_rows), 1)
        trilf = (c <= r).astype(F32)
        lane = lax.broadcasted_iota(jnp.int32, (cum_rows, g_ref.shape[-1]), 1)
        carry = jnp.zeros((1, g_ref.shape[-1]), F32)
        for i in range(S // cum_rows):
            g = g_ref[0, i * cum_rows:(i + 1) * cum_rows, :]
            lf = jnp.where(lane >= n_heads, _log_sigmoid(g), 0.0)
            cum = jnp.dot(trilf, lf, precision=HIGHEST, preferred_element_type=F32) + carry
            g_ref[0, i * cum_rows:(i + 1) * cum_rows, :] = jnp.where(lane >= n_heads, cum, g)
            carry = cum[cum_rows - 1:cum_rows, :]


def _conv_gates(xz, buf, conv_w, conv_b, w_xc, w_xm, b_gate):
    B, S, _ = xz.shape
    H, DH, G = w_xc.shape
    AI = H * DH
    CW = conv_w.shape[0]
    rows = _tile(S, 768)
    act_spec = pl.BlockSpec((1, S, DH), lambda b, h: (b, 0, h))
    w_spec = pl.BlockSpec((1, DH, G), lambda b, h: (h, 0, 0))
    kern = functools.partial(_conv_gates_kernel, rows=rows, cum_rows=_tile(S, 512, 8), conv_w=CW,
                             n_heads=H)
    return pl.pallas_call(
        kern, grid=(B, H),
        in_specs=[act_spec,
                  pl.BlockSpec((1, CONV_PAD, DH), lambda b, h: (b, 0, h)),
                  pl.BlockSpec((1, CW, DH), lambda b, h: (h, 0, 0)),
                  pl.BlockSpec((1, 1, DH), lambda b, h: (h, 0, 0)),
                  w_spec, w_spec,
                  pl.BlockSpec((1, G), lambda b, h: (0, 0))],
        out_specs=[act_spec, pl.BlockSpec((1, S, G), lambda b, h: (b, 0, 0))],
        out_shape=[jax.ShapeDtypeStruct((B, S, AI), BF16), jax.ShapeDtypeStruct((B, S, G), F32)],
        scratch_shapes=[pltpu.VMEM((S + CONV_PAD, DH), F32)],
        compiler_params=_params("parallel", "arbitrary"), name="conv_gates",
    )(xz, buf,
      conv_w.reshape(CW, H, DH).transpose(1, 0, 2).astype(F32),
      conv_b.reshape(H, 1, DH).astype(F32), w_xc, w_xm, b_gate.reshape(1, G).astype(F32))


def _in_conv_kernel(x_ref, g_ref, w_ref, wz_ref, buf_ref, cw_ref, cb_ref, wa_ref, wb_ref, bg_ref,
                    xm_ref, z_ref, xc_ref, gate_ref, xn_ref, hist_ref, *, conv_w, n_heads):
    i = pl.program_id(1)
    j = pl.program_id(2)
    tm = x_ref.shape[1]

    @pl.when(j == 0)
    def _():
        x = x_ref[0].astype(F32)
        ms = jnp.mean(x * x, axis=-1, keepdims=True)
        xn_ref[...] = (x * lax.rsqrt(ms + EPS) * g_ref[...]).astype(BF16)

    prev = jnp.where(i == 0, buf_ref[0], hist_ref[j])
    row = lax.broadcasted_iota(jnp.int32, prev.shape, 0)
    mid = tm // 2 // BF16_ROWS * BF16_ROWS
    parts = []
    for r0, r1 in ((0, mid), (mid, tm)) if mid else ((0, tm),):
        acc = _dot(xn_ref[r0:r1, :], w_ref[...])
        xm = acc.astype(BF16)
        xm_ref[0, r0:r1, :] = xm
        y = cb_ref[...] + acc * cw_ref[conv_w - 1:conv_w, :]
        for shift in range(1, conv_w):
            rolled = pltpu.roll(acc, shift, 0)
            head = jnp.where(row < shift, pltpu.roll(prev, shift, 0), rolled[0:CONV_PAD, :])
            shifted = jnp.concatenate([head, rolled[CONV_PAD:, :]], axis=0)
            y = y + shifted * cw_ref[conv_w - 1 - shift:conv_w - shift, :]
        xc = _silu(y).astype(BF16)
        xc_ref[0, r0:r1, :] = xc
        z_ref[0, r0:r1, :] = _dot(xn_ref[r0:r1, :], wz_ref[...]).astype(BF16)
        parts.append(_dot(xc, wa_ref[...]) + _dot(xm, wb_ref[...]))
        prev = acc[r1 - r0 - CONV_PAD:r1 - r0, :]
    hist_ref[j] = prev
    part = jnp.concatenate(parts, axis=0)

    @pl.when(j == 0)
    def _():
        gate_ref[0] = part + bg_ref[...]

    @pl.when(j > 0)
    def _():
        gate_ref[0] = gate_ref[0] + part

    @pl.when(j == pl.num_programs(2) - 1)
    def _():
        g = gate_ref[0]
        lane = lax.broadcasted_iota(jnp.int32, g.shape, 1)
        gate_ref[0] = jnp.where(lane >= n_heads, _log_sigmoid(g), g)


def _in_conv(x, norm_g, w_in, buf, conv_w, conv_b, w_xc, w_xm, b_gate):
    B, S, D = x.shape
    H, DH, G = w_xc.shape
    AI = H * DH
    CW = conv_w.shape[0]
    tm = _tile(S, 1024)
    tn = _tile(AI, 1024, LANES)
    act_spec = pl.BlockSpec((1, tm, tn), lambda b, i, j: (b, i, j))
    act = jax.ShapeDtypeStruct((B, S, AI), BF16)
    kern = functools.partial(_in_conv_kernel, conv_w=CW, n_heads=H)
    return pl.pallas_call(
        kern, grid=(B, S // tm, AI // tn),
        in_specs=[pl.BlockSpec((1, tm, D), lambda b, i, j: (b, i, 0)),
                  pl.BlockSpec((1, D), lambda b, i, j: (0, 0)),
                  pl.BlockSpec((D, tn), lambda b, i, j: (0, j)),
                  pl.BlockSpec((D, tn), lambda b, i, j: (0, AI // tn + j)),
                  pl.BlockSpec((1, CONV_PAD, tn), lambda b, i, j: (b, 0, j)),
                  pl.BlockSpec((CW, tn), lambda b, i, j: (0, j)),
                  pl.BlockSpec((1, tn), lambda b, i, j: (0, j)),
                  pl.BlockSpec((tn, G), lambda b, i, j: (j, 0)),
                  pl.BlockSpec((tn, G), lambda b, i, j: (j, 0)),
                  pl.BlockSpec((1, G), lambda b, i, j: (0, 0))],
        out_specs=[act_spec, act_spec, act_spec,
                   pl.BlockSpec((1, tm, G), lambda b, i, j: (b, i, 0))],
        out_shape=[act, act, act, jax.ShapeDtypeStruct((B, S, G), F32)],
        scratch_shapes=[pltpu.VMEM((tm, D), BF16), pltpu.VMEM((AI // tn, CONV_PAD, tn), F32)],
        compiler_params=_params("parallel", "arbitrary", "arbitrary"), name="in_conv",
    )(x, norm_g.reshape(1, D).astype(F32), w_in, w_in, buf, conv_w.astype(F32),
      conv_b.reshape(1, AI).astype(F32), w_xc.reshape(AI, G), w_xm.reshape(AI, G),
      b_gate.reshape(1, G).astype(F32))


def _mlstm_core(xm, xc, wq, wk, wv, g_rows, g_cols, head, n_heads, C_ref, n_ref, m_ref, b_ref):
    L = xm.shape[0]
    q = _dot(xc, wq).astype(BF16)
    k = _dot(xc, wk).astype(BF16)
    v = _dot(xm, wv).astype(BF16)
    row = lax.broadcasted_iota(jnp.int32, (L, L), 0)
    col = lax.broadcasted_iota(jnp.int32, (L, L), 1)
    tril = col <= row
    lane = lax.broadcasted_iota(jnp.int32, g_cols.shape, 1)
    ig_col = jnp.sum(jnp.where(lane == head, g_cols, 0.0), axis=-1, keepdims=True)
    f_col = jnp.sum(jnp.where(lane == head + n_heads, g_cols, 0.0), axis=-1, keepdims=True)
    ig_row = g_rows[0:1, :]
    f_row = g_rows[1:2, :]
    m_prev = m_ref[...]
    f_prev = b_ref[...]
    log_d = jnp.where(tril, f_col - f_row + ig_row, NEG_INF)
    inter = f_col - f_prev + m_prev
    m_t = jnp.maximum(jnp.max(log_d, axis=-1, keepdims=True), inter)
    d = jnp.exp(log_d - m_t)
    a = jnp.exp(inter - m_t)
    s = _dot_nt(q, k) * d
    C = C_ref[...]
    n = n_ref[...]
    num = _dot(s.astype(BF16), v) + a * _dot(q, C.astype(BF16))
    qn = _dot_nt(q, jnp.broadcast_to(n, (8, n.shape[1])).astype(BF16))[:, 0:1]
    den = jnp.sum(s, axis=-1, keepdims=True) + a * qn
    m_new = m_t[L - 1:L, :]
    f_last = f_row[:, L - 1:L]
    decay = jnp.exp(f_last - f_prev + m_prev - m_new)
    w_col = jnp.exp(f_last - f_col + ig_col - m_new)
    C_ref[...] = decay * C + _dot_tn(k, (w_col * v.astype(F32)).astype(BF16))
    n_ref[...] = decay * n + jnp.sum(w_col * k.astype(F32), axis=0, keepdims=True)
    m_ref[...] = m_new
    b_ref[...] = f_last
    return num * (1.0 / jnp.maximum(jnp.abs(den), jnp.exp(-m_t)))


def _mlstm_finish(hid, xc, z, og, skip):
    hc = hid - jnp.mean(hid, axis=-1, keepdims=True)
    hn = hc * lax.rsqrt(jnp.mean(hc * hc, axis=-1, keepdims=True) + EPS) * og
    return ((hn + skip * xc.astype(F32)) * _silu(z.astype(F32))).astype(BF16)


def _mlstm_kernel(*refs, lead, chunk, n_chunks, n_heads, has_state, merge_lead, head_axis):
    xm_ref, xc_ref, z_ref, wq_ref, wk_ref, wv_ref = refs[:6]
    pos = 6
    if lead:
        gl_ref = refs[pos]
        pos += 1
    gm_ref, gc_ref, og_ref, skip_ref = refs[pos:pos + 4]
    pos += 4
    if has_state:
        c0_ref, n0_ref, m0_ref = refs[pos:pos + 3]
        pos += 3
    pos += 1
    o_ref, C_ref, n_ref, m_ref, b_ref, hid_ref = refs[pos:pos + 6]
    C_st, n_st, m_st = C_ref.at[0, 0, 0], n_ref.at[0, 0], m_ref.at[0, 0]
    if has_state:
        C_st[...] = c0_ref[0, 0, 0]
        n_st[...] = n0_ref[0, 0, 0]
        m_st[...] = m0_ref[0, 0, 0]
    else:
        C_st[...] = jnp.zeros(C_st.shape, F32)
        n_st[...] = jnp.zeros(n_st.shape, F32)
        m_st[...] = jnp.zeros(m_st.shape, F32)
    b_ref[...] = jnp.zeros(b_ref.shape, F32)
    head = pl.program_id(head_axis)

    def chunk_rows(c):
        return pl.ds(pl.multiple_of(lead + c * chunk, BF16_ROWS), chunk)

    def core(rows, g_rows):
        return _mlstm_core(xm_ref[0, rows, :], xc_ref[0, rows, :], wq_ref[0], wk_ref[0], wv_ref[0],
                           g_rows, gc_ref[0, rows, :], head, n_heads, C_st, n_st, m_st, b_ref)

    def finish(rows, hid):
        o_ref[0, rows, :] = _mlstm_finish(hid, xc_ref[0, rows, :], z_ref[0, rows, :], og_ref[0],
                                          skip_ref[0])

    if merge_lead:
        first = pl.ds(0, lead + chunk)
        finish(first, core(first, gl_ref[0, 0]))
        hid_ref[0] = core(chunk_rows(1), gm_ref[0, 0, 1])

        def pair(p, carry):
            c = 2 * p + 2
            hid_ref[1] = core(chunk_rows(c), gm_ref[0, 0, c])
            finish(chunk_rows(c - 1), hid_ref[0])
            hid_ref[0] = core(chunk_rows(c + 1), gm_ref[0, 0, c + 1])
            finish(chunk_rows(c), hid_ref[1])
            return carry

        lax.fori_loop(0, n_chunks // 2 - 1, pair, 0)
        finish(chunk_rows(n_chunks - 1), hid_ref[0])
    else:
        if lead:
            finish(pl.ds(0, lead), core(pl.ds(0, lead), gl_ref[0, 0]))

        def body(c, carry):
            finish(chunk_rows(c), core(chunk_rows(c), gm_ref[0, 0, c]))
            return carry

        lax.fori_loop(0, n_chunks, body, 0, unroll=_tile(n_chunks, 2, 1))


def _mlstm(xm, z, z_head0, xc, gates, wq, wk, wv, out_g, skip, state, lead, chunk, c_stack, layer,
           n_layers):
    B, S, AI = xc.shape
    H, DH = out_g.shape
    n_chunks = (S - lead) // chunk
    assert lead + n_chunks * chunk == S
    g = gates.reshape(B, S, 2, H).transpose(0, 3, 2, 1)
    g_main = g[..., lead:].reshape(B, H, 2, n_chunks, chunk).transpose(0, 1, 3, 2, 4)
    has_state = state is not None
    heads_outer = has_state

    def spec(shape, index):
        return pl.BlockSpec(shape, (lambda h, b: index(b, h)) if heads_outer else index)

    act_spec = spec((1, S, DH), lambda b, h: (b, 0, h))
    w_spec = spec((1, DH, DH), lambda b, h: (h, 0, 0))
    in_specs = [act_spec, act_spec, spec((1, S, DH), lambda b, h: (b, 0, z_head0 + h)),
                w_spec, w_spec, w_spec]
    args = [xm, xc, z, wq, wk, wv]
    merge_lead = bool(lead) and n_chunks % 2 == 0
    if lead:
        n_first = lead + chunk if merge_lead else lead
        in_specs.append(spec((1, 1, 2, n_first), lambda b, h: (b, h, 0, 0)))
        args.append(g[..., :n_first])
    in_specs += [spec((1, 1, n_chunks, 2, chunk), lambda b, h: (b, h, 0, 0, 0)),
                 spec((1, S, 2 * H), lambda b, h: (b, 0, 0)),
                 spec((1, 1, DH), lambda b, h: (h, 0, 0)),
                 spec((1, 1, DH), lambda b, h: (h, 0, 0))]
    args += [g_main, gates, out_g.reshape(H, 1, DH).astype(F32), skip.reshape(H, 1, DH).astype(F32)]
    n_spec = spec((1, 1, 1, DH), lambda b, h: (b, h, 0, 0))
    m_spec = spec((1, 1, 1, 1), lambda b, h: (b, h, 0, 0))
    if has_state:
        C0, n0, m0 = state
        NL = C0.shape[0]
        in_specs += [spec((1, 1, 1, DH, DH), lambda b, h: (layer, b, h, 0, 0)),
                     spec((1, 1, 1, 1, DH), lambda b, h: (layer, b, h, 0, 0)),
                     spec((1, 1, 1, 1, 1), lambda b, h: (layer, b, h, 0, 0))]
        args += [C0.astype(F32), n0.reshape(NL, B, H, 1, DH).astype(F32),
                 m0.reshape(NL, B, H, 1, 1).astype(F32)]
    if c_stack is None:
        c_stack = pl.empty((n_layers, B, H, DH, DH), F32)
    aliases = {len(args): 1}
    in_specs.append(pl.BlockSpec(memory_space=pl.ANY))
    args.append(c_stack)
    kern = functools.partial(_mlstm_kernel, lead=lead, chunk=chunk, n_chunks=n_chunks, n_heads=H,
                             has_state=has_state, merge_lead=merge_lead,
                             head_axis=0 if heads_outer else 1)
    hz, C, n, m = pl.pallas_call(
        kern, grid=(H, B) if heads_outer else (B, H), in_specs=in_specs,
        out_specs=[act_spec,
                   spec((1, 1, 1, DH, DH), lambda b, h: (layer, b, h, 0, 0)),
                   n_spec, m_spec],
        out_shape=[jax.ShapeDtypeStruct((B, S, AI), BF16),
                   jax.ShapeDtypeStruct((n_layers, B, H, DH, DH), F32),
                   jax.ShapeDtypeStruct((B, H, 1, DH), F32),
                   jax.ShapeDtypeStruct((B, H, 1, 1), F32)],
        scratch_shapes=[pltpu.VMEM((1, 1), F32), pltpu.VMEM((2, chunk, DH), F32)],
        input_output_aliases=aliases,
        compiler_params=_params("parallel", "parallel"), name="mlstm",
    )(*args)
    return hz, C, n.reshape(B, H, DH), m.reshape(B, H)


def _cumsum_kernel(x_ref, o_ref, *, scale, first_lane):
    x = x_ref[0]
    S = x.shape[0]
    summed = lax.broadcasted_iota(jnp.int32, x.shape, 1) >= first_lane
    row = lax.broadcasted_iota(jnp.int32, x.shape, 0)
    y = jnp.where(summed, x, 0.0)
    step = 1
    while step < S:
        y = y + jnp.where(row >= step, pltpu.roll(y, step, 0), 0.0)
        step *= 2
    o_ref[0] = jnp.where(summed, y * scale, x)


def _cumsum(x, scale=1.0, first_lane=0):
    B, S, G = x.shape
    spec = pl.BlockSpec((1, S, G), lambda b: (b, 0, 0))
    return pl.pallas_call(
        functools.partial(_cumsum_kernel, scale=scale, first_lane=first_lane),
        grid=(B,), in_specs=[spec], out_specs=spec, out_shape=jax.ShapeDtypeStruct((B, S, G), F32),
        compiler_params=_params("parallel"), name="cumsum",
    )(x)


def _lane_fold(op, acc, x):
    for c in range(x.shape[1] // LANES):
        acc = op(acc, x[:, c * LANES:(c + 1) * LANES])
    return acc


def _fox_prompt_kernel(q_ref, z_ref, k_ref, v_ref, fq_ref, fkl_ref, fkm_ref, o_ref,
                       s_ref, sl_ref, fq_rep, m_ref, l_ref, acc_ref, *, lead, blk, n_blk, heads,
                       dim):
    n_lane = blk // LANES
    cols = [slice(hh * dim, (hh + 1) * dim) for hh in range(heads)]

    def key_rows(j):
        return pl.ds(pl.multiple_of(lead + j * blk, BF16_ROWS), blk)

    def tiled(x):
        return jnp.concatenate([x] * n_lane, axis=1)

    def q_block(i, carry):
        qrows = pl.ds(pl.multiple_of(i * blk, blk), blk)

        def logits(hh, j):
            return (_dot_nt(q_ref[0, qrows, cols[hh]], k_ref[0, key_rows(j), cols[hh]])
                    + tiled(fq_rep[hh]) - fkm_ref[0, 0, j, hh:hh + 1, :])

        for hh in range(heads):
            fq_rep[hh] = jnp.broadcast_to(fq_ref[0, 0, qrows, hh:hh + 1], (blk, LANES))
        if lead:
            for hh in range(heads):
                s = (_dot_nt(q_ref[0, qrows, cols[hh]], k_ref[0, 0:LANES, cols[hh]])
                     + fq_rep[hh] - fkl_ref[0, 0, hh:hh + 1, :])
                sl_ref[hh] = s
                m_ref[hh] = s
        else:
            m_ref[...] = jnp.full(m_ref.shape, NEG_INF, F32)

        def pass1(j, c):
            for hh in range(heads):
                s = logits(hh, j)
                s_ref[hh, j] = s
                m_ref[hh] = _lane_fold(jnp.maximum, m_ref[hh], s)
            return c

        lax.fori_loop(0, i, pass1, 0)
        row = lax.broadcasted_iota(jnp.int32, (blk, blk), 0)
        col = lax.broadcasted_iota(jnp.int32, (blk, blk), 1)
        for hh in range(heads):
            s = jnp.where(col <= row, logits(hh, i), NEG_INF)
            s_ref[hh, i] = s
            m = jnp.max(_lane_fold(jnp.maximum, m_ref[hh], s), axis=-1, keepdims=True)
            m_ref[hh] = jnp.broadcast_to(m, (blk, LANES))
        if lead:
            for hh in range(heads):
                p = jnp.exp2(sl_ref[hh] - m_ref[hh])
                l_ref[hh] = p
                acc_ref[hh] = _dot(p.astype(BF16), v_ref[0, 0:LANES, cols[hh]])
        else:
            l_ref[...] = jnp.zeros(l_ref.shape, F32)
            acc_ref[...] = jnp.zeros(acc_ref.shape, F32)

        def pass2(j, c):
            for hh in range(heads):
                p = jnp.exp2(s_ref[hh, j] - tiled(m_ref[hh]))
                l_ref[hh] = _lane_fold(jnp.add, l_ref[hh], p)
                acc_ref[hh] += _dot(p.astype(BF16), v_ref[0, key_rows(j), cols[hh]])
            return c

        lax.fori_loop(0, i + 1, pass2, 0)
        for hh in range(heads):
            l = jnp.sum(l_ref[hh], axis=-1, keepdims=True)
            o_ref[0, qrows, cols[hh]] = (acc_ref[hh] * (1.0 / l)
                                         * _silu(z_ref[0, qrows, cols[hh]].astype(F32))).astype(BF16)
        return carry

    lax.fori_loop(0, n_blk, q_block, 0)


def _fox_prompt(q, z, kb, vb, f_cum, lead, heads_per_step=4, blk=512):
    B, SQ, W = q.shape
    NH = f_cum.shape[-1]
    dim = W // NH
    hg = min(heads_per_step, NH)
    G = NH // hg
    blk = _tile(SQ, blk, LANES)
    n_blk = SQ // blk
    SK = lead + SQ
    assert lead <= LANES <= SK
    fq = f_cum[:, lead:].reshape(B, SQ, G, hg).transpose(0, 2, 1, 3)
    fk = f_cum.transpose(0, 2, 1).reshape(B, G, hg, SK)
    fk_lead = jnp.pad(fk[..., :lead], ((0, 0), (0, 0), (0, 0), (0, LANES - lead)),
                      constant_values=float("inf"))
    fk_main = fk[..., lead:].reshape(B, G, hg, n_blk, blk).transpose(0, 1, 3, 2, 4)
    wcols = hg * dim
    kern = functools.partial(_fox_prompt_kernel, lead=lead, blk=blk, n_blk=n_blk, heads=hg, dim=dim)
    return pl.pallas_call(
        kern, grid=(B, G),
        in_specs=[pl.BlockSpec((1, SQ, wcols), lambda b, g: (b, 0, g)),
                  pl.BlockSpec((1, SQ, wcols), lambda b, g: (b, 0, g)),
                  pl.BlockSpec((1, SK, wcols), lambda b, g: (b, 0, g)),
                  pl.BlockSpec((1, SK, wcols), lambda b, g: (b, 0, g)),
                  pl.BlockSpec((1, 1, SQ, hg), lambda b, g: (b, g, 0, 0)),
                  pl.BlockSpec((1, 1, hg, LANES), lambda b, g: (b, g, 0, 0)),
                  pl.BlockSpec((1, 1, n_blk, hg, blk), lambda b, g: (b, g, 0, 0, 0))],
        out_specs=pl.BlockSpec((1, SQ, wcols), lambda b, g: (b, 0, g)),
        out_shape=jax.ShapeDtypeStruct((B, SQ, W), BF16),
        scratch_shapes=[pltpu.VMEM((hg, n_blk, blk, blk), F32)]
        + [pltpu.VMEM((hg, blk, LANES), F32)] * 4 + [pltpu.VMEM((hg, blk, dim), F32)],
        compiler_params=_params("parallel", "parallel"), name="fox_prompt",
    )(q, z, kb, vb, fq, fk_lead, fk_main)


def _pack_heads_kernel(x_ref, o_ref):
    o_ref[0] = pltpu.einshape("khd->k(hd)", x_ref[0]).astype(o_ref.dtype)


def _pack_heads(x, dtype):
    B, P, NH, dim = x.shape
    rows = _tile(P, 512)
    return pl.pallas_call(
        _pack_heads_kernel, grid=(B, P // rows),
        in_specs=[pl.BlockSpec((1, rows, NH, dim), lambda b, i: (b, i, 0, 0))],
        out_specs=pl.BlockSpec((1, rows, NH * dim), lambda b, i: (b, i, 0)),
        out_shape=jax.ShapeDtypeStruct((B, P, NH * dim), dtype),
        compiler_params=_params("parallel", "parallel"), name="pack_heads",
    )(x)


def _fox_decode_kernel(q_ref, z_ref, ck_ref, cv_ref, k_ref, v_ref, fq_ref, fkc_ref, fkn_ref, o_ref,
                       *, heads, dim):
    Q = q_ref.shape[1]
    row = lax.broadcasted_iota(jnp.int32, (Q, Q), 0)
    col = lax.broadcasted_iota(jnp.int32, (Q, Q), 1)
    outs = []
    for hh in range(heads):
        cs = slice(hh * dim, (hh + 1) * dim)
        q = q_ref[0, :, cs]
        fq = fq_ref[0, 0, :, hh:hh + 1]
        s_c = _dot_nt(q, ck_ref[0, :, cs].astype(BF16)) + fq - fkc_ref[0, 0, hh:hh + 1, :]
        s_n = jnp.where(col <= row, _dot_nt(q, k_ref[0, :, cs]) + fq - fkn_ref[0, 0, hh:hh + 1, :],
                        NEG_INF)
        m = jnp.maximum(jnp.max(s_c, axis=-1, keepdims=True), jnp.max(s_n, axis=-1, keepdims=True))
        p_c = jnp.exp2(s_c - m)
        p_n = jnp.exp2(s_n - m)
        l = jnp.sum(p_c, axis=-1, keepdims=True) + jnp.sum(p_n, axis=-1, keepdims=True)
        acc = _dot(p_c.astype(BF16), cv_ref[0, :, cs].astype(BF16)) + _dot(p_n.astype(BF16), v_ref[0, :, cs])
        outs.append((acc * (1.0 / l) * _silu(z_ref[0, :, cs].astype(F32))).astype(BF16))
    o_ref[0] = jnp.concatenate(outs, axis=-1)


def _fox_decode(q, z, cache_k, cache_v, kb, vb, f_cum, heads_per_step=4):
    B, Q, W = q.shape
    P = cache_k.shape[1]
    NH = f_cum.shape[-1]
    dim = W // NH
    hg = min(heads_per_step, NH)
    G = NH // hg
    fq = f_cum[:, P:].reshape(B, Q, G, hg).transpose(0, 2, 1, 3)
    fk = f_cum.transpose(0, 2, 1).reshape(B, G, hg, P + Q)
    wcols = hg * dim
    kern = functools.partial(_fox_decode_kernel, heads=hg, dim=dim)
    return pl.pallas_call(
        kern, grid=(B, G),
        in_specs=[pl.BlockSpec((1, Q, wcols), lambda b, g: (b, 0, g)),
                  pl.BlockSpec((1, Q, wcols), lambda b, g: (b, 0, g)),
                  pl.BlockSpec((1, P, wcols), lambda b, g: (b, 0, g)),
                  pl.BlockSpec((1, P, wcols), lambda b, g: (b, 0, g)),
                  pl.BlockSpec((1, Q, wcols), lambda b, g: (b, 0, g)),
                  pl.BlockSpec((1, Q, wcols), lambda b, g: (b, 0, g)),
                  pl.BlockSpec((1, 1, Q, hg), lambda b, g: (b, g, 0, 0)),
                  pl.BlockSpec((1, 1, hg, P), lambda b, g: (b, g, 0, 0)),
                  pl.BlockSpec((1, 1, hg, Q), lambda b, g: (b, g, 0, 0))],
        out_specs=pl.BlockSpec((1, Q, wcols), lambda b, g: (b, 0, g)),
        out_shape=jax.ShapeDtypeStruct((B, Q, W), BF16),
        compiler_params=_params("parallel", "parallel"), name="fox_decode",
    )(q, z, cache_k, cache_v, kb, vb, fq, fk[..., :P], fk[..., P:])


def _mlstm_layer(x, conv_hist, state, w, lead, chunk, c_stack, layer, n_layers):
    (norm_g, w_in, conv_w, conv_b, wq, wk, wv, w_xc, w_xm, b_gate, out_g, skip, w_out) = w
    B, S, _ = x.shape
    H, DH, _ = wq.shape
    AI = H * DH
    CW = conv_w.shape[0]
    assert S >= CW - 1 and CW - 1 <= CONV_PAD
    buf = jnp.zeros((B, CONV_PAD, AI), F32)
    if conv_hist is not None:
        buf = buf.at[:, CONV_PAD - (CW - 1):].set(conv_hist.astype(F32))
    if S >= MIN_FUSED_CONV_ROWS:
        wb = w_in.astype(BF16)
        xm, z, xc, gates = _in_conv(x, norm_g, wb, buf, conv_w, conv_b,
                                    w_xc.astype(BF16), w_xm.astype(BF16), b_gate)
        gates = _cumsum(gates, first_lane=H)
        z_head0 = 0
    else:
        xm = _norm_matmul(x, norm_g, w_in.astype(BF16))
        xc, gates = _conv_gates(xm, buf, conv_w, conv_b, w_xc.astype(BF16), w_xm.astype(BF16),
                                b_gate)
        z, z_head0 = xm, H
    hz, c_stack, n, m = _mlstm(
        xm, z, z_head0, xc, gates, wq.astype(BF16), (wk * float(DH) ** -0.5).astype(BF16),
        wv.astype(BF16), out_g, skip.reshape(H, DH), state, lead, chunk, c_stack, layer, n_layers)
    x_new = _matmul_residual(hz, w_out.astype(BF16), x)
    new_hist = xm[:, S - (CW - 1):, :AI].astype(F32)
    return x_new, new_hist, c_stack, n, m


def _shared_kv(x, kv_norm_g, w_kvf, b_f, k_norm_g):
    NH, dim = k_norm_g.shape
    W = NH * dim
    wb = w_kvf.astype(BF16)
    k32, kb, v32, vb, logf = _dual_norm_matmul(
        x, kv_norm_g, wb, W, k_norm_g.reshape(1, W), a_dtypes=[F32, BF16],
        b_dtypes=[F32, BF16], head_dim=dim, split_heads_first=True, gate_w=wb[:, 2 * W:],
        gate_b=b_f)
    return k32, kb, v32, vb, logf


def _fox_in(x, norm_g, w_in, q_norm_g):
    NH, dim = q_norm_g.shape
    W = NH * dim
    wb = w_in.astype(BF16)
    gain = q_norm_g.reshape(1, W).astype(F32) * (float(dim) ** -0.5 * LOG2E)
    q, z = _dual_norm_matmul(x, norm_g, wb, W, gain, a_dtypes=[BF16],
                             b_dtypes=[BF16], head_dim=dim)
    return q, z


def kernel(x_prompt, x_sample, cache_k, cache_v, cache_logf, state_C, state_n, state_m, state_conv,
           meta_tokens, a_norm_g, a_w_in, a_conv_w, a_conv_b, a_wq, a_wk, a_wv, a_w_gate, a_b_gate,
           a_out_g, a_skip, a_w_out, kv_norm_g, w_kvf, b_f, k_norm_g, b_norm_g, b_w_in, q_norm_g,
           b_w_out):
    B, SEQ, D = x_prompt.shape
    DB, DEC, _ = x_sample.shape
    NM = meta_tokens.shape[0]
    N_A = a_norm_g.shape[0]
    N_B = b_norm_g.shape[0]
    NH, dim = k_norm_g.shape
    W = NH * dim
    P = cache_k.shape[1]
    chunk = _tile(SEQ, 256)

    xp = jnp.concatenate(
        [jnp.broadcast_to(meta_tokens.astype(x_prompt.dtype)[None], (B, NM, D)), x_prompt], axis=1)
    xs = x_sample
    p_state, s_state = [], []
    p_C = s_C = None
    w_xc, w_xm = _fold_gate_weights(a_wq, a_wk, a_wv, a_w_gate)
    for i in range(N_A):
        w = (a_norm_g[i], a_w_in[i], a_conv_w[i], a_conv_b[i], a_wq[i], a_wk[i], a_wv[i],
             w_xc[i], w_xm[i], a_b_gate[i], a_out_g[i], a_skip[i], a_w_out[i])
        xp, hist, p_C, n, m = _mlstm_layer(xp, None, None, w, NM, chunk, p_C, i, N_A)
        p_state.append((hist, n, m))
        xs, hist, s_C, n, m = _mlstm_layer(xs, state_conv[i], (state_C, state_n, state_m),
                                           w, 0, DEC, s_C, i, N_A)
        s_state.append((hist, n, m))

    pk32, pkb, pv32, pvb, p_logf = _shared_kv(xp, kv_norm_g, w_kvf, b_f, k_norm_g)
    sk32, skb, sv32, svb, s_logf = _shared_kv(xs, kv_norm_g, w_kvf, b_f, k_norm_g)
    fp = _cumsum(p_logf, LOG2E)
    fs = _cumsum(jnp.concatenate([cache_logf.astype(F32), s_logf], axis=1), LOG2E)
    ck = _pack_heads(cache_k, BF16)
    cv = _pack_heads(cache_v, BF16)
    xp = xp[:, NM:]
    for j in range(N_B):
        q, z = _fox_in(xp, b_norm_g[j], b_w_in[j], q_norm_g[j])
        xp = _matmul_residual(_fox_prompt(q, z, pkb, pvb, fp, NM), b_w_out[j].astype(BF16), xp)
        q, z = _fox_in(xs, b_norm_g[j], b_w_in[j], q_norm_g[j])
        xs = _matmul_residual(_fox_decode(q, z, ck, cv, skb, svb, fs), b_w_out[j].astype(BF16), xs)

    def stack(states, idx):
        return jnp.stack([st[idx] for st in states])

    return (xp, xs, pk32, pv32, p_logf,
            p_C, stack(p_state, 1), stack(p_state, 2), stack(p_state, 0),
            sk32, sv32, s_logf,
            s_C, stack(s_state, 1), stack(s_state, 2), stack(s_state, 0))
```

```python
import functools

import jax
import jax.numpy as jnp
from jax import lax
from jax.experimental import pallas as pl
from jax.experimental.pallas import tpu as pltpu

F32 = jnp.float32
BF16 = jnp.bfloat16
EPS = 1e-6
CONV_PAD = 8
BF16_ROWS = 16
LANES = 128
MIN_FUSED_CONV_ROWS = 256
VMEM_LIMIT_BYTES = 56 * 1024 * 1024
HIGHEST = lax.Precision.HIGHEST
NEG_INF = float("-inf")
LOG2E = 1.4426950408889634


def _params(*sem):
    return pltpu.CompilerParams(dimension_semantics=sem, vmem_limit_bytes=VMEM_LIMIT_BYTES)


def _tile(n, cap, mult=BF16_ROWS):
    best = None
    for t in range(mult, min(n, cap) + 1, mult):
        if n % t == 0:
            best = t
    return best if best is not None else n


def _dot(a, b):
    return jnp.dot(a, b, preferred_element_type=F32)


def _dot_nt(a, b):
    return lax.dot_general(a, b, (((1,), (1,)), ((), ())), preferred_element_type=F32)


def _dot_tn(a, b):
    return lax.dot_general(a, b, (((0,), (0,)), ((), ())), preferred_element_type=F32)


def _silu(x):
    return x / (1.0 + jnp.exp(-x))


def _log_sigmoid(x):
    return jnp.minimum(x, 0.0) - jnp.log1p(jnp.exp(-jnp.abs(x)))


def _norm_matmul_kernel(x_ref, g_ref, w_ref, o_ref, xn_ref):
    @pl.when(pl.program_id(2) == 0)
    def _():
        x = x_ref[0].astype(F32)
        ms = jnp.mean(x * x, axis=-1, keepdims=True)
        xn_ref[...] = (x * lax.rsqrt(ms + EPS) * g_ref[...]).astype(BF16)

    o_ref[0] = _dot(xn_ref[...], w_ref[...]).astype(o_ref.dtype)


def _norm_matmul(x, g, w):
    lead_shape = x.shape[:2]
    x = x.reshape(1, -1, x.shape[-1])
    B, S, D = x.shape
    N = w.shape[1]
    tm = _tile(S, 1024)
    tn = _tile(N, 1024, LANES)
    out = pl.pallas_call(
        _norm_matmul_kernel, grid=(B, S // tm, N // tn),
        in_specs=[pl.BlockSpec((1, tm, D), lambda b, i, j: (b, i, 0)),
                  pl.BlockSpec((1, D), lambda b, i, j: (0, 0)),
                  pl.BlockSpec((D, tn), lambda b, i, j: (0, j))],
        out_specs=pl.BlockSpec((1, tm, tn), lambda b, i, j: (b, i, j)),
        out_shape=jax.ShapeDtypeStruct((B, S, N), BF16),
        scratch_shapes=[pltpu.VMEM((tm, D), BF16)],
        compiler_params=_params("parallel", "parallel", "arbitrary"), name="norm_matmul",
    )(x, g.reshape(1, D).astype(F32), w)
    return out.reshape(lead_shape + (N,))


def _dual_norm_matmul_kernel(*refs, n_a, n_b, head_dim, has_gate):
    x_ref, g_ref, wa_ref, gain_ref, wb_ref = refs[:5]
    pos = 5
    if has_gate:
        gw_ref, gb_ref = refs[pos:pos + 2]
        pos += 2
    a_refs = refs[pos:pos + n_a]
    b_refs = refs[pos + n_a:pos + n_a + n_b]
    pos += n_a + n_b
    if has_gate:
        gate_out_ref = refs[pos]
        pos += 1
    xn_ref = refs[pos]

    @pl.when(pl.program_id(2) == 0)
    def _():
        x = x_ref[0].astype(F32)
        ms = jnp.mean(x * x, axis=-1, keepdims=True)
        xn_ref[...] = (x * lax.rsqrt(ms + EPS) * g_ref[...]).astype(BF16)
        if has_gate:
            gate_out_ref[0] = _log_sigmoid(_dot(xn_ref[...], gw_ref[...]) + gb_ref[...])

    def store(out_refs, val):
        for o in out_refs:
            if len(o.shape) == 4:
                o[0] = val.reshape(o.shape[1:]).astype(o.dtype)
            else:
                o[0] = val.astype(o.dtype)

    acc = _dot(xn_ref[...], wa_ref[...])
    cols = []
    for c in range(acc.shape[1] // head_dim):
        blk = acc[:, c * head_dim:(c + 1) * head_dim]
        ms = jnp.mean(blk * blk, axis=-1, keepdims=True)
        cols.append(blk * lax.rsqrt(ms + EPS))
    store(a_refs, jnp.concatenate(cols, axis=-1) * gain_ref[...])
    store(b_refs, _dot(xn_ref[...], wb_ref[...]))


def _dual_norm_matmul(x, g, w, N, gain_a, *, a_dtypes, b_dtypes, head_dim, split_heads_first=False,
                      gate_w=None, gate_b=None):
    lead_shape = x.shape[:2]
    x = x.reshape(1, -1, x.shape[-1])
    B, S, D = x.shape
    tm = _tile(S, 1024)
    tn = _tile(N, 1024, LANES)
    has_gate = gate_w is not None
    in_specs = [pl.BlockSpec((1, tm, D), lambda b, i, j: (b, i, 0)),
                pl.BlockSpec((1, D), lambda b, i, j: (0, 0)),
                pl.BlockSpec((D, tn), lambda b, i, j: (0, j)),
                pl.BlockSpec((1, tn), lambda b, i, j: (0, j)),
                pl.BlockSpec((D, tn), lambda b, i, j: (0, N // tn + j))]
    args = [x, g.reshape(1, D).astype(F32), w, gain_a.astype(F32), w]
    if has_gate:
        G = gate_w.shape[1]
        in_specs += [pl.BlockSpec((D, G), lambda b, i, j: (0, 0)),
                     pl.BlockSpec((1, G), lambda b, i, j: (0, 0))]
        args += [gate_w, gate_b.reshape(1, G).astype(F32)]
    out_shape, out_specs = [], []
    for dtypes in (a_dtypes, b_dtypes):
        for k, dt in enumerate(dtypes):
            if split_heads_first and k == 0:
                out_shape.append(jax.ShapeDtypeStruct((B, S, N // head_dim, head_dim), dt))
                out_specs.append(pl.BlockSpec((1, tm, tn // head_dim, head_dim),
                                              lambda b, i, j: (b, i, j, 0)))
            else:
                out_shape.append(jax.ShapeDtypeStruct((B, S, N), dt))
                out_specs.append(pl.BlockSpec((1, tm, tn), lambda b, i, j: (b, i, j)))
    if has_gate:
        out_shape.append(jax.ShapeDtypeStruct((B, S, G), F32))
        out_specs.append(pl.BlockSpec((1, tm, G), lambda b, i, j: (b, i, 0)))
    kern = functools.partial(_dual_norm_matmul_kernel, n_a=len(a_dtypes), n_b=len(b_dtypes),
                             head_dim=head_dim, has_gate=has_gate)
    outs = pl.pallas_call(
        kern, grid=(B, S // tm, N // tn), in_specs=in_specs, out_specs=out_specs,
        out_shape=out_shape, scratch_shapes=[pltpu.VMEM((tm, D), BF16)],
        compiler_params=_params("parallel", "parallel", "arbitrary"), name="dual_norm_matmul",
    )(*args)
    return [o.reshape(lead_shape + o.shape[2:]) for o in outs]


def _matmul_residual_kernel(a_ref, w_ref, x_ref, o_ref):
    o_ref[0] = x_ref[0] + _dot(a_ref[0], w_ref[...])


def _matmul_residual(a, w, x):
    out_shape = x.shape
    a = a.reshape(1, -1, a.shape[-1])
    x = x.reshape(1, -1, x.shape[-1])
    B, S, K = a.shape
    N = w.shape[1]
    tm = _tile(S, 1024)
    tn = _tile(N, 1024, 128)
    return pl.pallas_call(
        _matmul_residual_kernel, grid=(B, S // tm, N // tn),
        in_specs=[pl.BlockSpec((1, tm, K), lambda b, i, j: (b, i, 0)),
                  pl.BlockSpec((K, tn), lambda b, i, j: (0, j)),
                  pl.BlockSpec((1, tm, tn), lambda b, i, j: (b, i, j))],
        out_specs=pl.BlockSpec((1, tm, tn), lambda b, i, j: (b, i, j)),
        out_shape=jax.ShapeDtypeStruct((B, S, N), F32),
        compiler_params=_params("parallel", "parallel", "arbitrary"), name="matmul_residual",
    )(a, w, x).reshape(out_shape)


def _fold_gate_kernel(wq_ref, wk_ref, wv_ref, wg_ref, a_ref, b_ref, *, k_scale):
    def hdot(a, b):
        return jnp.dot(a, b, precision=HIGHEST, preferred_element_type=F32)

    a_ref[0, 0] = hdot(wq_ref[0, 0], wg_ref[0, 0, 0]) + hdot(wk_ref[0, 0], wg_ref[0, 1, 0]) * k_scale
    b_ref[0, 0] = hdot(wv_ref[0, 0], wg_ref[0, 2, 0])


def _fold_gate_weights(wq, wk, wv, w_gate):
    NA, H, DH, _ = wq.shape
    G = w_gate.shape[-1]
    w_spec = pl.BlockSpec((1, 1, DH, DH), lambda i, h: (i, h, 0, 0))
    o_spec = pl.BlockSpec((1, 1, DH, G), lambda i, h: (i, h, 0, 0))
    out = jax.ShapeDtypeStruct((NA, H, DH, G), F32)
    return pl.pallas_call(
        functools.partial(_fold_gate_kernel, k_scale=float(DH) ** -0.5), grid=(NA, H),
        in_specs=[w_spec, w_spec, w_spec,
                  pl.BlockSpec((1, 3, 1, DH, G), lambda i, h: (i, 0, h, 0, 0))],
        out_specs=[o_spec, o_spec], out_shape=[out, out],
        compiler_params=_params("parallel", "parallel"), name="fold_gate_weights",
    )(wq, wk, wv, w_gate.reshape(NA, 3, H, DH, G))


def _conv_gates_kernel(xm_ref, buf_ref, cw_ref, cb_ref, wa_ref, wb_ref, bg_ref,
                       xc_ref, g_ref, seq_ref, *, rows, cum_rows, conv_w, n_heads):
    h = pl.program_id(1)
    S = xm_ref.shape[1]
    seq_ref[0:CONV_PAD, :] = buf_ref[0]
    seq_ref[CONV_PAD:CONV_PAD + S, :] = xm_ref[0].astype(F32)
    for r in range(S // rows):
        base = r * rows
        acc = jnp.broadcast_to(cb_ref[0], (rows, cb_ref.shape[-1]))
        for i in range(conv_w):
            start = base + CONV_PAD - (conv_w - 1) + i
            acc = acc + seq_ref[start:start + rows, :] * cw_ref[0, i:i + 1, :]
        xc = _silu(acc).astype(BF16)
        xc_ref[0, base:base + rows, :] = xc
        part = _dot(xc, wa_ref[0]) + _dot(xm_ref[0, base:base + rows, :], wb_ref[0])

        @pl.when(h == 0)
        def _():
            g_ref[0, base:base + rows, :] = part + bg_ref[...]

        @pl.when(h > 0)
        def _():
            g_ref[0, base:base + rows, :] = g_ref[0, base:base + rows, :] + part

    @pl.when(h == n_heads - 1)
    def _():
        r = lax.broadcasted_iota(jnp.int32, (cum_rows, cum_rows), 0)
        c = lax.broadcasted_iota(jnp.int32, (cum_rows, cum_rows), 1)
        trilf = (c <= r).astype(F32)
        lane = lax.broadcasted_iota(jnp.int32, (cum_rows, g_ref.shape[-1]), 1)
        carry = jnp.zeros((1, g_ref.shape[-1]), F32)
        for i in range(S // cum_rows):
            g = g_ref[0, i * cum_rows:(i + 1) * cum_rows, :]
            lf = jnp.where(lane >= n_heads, _log_sigmoid(g), 0.0)
            cum = jnp.dot(trilf, lf, precision=HIGHEST, preferred_element_type=F32) + carry
            g_ref[0, i * cum_rows:(i + 1) * cum_rows, :] = jnp.where(lane >= n_heads, cum, g)
            carry = cum[cum_rows - 1:cum_rows, :]


def _conv_gates(xz, buf, conv_w, conv_b, w_xc, w_xm, b_gate):
    B, S, _ = xz.shape
    H, DH, G = w_xc.shape
    AI = H * DH
    CW = conv_w.shape[0]
    rows = _tile(S, 768)
    act_spec = pl.BlockSpec((1, S, DH), lambda b, h: (b, 0, h))
    w_spec = pl.BlockSpec((1, DH, G), lambda b, h: (h, 0, 0))
    kern = functools.partial(_conv_gates_kernel, rows=rows, cum_rows=_tile(S, 512, 8), conv_w=CW,
                             n_heads=H)
    return pl.pallas_call(
        kern, grid=(B, H),
        in_specs=[act_spec,
                  pl.BlockSpec((1, CONV_PAD, DH), lambda b, h: (b, 0, h)),
                  pl.BlockSpec((1, CW, DH), lambda b, h: (h, 0, 0)),
                  pl.BlockSpec((1, 1, DH), lambda b, h: (h, 0, 0)),
                  w_spec, w_spec,
                  pl.BlockSpec((1, G), lambda b, h: (0, 0))],
        out_specs=[act_spec, pl.BlockSpec((1, S, G), lambda b, h: (b, 0, 0))],
        out_shape=[jax.ShapeDtypeStruct((B, S, AI), BF16), jax.ShapeDtypeStruct((B, S, G), F32)],
        scratch_shapes=[pltpu.VMEM((S + CONV_PAD, DH), F32)],
        compiler_params=_params("parallel", "arbitrary"), name="conv_gates",
    )(xz, buf,
      conv_w.reshape(CW, H, DH).transpose(1, 0, 2).astype(F32),
      conv_b.reshape(H, 1, DH).astype(F32), w_xc, w_xm, b_gate.reshape(1, G).astype(F32))


def _in_conv_kernel(x_ref, g_ref, w_ref, wz_ref, buf_ref, cw_ref, cb_ref, wa_ref, wb_ref, bg_ref,
                    xm_ref, z_ref, xc_ref, gate_ref, xn_ref, hist_ref, *, conv_w, n_heads):
    i = pl.program_id(1)
    j = pl.program_id(2)
    tm = x_ref.shape[1]

    @pl.when(j == 0)
    def _():
        x = x_ref[0].astype(F32)
        ms = jnp.mean(x * x, axis=-1, keepdims=True)
        xn_ref[...] = (x * lax.rsqrt(ms + EPS) * g_ref[...]).astype(BF16)

    prev = jnp.where(i == 0, buf_ref[0], hist_ref[j])
    row = lax.broadcasted_iota(jnp.int32, prev.shape, 0)
    mid = tm // 2 // BF16_ROWS * BF16_ROWS
    parts = []
    for r0, r1 in ((0, mid), (mid, tm)) if mid else ((0, tm),):
        acc = _dot(xn_ref[r0:r1, :], w_ref[...])
        xm = acc.astype(BF16)
        xm_ref[0, r0:r1, :] = xm
        y = cb_ref[...] + acc * cw_ref[conv_w - 1:conv_w, :]
        for shift in range(1, conv_w):
            rolled = pltpu.roll(acc, shift, 0)
            head = jnp.where(row < shift, pltpu.roll(prev, shift, 0), rolled[0:CONV_PAD, :])
            shifted = jnp.concatenate([head, rolled[CONV_PAD:, :]], axis=0)
            y = y + shifted * cw_ref[conv_w - 1 - shift:conv_w - shift, :]
        xc = _silu(y).astype(BF16)
        xc_ref[0, r0:r1, :] = xc
        z_ref[0, r0:r1, :] = _dot(xn_ref[r0:r1, :], wz_ref[...]).astype(BF16)
        parts.append(_dot(xc, wa_ref[...]) + _dot(xm, wb_ref[...]))
        prev = acc[r1 - r0 - CONV_PAD:r1 - r0, :]
    hist_ref[j] = prev
    part = jnp.concatenate(parts, axis=0)

    @pl.when(j == 0)
    def _():
        gate_ref[0] = part + bg_ref[...]

    @pl.when(j > 0)
    def _():
        gate_ref[0] = gate_ref[0] + part

    @pl.when(j == pl.num_programs(2) - 1)
    def _():
        g = gate_ref[0]
        lane = lax.broadcasted_iota(jnp.int32, g.shape, 1)
        gate_ref[0] = jnp.where(lane >= n_heads, _log_sigmoid(g), g)


def _in_conv(x, norm_g, w_in, buf, conv_w, conv_b, w_xc, w_xm, b_gate):
    B, S, D = x.shape
    H, DH, G = w_xc.shape
    AI = H * DH
    CW = conv_w.shape[0]
    tm = _tile(S, 1024)
    tn = _tile(AI, 1024, LANES)
    act_spec = pl.BlockSpec((1, tm, tn), lambda b, i, j: (b, i, j))
    act = jax.ShapeDtypeStruct((B, S, AI), BF16)
    kern = functools.partial(_in_conv_kernel, conv_w=CW, n_heads=H)
    return pl.pallas_call(
        kern, grid=(B, S // tm, AI // tn),
        in_specs=[pl.BlockSpec((1, tm, D), lambda b, i, j: (b, i, 0)),
                  pl.BlockSpec((1, D), lambda b, i, j: (0, 0)),
                  pl.BlockSpec((D, tn), lambda b, i, j: (0, j)),
                  pl.BlockSpec((D, tn), lambda b, i, j: (0, AI // tn + j)),
                  pl.BlockSpec((1, CONV_PAD, tn), lambda b, i, j: (b, 0, j)),
                  pl.BlockSpec((CW, tn), lambda b, i, j: (0, j)),
                  pl.BlockSpec((1, tn), lambda b, i, j: (0, j)),
                  pl.BlockSpec((tn, G), lambda b, i, j: (j, 0)),
                  pl.BlockSpec((tn, G), lambda b, i, j: (j, 0)),
                  pl.BlockSpec((1, G), lambda b, i, j: (0, 0))],
        out_specs=[act_spec, act_spec, act_spec,
                   pl.BlockSpec((1, tm, G), lambda b, i, j: (b, i, 0))],
        out_shape=[act, act, act, jax.ShapeDtypeStruct((B, S, G), F32)],
        scratch_shapes=[pltpu.VMEM((tm, D), BF16), pltpu.VMEM((AI // tn, CONV_PAD, tn), F32)],
        compiler_params=_params("parallel", "arbitrary", "arbitrary"), name="in_conv",
    )(x, norm_g.reshape(1, D).astype(F32), w_in, w_in, buf, conv_w.astype(F32),
      conv_b.reshape(1, AI).astype(F32), w_xc.reshape(AI, G), w_xm.reshape(AI, G),
      b_gate.reshape(1, G).astype(F32))


def _mlstm_core(xm, xc, wq, wk, wv, g_rows, g_cols, head, n_heads, C_ref, n_ref, m_ref, b_ref):
    L = xm.shape[0]
    q = _dot(xc, wq).astype(BF16)
    k = _dot(xc, wk).astype(BF16)
    v = _dot(xm, wv).astype(BF16)
    row = lax.broadcasted_iota(jnp.int32, (L, L), 0)
    col = lax.broadcasted_iota(jnp.int32, (L, L), 1)
    tril = col <= row
    lane = lax.broadcasted_iota(jnp.int32, g_cols.shape, 1)
    ig_col = jnp.sum(jnp.where(lane == head, g_cols, 0.0), axis=-1, keepdims=True)
    f_col = jnp.sum(jnp.where(lane == head + n_heads, g_cols, 0.0), axis=-1, keepdims=True)
    ig_row = g_rows[0:1, :]
    f_row = g_rows[1:2, :]
    m_prev = m_ref[...]
    f_prev = b_ref[...]
    log_d = jnp.where(tril, f_col - f_row + ig_row, NEG_INF)
    inter = f_col - f_prev + m_prev
    m_t = jnp.maximum(jnp.max(log_d, axis=-1, keepdims=True), inter)
    d = jnp.exp(log_d - m_t)
    a = jnp.exp(inter - m_t)
    s = _dot_nt(q, k) * d
    n = n_ref[...]
    qn = _dot_nt(q, jnp.broadcast_to(n, (8, n.shape[1])).astype(BF16))[:, 0:1]
    den = jnp.sum(s, axis=-1, keepdims=True) + a * qn
    inv_den = 1.0 / jnp.maximum(jnp.abs(den), jnp.exp(-m_t))
    m_new = m_t[L - 1:L, :]
    f_last = f_row[:, L - 1:L]
    decay = jnp.exp(f_last - f_prev + m_prev - m_new)
    w_col = jnp.exp(f_last - f_col + ig_col - m_new)
    sb = s.astype(BF16)
    wv = (w_col * v.astype(F32)).astype(BF16)
    hid = []
    half = C_ref.shape[1] // 2
    for cs in (slice(0, half), slice(half, 2 * half)):
        C = C_ref[:, cs]
        hid.append((_dot(sb, v[:, cs]) + a * _dot(q, C.astype(BF16))) * inv_den)
        C_ref[:, cs] = decay * C + _dot_tn(k, wv[:, cs])
    n_ref[...] = decay * n + jnp.sum(w_col * k.astype(F32), axis=0, keepdims=True)
    m_ref[...] = m_new
    b_ref[...] = f_last
    return jnp.concatenate(hid, axis=1)


def _mlstm_finish(hid, xc, z, og, skip):
    hc = hid - jnp.mean(hid, axis=-1, keepdims=True)
    hn = hc * lax.rsqrt(jnp.mean(hc * hc, axis=-1, keepdims=True) + EPS) * og
    return ((hn + skip * xc.astype(F32)) * _silu(z.astype(F32))).astype(BF16)


def _mlstm_kernel(*refs, lead, chunk, n_chunks, n_heads, has_state, merge_lead, head_axis):
    xm_ref, xc_ref, z_ref, wq_ref, wk_ref, wv_ref = refs[:6]
    pos = 6
    if lead:
        gl_ref = refs[pos]
        pos += 1
    gm_ref, gc_ref, og_ref, skip_ref = refs[pos:pos + 4]
    pos += 4
    if has_state:
        c0_ref, n0_ref, m0_ref = refs[pos:pos + 3]
        pos += 3
    pos += 1
    o_ref, C_ref, n_ref, m_ref, b_ref, hid_ref = refs[pos:pos + 6]
    C_st, n_st, m_st = C_ref.at[0, 0, 0], n_ref.at[0, 0], m_ref.at[0, 0]
    if has_state:
        C_st[...] = c0_ref[0, 0, 0]
        n_st[...] = n0_ref[0, 0, 0]
        m_st[...] = m0_ref[0, 0, 0]
    else:
        C_st[...] = jnp.zeros(C_st.shape, F32)
        n_st[...] = jnp.zeros(n_st.shape, F32)
        m_st[...] = jnp.zeros(m_st.shape, F32)
    b_ref[...] = jnp.zeros(b_ref.shape, F32)
    head = pl.program_id(head_axis)

    def chunk_rows(c):
        return pl.ds(pl.multiple_of(lead + c * chunk, BF16_ROWS), chunk)

    def core(rows, g_rows):
        return _mlstm_core(xm_ref[0, rows, :], xc_ref[0, rows, :], wq_ref[0], wk_ref[0], wv_ref[0],
                           g_rows, gc_ref[0, rows, :], head, n_heads, C_st, n_st, m_st, b_ref)

    def finish(rows, hid):
        o_ref[0, rows, :] = _mlstm_finish(hid, xc_ref[0, rows, :], z_ref[0, rows, :], og_ref[0],
                                          skip_ref[0])

    if merge_lead:
        first = pl.ds(0, lead + chunk)
        finish(first, core(first, gl_ref[0, 0]))
        hid_ref[0] = core(chunk_rows(1), gm_ref[0, 0, 1])

        def pair(p, carry):
            c = 2 * p + 2
            hid_ref[1] = core(chunk_rows(c), gm_ref[0, 0, c])
            finish(chunk_rows(c - 1), hid_ref[0])
            hid_ref[0] = core(chunk_rows(c + 1), gm_ref[0, 0, c + 1])
            finish(chunk_rows(c), hid_ref[1])
            return carry

        lax.fori_loop(0, n_chunks // 2 - 1, pair, 0)
        finish(chunk_rows(n_chunks - 1), hid_ref[0])
    else:
        if lead:
            finish(pl.ds(0, lead), core(pl.ds(0, lead), gl_ref[0, 0]))

        def body(c, carry):
            finish(chunk_rows(c), core(chunk_rows(c), gm_ref[0, 0, c]))
            return carry

        lax.fori_loop(0, n_chunks, body, 0, unroll=_tile(n_chunks, 2, 1))


def _mlstm(xm, z, z_head0, xc, gates, wq, wk, wv, out_g, skip, state, lead, chunk, c_stack, layer,
           n_layers):
    B, S, AI = xc.shape
    H, DH = out_g.shape
    n_chunks = (S - lead) // chunk
    assert lead + n_chunks * chunk == S
    g = gates.reshape(B, S, 2, H).transpose(0, 3, 2, 1)
    g_main = g[..., lead:].reshape(B, H, 2, n_chunks, chunk).transpose(0, 1, 3, 2, 4)
    has_state = state is not None
    heads_outer = has_state

    def spec(shape, index):
        return pl.BlockSpec(shape, (lambda h, b: index(b, h)) if heads_outer else index)

    act_spec = spec((1, S, DH), lambda b, h: (b, 0, h))
    w_spec = spec((1, DH, DH), lambda b, h: (h, 0, 0))
    in_specs = [act_spec, act_spec, spec((1, S, DH), lambda b, h: (b, 0, z_head0 + h)),
                w_spec, w_spec, w_spec]
    args = [xm, xc, z, wq, wk, wv]
    merge_lead = bool(lead) and n_chunks % 2 == 0
    if lead:
        n_first = lead + chunk if merge_lead else lead
        in_specs.append(spec((1, 1, 2, n_first), lambda b, h: (b, h, 0, 0)))
        args.append(g[..., :n_first])
    in_specs += [spec((1, 1, n_chunks, 2, chunk), lambda b, h: (b, h, 0, 0, 0)),
                 spec((1, S, 2 * H), lambda b, h: (b, 0, 0)),
                 spec((1, 1, DH), lambda b, h: (h, 0, 0)),
                 spec((1, 1, DH), lambda b, h: (h, 0, 0))]
    args += [g_main, gates, out_g.reshape(H, 1, DH).astype(F32), skip.reshape(H, 1, DH).astype(F32)]
    n_spec = spec((1, 1, 1, DH), lambda b, h: (b, h, 0, 0))
    m_spec = spec((1, 1, 1, 1), lambda b, h: (b, h, 0, 0))
    if has_state:
        C0, n0, m0 = state
        NL = C0.shape[0]
        in_specs += [spec((1, 1, 1, DH, DH), lambda b, h: (layer, b, h, 0, 0)),
                     spec((1, 1, 1, 1, DH), lambda b, h: (layer, b, h, 0, 0)),
                     spec((1, 1, 1, 1, 1), lambda b, h: (layer, b, h, 0, 0))]
        args += [C0.astype(F32), n0.reshape(NL, B, H, 1, DH).astype(F32),
                 m0.reshape(NL, B, H, 1, 1).astype(F32)]
    if c_stack is None:
        c_stack = pl.empty((n_layers, B, H, DH, DH), F32)
    aliases = {len(args): 1}
    in_specs.append(pl.BlockSpec(memory_space=pl.ANY))
    args.append(c_stack)
    kern = functools.partial(_mlstm_kernel, lead=lead, chunk=chunk, n_chunks=n_chunks, n_heads=H,
                             has_state=has_state, merge_lead=merge_lead,
                             head_axis=0 if heads_outer else 1)
    hz, C, n, m = pl.pallas_call(
        kern, grid=(H, B) if heads_outer else (B, H), in_specs=in_specs,
        out_specs=[act_spec,
                   spec((1, 1, 1, DH, DH), lambda b, h: (layer, b, h, 0, 0)),
                   n_spec, m_spec],
        out_shape=[jax.ShapeDtypeStruct((B, S, AI), BF16),
                   jax.ShapeDtypeStruct((n_layers, B, H, DH, DH), F32),
                   jax.ShapeDtypeStruct((B, H, 1, DH), F32),
                   jax.ShapeDtypeStruct((B, H, 1, 1), F32)],
        scratch_shapes=[pltpu.VMEM((1, 1), F32), pltpu.VMEM((2, chunk, DH), F32)],
        input_output_aliases=aliases,
        compiler_params=_params("parallel", "parallel"), name="mlstm",
    )(*args)
    return hz, C, n.reshape(B, H, DH), m.reshape(B, H)


def _cumsum_kernel(x_ref, o_ref, *, scale, first_lane):
    x = x_ref[0]
    S = x.shape[0]
    summed = lax.broadcasted_iota(jnp.int32, x.shape, 1) >= first_lane
    row = lax.broadcasted_iota(jnp.int32, x.shape, 0)
    y = jnp.where(summed, x, 0.0)
    step = 1
    while step < S:
        y = y + jnp.where(row >= step, pltpu.roll(y, step, 0), 0.0)
        step *= 2
    o_ref[0] = jnp.where(summed, y * scale, x)


def _cumsum(x, scale=1.0, first_lane=0):
    B, S, G = x.shape
    spec = pl.BlockSpec((1, S, G), lambda b: (b, 0, 0))
    return pl.pallas_call(
        functools.partial(_cumsum_kernel, scale=scale, first_lane=first_lane),
        grid=(B,), in_specs=[spec], out_specs=spec, out_shape=jax.ShapeDtypeStruct((B, S, G), F32),
        compiler_params=_params("parallel"), name="cumsum",
    )(x)


def _lane_fold(op, acc, x):
    for c in range(x.shape[1] // LANES):
        acc = op(acc, x[:, c * LANES:(c + 1) * LANES])
    return acc


def _fox_prompt_kernel(q_ref, z_ref, k_ref, v_ref, fq_ref, fkl_ref, fkm_ref, o_ref,
                       s_ref, sl_ref, fq_rep, m_ref, l_ref, acc_ref, *, lead, blk, n_blk, heads,
                       dim):
    n_lane = blk // LANES
    cols = [slice(hh * dim, (hh + 1) * dim) for hh in range(heads)]

    def key_rows(j):
        return pl.ds(pl.multiple_of(lead + j * blk, BF16_ROWS), blk)

    def tiled(x):
        return jnp.concatenate([x] * n_lane, axis=1)

    def q_block(i, carry):
        qrows = pl.ds(pl.multiple_of(i * blk, blk), blk)

        def logits(hh, j):
            return (_dot_nt(q_ref[0, qrows, cols[hh]], k_ref[0, key_rows(j), cols[hh]])
                    + tiled(fq_rep[hh]) - fkm_ref[0, 0, j, hh:hh + 1, :])

        for hh in range(heads):
            fq_rep[hh] = jnp.broadcast_to(fq_ref[0, 0, qrows, hh:hh + 1], (blk, LANES))
        if lead:
            for hh in range(heads):
                s = (_dot_nt(q_ref[0, qrows, cols[hh]], k_ref[0, 0:LANES, cols[hh]])
                     + fq_rep[hh] - fkl_ref[0, 0, hh:hh + 1, :])
                sl_ref[hh] = s
                m_ref[hh] = s
        else:
            m_ref[...] = jnp.full(m_ref.shape, NEG_INF, F32)

        def pass1(j, c):
            for hh in range(heads):
                s = logits(hh, j)
                s_ref[hh, j] = s
                m_ref[hh] = _lane_fold(jnp.maximum, m_ref[hh], s)
            return c

        lax.fori_loop(0, i, pass1, 0)
        row = lax.broadcasted_iota(jnp.int32, (blk, blk), 0)
        col = lax.broadcasted_iota(jnp.int32, (blk, blk), 1)
        for hh in range(heads):
            s = jnp.where(col <= row, logits(hh, i), NEG_INF)
            s_ref[hh, i] = s
            m = jnp.max(_lane_fold(jnp.maximum, m_ref[hh], s), axis=-1, keepdims=True)
            m_ref[hh] = jnp.broadcast_to(m, (blk, LANES))
        if lead:
            for hh in range(heads):
                p = jnp.exp2(sl_ref[hh] - m_ref[hh])
                l_ref[hh] = p
                acc_ref[hh] = _dot(p.astype(BF16), v_ref[0, 0:LANES, cols[hh]])
        else:
            l_ref[...] = jnp.zeros(l_ref.shape, F32)
            acc_ref[...] = jnp.zeros(acc_ref.shape, F32)

        def pass2(j, c):
            for hh in range(heads):
                p = jnp.exp2(s_ref[hh, j] - tiled(m_ref[hh]))
                l_ref[hh] = _lane_fold(jnp.add, l_ref[hh], p)
                acc_ref[hh] += _dot(p.astype(BF16), v_ref[0, key_rows(j), cols[hh]])
            return c

        lax.fori_loop(0, i + 1, pass2, 0)
        for hh in range(heads):
            l = jnp.sum(l_ref[hh], axis=-1, keepdims=True)
            o_ref[0, qrows, cols[hh]] = (acc_ref[hh] * (1.0 / l)
                                         * _silu(z_ref[0, qrows, cols[hh]].astype(F32))).astype(BF16)
        return carry

    lax.fori_loop(0, n_blk, q_block, 0)


def _fox_prompt(q, z, kb, vb, f_cum, lead, heads_per_step=4, blk=512):
    B, SQ, W = q.shape
    NH = f_cum.shape[-1]
    dim = W // NH
    hg = min(heads_per_step, NH)
    G = NH // hg
    blk = _tile(SQ, blk, LANES)
    n_blk = SQ // blk
    SK = lead + SQ
    assert lead <= LANES <= SK
    fq = f_cum[:, lead:].reshape(B, SQ, G, hg).transpose(0, 2, 1, 3)
    fk = f_cum.transpose(0, 2, 1).reshape(B, G, hg, SK)
    fk_lead = jnp.pad(fk[..., :lead], ((0, 0), (0, 0), (0, 0), (0, LANES - lead)),
                      constant_values=float("inf"))
    fk_main = fk[..., lead:].reshape(B, G, hg, n_blk, blk).transpose(0, 1, 3, 2, 4)
    wcols = hg * dim
    kern = functools.partial(_fox_prompt_kernel, lead=lead, blk=blk, n_blk=n_blk, heads=hg, dim=dim)
    return pl.pallas_call(
        kern, grid=(B, G),
        in_specs=[pl.BlockSpec((1, SQ, wcols), lambda b, g: (b, 0, g)),
                  pl.BlockSpec((1, SQ, wcols), lambda b, g: (b, 0, g)),
                  pl.BlockSpec((1, SK, wcols), lambda b, g: (b, 0, g)),
                  pl.BlockSpec((1, SK, wcols), lambda b, g: (b, 0, g)),
                  pl.BlockSpec((1, 1, SQ, hg), lambda b, g: (b, g, 0, 0)),
                  pl.BlockSpec((1, 1, hg, LANES), lambda b, g: (b, g, 0, 0)),
                  pl.BlockSpec((1, 1, n_blk, hg, blk), lambda b, g: (b, g, 0, 0, 0))],
        out_specs=pl.BlockSpec((1, SQ, wcols), lambda b, g: (b, 0, g)),
        out_shape=jax.ShapeDtypeStruct((B, SQ, W), BF16),
        scratch_shapes=[pltpu.VMEM((hg, n_blk, blk, blk), F32)]
        + [pltpu.VMEM((hg, blk, LANES), F32)] * 4 + [pltpu.VMEM((hg, blk, dim), F32)],
        compiler_params=_params("parallel", "parallel"), name="fox_prompt",
    )(q, z, kb, vb, fq, fk_lead, fk_main)


def _pack_heads_kernel(x_ref, o_ref):
    o_ref[0] = pltpu.einshape("khd->k(hd)", x_ref[0]).astype(o_ref.dtype)


def _pack_heads(x, dtype):
    B, P, NH, dim = x.shape
    rows = _tile(P, 512)
    return pl.pallas_call(
        _pack_heads_kernel, grid=(B, P // rows),
        in_specs=[pl.BlockSpec((1, rows, NH, dim), lambda b, i: (b, i, 0, 0))],
        out_specs=pl.BlockSpec((1, rows, NH * dim), lambda b, i: (b, i, 0)),
        out_shape=jax.ShapeDtypeStruct((B, P, NH * dim), dtype),
        compiler_params=_params("parallel", "parallel"), name="pack_heads",
    )(x)


def _fox_decode_kernel(q_ref, z_ref, ck_ref, cv_ref, k_ref, v_ref, fq_ref, fkc_ref, fkn_ref, o_ref,
                       *, heads, dim):
    Q = q_ref.shape[1]
    row = lax.broadcasted_iota(jnp.int32, (Q, Q), 0)
    col = lax.broadcasted_iota(jnp.int32, (Q, Q), 1)
    outs = []
    for hh in range(heads):
        cs = slice(hh * dim, (hh + 1) * dim)
        q = q_ref[0, :, cs]
        fq = fq_ref[0, 0, :, hh:hh + 1]
        s_c = _dot_nt(q, ck_ref[0, :, cs].astype(BF16)) + fq - fkc_ref[0, 0, hh:hh + 1, :]
        s_n = jnp.where(col <= row, _dot_nt(q, k_ref[0, :, cs]) + fq - fkn_ref[0, 0, hh:hh + 1, :],
                        NEG_INF)
        m = jnp.maximum(jnp.max(s_c, axis=-1, keepdims=True), jnp.max(s_n, axis=-1, keepdims=True))
        p_c = jnp.exp2(s_c - m)
        p_n = jnp.exp2(s_n - m)
        l = jnp.sum(p_c, axis=-1, keepdims=True) + jnp.sum(p_n, axis=-1, keepdims=True)
        acc = _dot(p_c.astype(BF16), cv_ref[0, :, cs].astype(BF16)) + _dot(p_n.astype(BF16), v_ref[0, :, cs])
        outs.append((acc * (1.0 / l) * _silu(z_ref[0, :, cs].astype(F32))).astype(BF16))
    o_ref[0] = jnp.concatenate(outs, axis=-1)


def _fox_decode(q, z, cache_k, cache_v, kb, vb, f_cum, heads_per_step=4):
    B, Q, W = q.shape
    P = cache_k.shape[1]
    NH = f_cum.shape[-1]
    dim = W // NH
    hg = min(heads_per_step, NH)
    G = NH // hg
    fq = f_cum[:, P:].reshape(B, Q, G, hg).transpose(0, 2, 1, 3)
    fk = f_cum.transpose(0, 2, 1).reshape(B, G, hg, P + Q)
    wcols = hg * dim
    kern = functools.partial(_fox_decode_kernel, heads=hg, dim=dim)
    return pl.pallas_call(
        kern, grid=(B, G),
        in_specs=[pl.BlockSpec((1, Q, wcols), lambda b, g: (b, 0, g)),
                  pl.BlockSpec((1, Q, wcols), lambda b, g: (b, 0, g)),
                  pl.BlockSpec((1, P, wcols), lambda b, g: (b, 0, g)),
                  pl.BlockSpec((1, P, wcols), lambda b, g: (b, 0, g)),
                  pl.BlockSpec((1, Q, wcols), lambda b, g: (b, 0, g)),
                  pl.BlockSpec((1, Q, wcols), lambda b, g: (b, 0, g)),
                  pl.BlockSpec((1, 1, Q, hg), lambda b, g: (b, g, 0, 0)),
                  pl.BlockSpec((1, 1, hg, P), lambda b, g: (b, g, 0, 0)),
                  pl.BlockSpec((1, 1, hg, Q), lambda b, g: (b, g, 0, 0))],
        out_specs=pl.BlockSpec((1, Q, wcols), lambda b, g: (b, 0, g)),
        out_shape=jax.ShapeDtypeStruct((B, Q, W), BF16),
        compiler_params=_params("parallel", "parallel"), name="fox_decode",
    )(q, z, cache_k, cache_v, kb, vb, fq, fk[..., :P], fk[..., P:])


def _mlstm_layer(x, conv_hist, state, w, lead, chunk, c_stack, layer, n_layers):
    (norm_g, w_in, conv_w, conv_b, wq, wk, wv, w_xc, w_xm, b_gate, out_g, skip, w_out) = w
    B, S, _ = x.shape
    H, DH, _ = wq.shape
    AI = H * DH
    CW = conv_w.shape[0]
    assert S >= CW - 1 and CW - 1 <= CONV_PAD
    buf = jnp.zeros((B, CONV_PAD, AI), F32)
    if conv_hist is not None:
        buf = buf.at[:, CONV_PAD - (CW - 1):].set(conv_hist.astype(F32))
    if S >= MIN_FUSED_CONV_ROWS:
        wb = w_in.astype(BF16)
        xm, z, xc, gates = _in_conv(x, norm_g, wb, buf, conv_w, conv_b,
                                    w_xc.astype(BF16), w_xm.astype(BF16), b_gate)
        gates = _cumsum(gates, first_lane=H)
        z_head0 = 0
    else:
        xm = _norm_matmul(x, norm_g, w_in.astype(BF16))
        xc, gates = _conv_gates(xm, buf, conv_w, conv_b, w_xc.astype(BF16), w_xm.astype(BF16),
                                b_gate)
        z, z_head0 = xm, H
    hz, c_stack, n, m = _mlstm(
        xm, z, z_head0, xc, gates, wq.astype(BF16), (wk * float(DH) ** -0.5).astype(BF16),
        wv.astype(BF16), out_g, skip.reshape(H, DH), state, lead, chunk, c_stack, layer, n_layers)
    x_new = _matmul_residual(hz, w_out.astype(BF16), x)
    new_hist = xm[:, S - (CW - 1):, :AI].astype(F32)
    return x_new, new_hist, c_stack, n, m


def _shared_kv(x, kv_norm_g, w_kvf, b_f, k_norm_g):
    NH, dim = k_norm_g.shape
    W = NH * dim
    wb = w_kvf.astype(BF16)
    k32, kb, v32, vb, logf = _dual_norm_matmul(
        x, kv_norm_g, wb, W, k_norm_g.reshape(1, W), a_dtypes=[F32, BF16],
        b_dtypes=[F32, BF16], head_dim=dim, split_heads_first=True, gate_w=wb[:, 2 * W:],
        gate_b=b_f)
    return k32, kb, v32, vb, logf


def _fox_in(x, norm_g, w_in, q_norm_g):
    NH, dim = q_norm_g.shape
    W = NH * dim
    wb = w_in.astype(BF16)
    gain = q_norm_g.reshape(1, W).astype(F32) * (float(dim) ** -0.5 * LOG2E)
    q, z = _dual_norm_matmul(x, norm_g, wb, W, gain, a_dtypes=[BF16],
                             b_dtypes=[BF16], head_dim=dim)
    return q, z


def kernel(x_prompt, x_sample, cache_k, cache_v, cache_logf, state_C, state_n, state_m, state_conv,
           meta_tokens, a_norm_g, a_w_in, a_conv_w, a_conv_b, a_wq, a_wk, a_wv, a_w_gate, a_b_gate,
           a_out_g, a_skip, a_w_out, kv_norm_g, w_kvf, b_f, k_norm_g, b_norm_g, b_w_in, q_norm_g,
           b_w_out):
    B, SEQ, D = x_prompt.shape
    DB, DEC, _ = x_sample.shape
    NM = meta_tokens.shape[0]
    N_A = a_norm_g.shape[0]
    N_B = b_norm_g.shape[0]
    NH, dim = k_norm_g.shape
    W = NH * dim
    P = cache_k.shape[1]
    chunk = _tile(SEQ, 256)

    xp = jnp.concatenate(
        [jnp.broadcast_to(meta_tokens.astype(x_prompt.dtype)[None], (B, NM, D)), x_prompt], axis=1)
    xs = x_sample
    p_state, s_state = [], []
    p_C = s_C = None
    w_xc, w_xm = _fold_gate_weights(a_wq, a_wk, a_wv, a_w_gate)
    for i in range(N_A):
        w = (a_norm_g[i], a_w_in[i], a_conv_w[i], a_conv_b[i], a_wq[i], a_wk[i], a_wv[i],
             w_xc[i], w_xm[i], a_b_gate[i], a_out_g[i], a_skip[i], a_w_out[i])
        xp, hist, p_C, n, m = _mlstm_layer(xp, None, None, w, NM, chunk, p_C, i, N_A)
        p_state.append((hist, n, m))
        xs, hist, s_C, n, m = _mlstm_layer(xs, state_conv[i], (state_C, state_n, state_m),
                                           w, 0, DEC, s_C, i, N_A)
        s_state.append((hist, n, m))

    pk32, pkb, pv32, pvb, p_logf = _shared_kv(xp, kv_norm_g, w_kvf, b_f, k_norm_g)
    sk32, skb, sv32, svb, s_logf = _shared_kv(xs, kv_norm_g, w_kvf, b_f, k_norm_g)
    fp = _cumsum(p_logf, LOG2E)
    fs = _cumsum(jnp.concatenate([cache_logf.astype(F32), s_logf], axis=1), LOG2E)
    ck = _pack_heads(cache_k, BF16)
    cv = _pack_heads(cache_v, BF16)
    xp = xp[:, NM:]
    for j in range(N_B):
        q, z = _fox_in(xp, b_norm_g[j], b_w_in[j], q_norm_g[j])
        xp = _matmul_residual(_fox_prompt(q, z, pkb, pvb, fp, NM), b_w_out[j].astype(BF16), xp)
        q, z = _fox_in(xs, b_norm_g[j], b_w_in[j], q_norm_g[j])
        xs = _matmul_residual(_fox_decode(q, z, ck, cv, skb, svb, fs), b_w_out[j].astype(BF16), xs)

    def stack(states, idx):
        return jnp.stack([st[idx] for st in states])

    return (xp, xs, pk32, pv32, p_logf,
            p_C, stack(p_state, 1), stack(p_state, 2), stack(p_state, 0),
            sk32, sv32, s_logf,
            s_C, stack(s_state, 1), stack(s_state, 2), stack(s_state, 0))
```
